```python
import jax, jax.numpy as jnp
from jax import lax
import numpy as np

D_MODEL = 2048
BATCH = 4
SEQ = 2048
DEPTH = 1

D_FF = 5632
PLE_DIM = 256
HGRN_WIDTH = D_MODEL // 2
HGRN_HEAD_DIM = 128
HGRN_HEADS = HGRN_WIDTH // HGRN_HEAD_DIM
CHUNK = 64
LRU_WIDTH = D_MODEL - HGRN_WIDTH
LRU_BLOCKS = 8
LRU_BLOCK_DIM = LRU_WIDTH // LRU_BLOCKS
CONV_WIDTH = 4
LRU_C = 8.0
EPS = 1e-6
IN_COLS = 4 * HGRN_WIDTH + 2 * LRU_WIDTH

kernel_name = "hymba_hgrn2_rglru_macaron_block"


def rmsnorm(x, g):
    xf = x.astype(jnp.float32)
    y = xf * lax.rsqrt(jnp.mean(xf * xf, axis=-1, keepdims=True) + EPS)
    return (y * g.astype(jnp.float32)).astype(x.dtype)


def swiglu(h, w_gate, w_up, w_down):
    return (jax.nn.silu(h @ w_gate) * (h @ w_up)) @ w_down


def hgrn2_chunkwise(q, k, v, logf):
    Bn, S, H, dk = q.shape
    dv = v.shape[-1]
    n = S // CHUNK

    def to_chunks(t):
        return t.reshape(Bn, n, CHUNK, H, t.shape[-1]).transpose(1, 0, 3, 2, 4)

    causal = jnp.tril(jnp.ones((CHUNK, CHUNK), dtype=bool))

    def step(state, inp):
        qc, kc, vc, lf = inp
        b = jnp.cumsum(lf, axis=2)
        o_inter = jnp.einsum('bhtk,bhkv->bhtv', qc * jnp.exp(b), state)
        diff = b[:, :, :, None, :] - b[:, :, None, :, :]
        decay = jnp.exp(jnp.where(causal[:, :, None], diff, -jnp.inf))
        scores = jnp.einsum('bhtk,bhtsk,bhsk->bhts', qc, decay, kc)
        o = o_inter + jnp.einsum('bhts,bhsv->bhtv', scores, vc)
        b_last = b[:, :, -1:, :]
        new_state = jnp.exp(b_last[:, :, 0, :])[..., None] * state + jnp.einsum(
            'bhsk,bhsv->bhkv', kc * jnp.exp(b_last - b), vc)
        return new_state, o

    s0 = jnp.zeros((Bn, H, dk, dv), jnp.float32)
    _, o = lax.scan(step, s0, (to_chunks(q), to_chunks(k), to_chunks(v), to_chunks(logf)))
    return o.transpose(1, 0, 3, 2, 4).reshape(Bn, S, H, dv)


def hgrn2_group(q, f_pre, i, g, lb, g_norm):
    Bn, S, _ = q.shape
    shp = (Bn, S, HGRN_HEADS, HGRN_HEAD_DIM)
    f = lb + (1.0 - lb) * jax.nn.sigmoid(f_pre.astype(jnp.float32))
    logf = jnp.log(f)
    k = 1.0 - f
    o = hgrn2_chunkwise(q.astype(jnp.float32).reshape(shp), k.reshape(shp),
                        i.astype(jnp.float32).reshape(shp), logf.reshape(shp))
    o = rmsnorm(o, g_norm) * jax.nn.silu(g.astype(jnp.float32).reshape(shp))
    return o.reshape(Bn, S, HGRN_WIDTH).astype(q.dtype)


def rglru_group(xb, gate, conv_w, conv_b, w_a, b_a, w_x, b_x, lam):
    Bn, S, _ = xb.shape
    xf = xb.astype(jnp.float32)
    xc = lax.conv_general_dilated(
        xf, conv_w.astype(jnp.float32)[:, None, :], window_strides=(1,),
        padding=[(CONV_WIDTH - 1, 0)], dimension_numbers=('NWC', 'WIO', 'NWC'),
        feature_group_count=LRU_WIDTH) + conv_b.astype(jnp.float32)
    xr = xc.reshape(Bn, S, LRU_BLOCKS, LRU_BLOCK_DIM)
    blk = (LRU_BLOCKS, LRU_BLOCK_DIM)
    r = jax.nn.sigmoid(jnp.einsum('bsnc,ncd->bsnd', xr, w_a.astype(jnp.float32))
                       + b_a.astype(jnp.float32).reshape(blk))
    ig = jax.nn.sigmoid(jnp.einsum('bsnc,ncd->bsnd', xr, w_x.astype(jnp.float32))
                        + b_x.astype(jnp.float32).reshape(blk))
    log_a = -LRU_C * r * jax.nn.softplus(-lam.astype(jnp.float32).reshape(blk))
    a = jnp.exp(log_a)
    u = jnp.sqrt(-jnp.expm1(2.0 * log_a)) * (ig * xr)

    def combine(c1, c2):
        a1, b1 = c1
        a2, b2 = c2
        return a1 * a2, a2 * b1 + b2

    _, h = lax.associative_scan(combine, (a, u), axis=1)
    y = h.reshape(Bn, S, LRU_WIDTH) * jax.nn.gelu(gate.astype(jnp.float32))
    return y.astype(xb.dtype)


def setup_inputs(seed: int = 0) -> dict:
    key = jax.random.key(seed)
    ks = jax.random.split(key, 32)
    f32 = jnp.float32

    def nrm(k, shape, fan_in):
        return jax.random.normal(k, shape, f32) * (fan_in ** -0.5)

    def gain(k, shape):
        return 1.0 + 0.05 * jax.random.normal(k, shape, f32)

    def small(k, shape):
        return 0.01 * jax.random.normal(k, shape, f32)

    a8 = jax.random.uniform(ks[20], (DEPTH, LRU_WIDTH), f32, 0.9, 0.999)
    a_base = a8 ** (1.0 / LRU_C)
    lam = jnp.log(a_base) - jnp.log1p(-a_base)

    return {
        "x": jax.random.normal(ks[0], (BATCH, SEQ, D_MODEL), f32),
        "p": jax.random.normal(ks[1], (DEPTH, BATCH, SEQ, PLE_DIM), f32),
        "ffn1_norm": gain(ks[2], (DEPTH, D_MODEL)),
        "ffn1_w_gate": nrm(ks[3], (DEPTH, D_MODEL, D_FF), D_MODEL),
        "ffn1_w_up": nrm(ks[4], (DEPTH, D_MODEL, D_FF), D_MODEL),
        "ffn1_w_down": nrm(ks[5], (DEPTH, D_FF, D_MODEL), D_FF),
        "mix_norm": gain(ks[6], (DEPTH, D_MODEL)),
        "w_in": nrm(ks[7], (DEPTH, D_MODEL, IN_COLS), D_MODEL),
        "hgrn_lower_bounds": 0.5 * jax.random.normal(ks[8], (DEPTH + 1, HGRN_WIDTH), f32),
        "hgrn_g_norm": gain(ks[9], (DEPTH, HGRN_HEAD_DIM)),
        "conv_w": nrm(ks[10], (DEPTH, CONV_WIDTH, LRU_WIDTH), CONV_WIDTH),
        "conv_b": small(ks[11], (DEPTH, LRU_WIDTH)),
        "lru_w_a": nrm(ks[12], (DEPTH, LRU_BLOCKS, LRU_BLOCK_DIM, LRU_BLOCK_DIM), LRU_BLOCK_DIM),
        "lru_b_a": small(ks[13], (DEPTH, LRU_WIDTH)),
        "lru_w_x": nrm(ks[14], (DEPTH, LRU_BLOCKS, LRU_BLOCK_DIM, LRU_BLOCK_DIM), LRU_BLOCK_DIM),
        "lru_b_x": small(ks[15], (DEPTH, LRU_WIDTH)),
        "lru_lambda": lam,
        "w_out": nrm(ks[16], (DEPTH, D_MODEL, D_MODEL), D_MODEL),
        "ffn2_norm": gain(ks[17], (DEPTH, D_MODEL)),
        "ffn2_w_gate": nrm(ks[18], (DEPTH, D_MODEL, D_FF), D_MODEL),
        "ffn2_w_up": nrm(ks[19], (DEPTH, D_MODEL, D_FF), D_MODEL),
        "ffn2_w_down": nrm(ks[21], (DEPTH, D_FF, D_MODEL), D_FF),
        "ple_norm": gain(ks[22], (DEPTH, D_MODEL)),
        "ple_w_gate": nrm(ks[23], (DEPTH, D_MODEL, D_MODEL), D_MODEL),
        "ple_b_gate": small(ks[24], (DEPTH, D_MODEL)),
        "ple_w_proj": nrm(ks[25], (DEPTH, PLE_DIM, D_MODEL), PLE_DIM),
        "final_norm": gain(ks[26], (D_MODEL,)),
    }


def reference(x, p, ffn1_norm, ffn1_w_gate, ffn1_w_up, ffn1_w_down, mix_norm, w_in,
              hgrn_lower_bounds, hgrn_g_norm, conv_w, conv_b, lru_w_a, lru_b_a, lru_w_x,
              lru_b_x, lru_lambda, w_out, ffn2_norm, ffn2_w_gate, ffn2_w_up, ffn2_w_down,
              ple_norm, ple_w_gate, ple_b_gate, ple_w_proj, final_norm):
    lb_all = jnp.cumsum(jax.nn.softmax(hgrn_lower_bounds.astype(jnp.float32), axis=0), axis=0)
    H1, H2, H3, H4 = HGRN_WIDTH, 2 * HGRN_WIDTH, 3 * HGRN_WIDTH, 4 * HGRN_WIDTH
    for l in range(DEPTH):
        h = rmsnorm(x, ffn1_norm[l])
        x = x + 0.5 * swiglu(h, ffn1_w_gate[l], ffn1_w_up[l], ffn1_w_down[l])

        h = rmsnorm(x, mix_norm[l])
        z = h @ w_in[l]
        o_hgrn = hgrn2_group(z[..., :H1], z[..., H1:H2], z[..., H2:H3], z[..., H3:H4],
                             lb_all[l], hgrn_g_norm[l])
        o_lru = rglru_group(z[..., H4:H4 + LRU_WIDTH], z[..., H4 + LRU_WIDTH:],
                            conv_w[l], conv_b[l], lru_w_a[l], lru_b_a[l],
                            lru_w_x[l], lru_b_x[l], lru_lambda[l])
        x = x + jnp.concatenate([o_hgrn, o_lru], axis=-1) @ w_out[l]

        h = rmsnorm(x, ffn2_norm[l])
        x = x + 0.5 * swiglu(h, ffn2_w_gate[l], ffn2_w_up[l], ffn2_w_down[l])

        h = rmsnorm(x, ple_norm[l])
        gate = jax.nn.sigmoid(h @ ple_w_gate[l] + ple_b_gate[l])
        x = x + gate * (p[l] @ ple_w_proj[l])
    return rmsnorm(x, final_norm)
```

```python
import functools

import jax
import jax.numpy as jnp
from jax import lax
from jax.experimental import pallas as pl
from jax.experimental.pallas import tpu as pltpu

F32 = jnp.float32
BF16 = jnp.bfloat16

D_MODEL = 2048
D_FF = 5632
PLE_DIM = 256
HGRN_WIDTH = D_MODEL // 2
HEAD_DIM = 128
HGRN_HEADS = HGRN_WIDTH // HEAD_DIM
LRU_WIDTH = D_MODEL - HGRN_WIDTH
LRU_BLOCKS = 8
LRU_BLOCK_DIM = LRU_WIDTH // LRU_BLOCKS
CONV_WIDTH = 4
LRU_C = 8.0
EPS = 1e-6

V7X_VMEM_BYTES = 64 * 1024 * 1024
SUBLANES = 8
LANES = 128

FFN_TM = 512
FFN_TF = 512
PROJ_TM = 1024
PROJ_TN = 512
ROW_TM = 512
HGRN_CHUNK = 64
HGRN_SUB = 16
HGRN_HB = 2
LRU_TS = 256


def _vmem_limit(nbytes):
    return int(min(V7X_VMEM_BYTES - (4 << 20), nbytes + nbytes // 4 + (2 << 20)))


def _rmsnorm(xf, g):
    ms = jnp.mean(xf * xf, axis=-1, keepdims=True)
    return xf * lax.rsqrt(ms + EPS) * g


def _ffn_kernel(x_ref, g_ref, wg_ref, wu_ref, wd_ref, o_ref, h_ref):
    j = pl.program_id(1)

    @pl.when(j == 0)
    def _():
        h_ref[...] = _rmsnorm(x_ref[...], g_ref[...]).astype(BF16)

    h = h_ref[...]
    gate = jnp.dot(h, wg_ref[...], preferred_element_type=F32)
    up = jnp.dot(h, wu_ref[...], preferred_element_type=F32)
    act = (gate * jax.nn.sigmoid(gate) * up).astype(BF16)
    part = jnp.dot(act, wd_ref[...], preferred_element_type=F32)

    @pl.when(j == 0)
    def _():
        o_ref[...] = part

    @pl.when(j > 0)
    def _():
        o_ref[...] += part

    @pl.when(j == pl.num_programs(1) - 1)
    def _():
        o_ref[...] = x_ref[...] + 0.5 * o_ref[...]


def _ffn(x, g, wg, wu, wd):
    t, d = x.shape
    dff = wg.shape[1]
    tm, tf = FFN_TM, FFN_TF
    est = (2 * tm * d * 4) * 2 + tm * d * 2 + 2 * 3 * d * tf * 2 + 3 * tm * tf * 4
    return pl.pallas_call(
        _ffn_kernel,
        grid=(t // tm, dff // tf),
        in_specs=[
            pl.BlockSpec((tm, d), lambda i, j: (i, 0)),
            pl.BlockSpec((1, d), lambda i, j: (0, 0)),
            pl.BlockSpec((d, tf), lambda i, j: (0, j)),
            pl.BlockSpec((d, tf), lambda i, j: (0, j)),
            pl.BlockSpec((tf, d), lambda i, j: (j, 0)),
        ],
        out_specs=pl.BlockSpec((tm, d), lambda i, j: (i, 0)),
        out_shape=jax.ShapeDtypeStruct((t, d), F32),
        scratch_shapes=[pltpu.VMEM((tm, d), BF16)],
        compiler_params=pltpu.CompilerParams(
            dimension_semantics=("arbitrary", "arbitrary"),
            vmem_limit_bytes=_vmem_limit(est)),
    )(x, g, wg, wu, wd)


def _proj_kernel(x_ref, g_ref, w_ref, o_ref, h_ref):
    @pl.when(pl.program_id(1) == 0)
    def _():
        h_ref[...] = _rmsnorm(x_ref[...], g_ref[...]).astype(BF16)

    o_ref[...] = jnp.dot(h_ref[...], w_ref[...], preferred_element_type=F32)


def _in_proj(x, g, w):
    t, d = x.shape
    n = w.shape[1]
    tm, tn = PROJ_TM, PROJ_TN
    est = 2 * tm * d * 4 + tm * d * 2 + 2 * d * tn * 2 + 2 * tm * tn * 4
    return pl.pallas_call(
        _proj_kernel,
        grid=(t // tm, n // tn),
        in_specs=[
            pl.BlockSpec((tm, d), lambda i, j: (i, 0)),
            pl.BlockSpec((1, d), lambda i, j: (0, 0)),
            pl.BlockSpec((d, tn), lambda i, j: (0, j)),
        ],
        out_specs=pl.BlockSpec((tm, tn), lambda i, j: (i, j)),
        out_shape=jax.ShapeDtypeStruct((t, n), F32),
        scratch_shapes=[pltpu.VMEM((tm, d), BF16)],
        compiler_params=pltpu.CompilerParams(
            dimension_semantics=("arbitrary", "arbitrary"),
            vmem_limit_bytes=_vmem_limit(est)),
    )(x, g, w)


def _bcast_row(x, r):
    return jnp.broadcast_to(x[r:r + 1, :], x.shape)


def _hgrn_chunk(q, fp, v, gt, lb, gn, st):
    c, sub = HGRN_CHUNK, HGRN_SUB
    nb = c // sub
    assert nb == 4, "the two off-diagonal levels below are written for four sub-blocks"
    f = lb + (1.0 - lb) * jax.nn.sigmoid(fp)
    lf = jnp.log(f)
    kk = 1.0 - f

    row = lax.broadcasted_iota(jnp.int32, (c, HEAD_DIM), 0)
    rsub = row & (sub - 1)
    cs = lf
    d = 1
    while d < sub:
        cs = cs + jnp.where(rsub >= d, pltpu.roll(cs, d, axis=0), 0.0)
        d *= 2

    cb = [cs[i * sub:(i + 1) * sub] for i in range(nb)]
    tot = [_bcast_row(cb[i], sub - 1) for i in range(nb)]
    mid = [_bcast_row(cb[i], sub // 2 - 1) for i in range(nb)]

    pre = [None] * nb
    acc = None
    for i in range(nb):
        pre[i] = acc
        acc = tot[i] if acc is None else acc + tot[i]
    b_last = acc
    b = jnp.concatenate(
        [cb[i] if pre[i] is None else cb[i] + pre[i] for i in range(nb)], axis=0)
    b_last_c = jnp.concatenate([b_last] * nb, axis=0)

    zeros = jnp.zeros((sub, HEAD_DIM), F32)
    x32 = jnp.concatenate([tot[0] + tot[1] - cb[0], tot[1] - cb[1],
                           cb[2], cb[3] + tot[2]], axis=0)
    x16 = jnp.concatenate([tot[0] - cb[0], cb[1], tot[2] - cb[2], cb[3]], axis=0)
    xd = jnp.concatenate([cb[i] - mid[i] for i in range(nb)], axis=0)

    e32 = jnp.exp(x32)
    e16 = jnp.exp(x16)
    ed = jnp.exp(xd)
    edi = jnp.exp(-xd)

    upper = row >= (c // 2)
    odd = (row & sub) != 0
    qe32 = jnp.where(upper, q * e32, 0.0).astype(BF16)
    ke32 = jnp.where(upper, 0.0, kk * e32).astype(BF16)
    qe16 = jnp.where(odd, q * e16, 0.0).astype(BF16)
    ke16 = jnp.where(odd, 0.0, kk * e16).astype(BF16)
    qed = (q * ed).astype(BF16)
    ked = (kk * edi).astype(BF16)

    nt = (((1,), (1,)), ((), ()))
    s32 = lax.dot_general(qe32, ke32, nt, preferred_element_type=F32)
    s16 = lax.dot_general(qe16, ke16, nt, preferred_element_type=F32)
    sd = lax.dot_general(qed, ked, nt, preferred_element_type=F32)

    ti = lax.broadcasted_iota(jnp.int32, (c, c), 0)
    si = lax.broadcasted_iota(jnp.int32, (c, c), 1)
    same32 = (ti // (2 * sub)) == (si // (2 * sub))
    diag = ((ti // sub) == (si // sub)) & (si <= ti)
    scores = s32 + jnp.where(same32, s16, 0.0) + jnp.where(diag, sd, 0.0)

    vb = v.astype(BF16)
    qhat = (q * jnp.exp(b)).astype(BF16)
    khat = (kk * jnp.exp(b_last_c - b)).astype(BF16)
    o = (lax.dot_general(qhat, st.astype(BF16), nt, preferred_element_type=F32)
         + jnp.dot(scores.astype(BF16), vb, preferred_element_type=F32))

    tn = (((0,), (0,)), ((), ()))
    st_new = (jnp.exp(b_last[0:1, :]) * st
              + lax.dot_general(vb, khat, tn, preferred_element_type=F32))

    y = _rmsnorm(o, gn) * (gt * jax.nn.sigmoid(gt))
    return y, st_new


def _hgrn_kernel(q_ref, f_ref, i_ref, g_ref, lbr_ref, gn_ref, o_ref, st_ref):
    c = HGRN_CHUNK
    s = q_ref.shape[0]
    st_ref[...] = jnp.zeros_like(st_ref)

    lbr = lbr_ref[...]
    mx = jnp.max(lbr, axis=0, keepdims=True)
    ex = jnp.exp(lbr - mx)
    lb_all = ex[0:1, :] / jnp.sum(ex, axis=0, keepdims=True)
    gn = gn_ref[...]

    def body(ci, carry):
        rows = pl.ds(pl.multiple_of(ci * c, c), c)
        for hh in range(HGRN_HB):
            cols = slice(hh * HEAD_DIM, (hh + 1) * HEAD_DIM)
            y, st_new = _hgrn_chunk(
                q_ref[rows, cols], f_ref[rows, cols], i_ref[rows, cols],
                g_ref[rows, cols], lb_all[:, cols], gn, st_ref[hh])
            st_ref[hh] = st_new
            o_ref[rows, cols] = y.astype(o_ref.dtype)
        return carry

    lax.fori_loop(0, s // c, body, 0)


def _hgrn(z, lower_bounds, g_norm, batch, seq):
    t = z.shape[0]
    w = HGRN_HB * HEAD_DIM
    nh = HGRN_WIDTH // w
    est = 4 * 2 * seq * w * 4 + 2 * seq * w * 2 + HGRN_HB * HEAD_DIM * HEAD_DIM * 4

    def zspec(part):
        return pl.BlockSpec((seq, w), lambda b, h, part=part: (b, part * nh + h))

    return pl.pallas_call(
        _hgrn_kernel,
        grid=(batch, nh),
        in_specs=[zspec(0), zspec(1), zspec(2), zspec(3),
                  pl.BlockSpec((lower_bounds.shape[0], w), lambda b, h: (0, h)),
                  pl.BlockSpec((1, HEAD_DIM), lambda b, h: (0, 0))],
        out_specs=pl.BlockSpec((seq, w), lambda b, h: (b, h)),
        out_shape=jax.ShapeDtypeStruct((t, HGRN_WIDTH), BF16),
        scratch_shapes=[pltpu.VMEM((HGRN_HB, HEAD_DIM, HEAD_DIM), F32)],
        compiler_params=pltpu.CompilerParams(
            dimension_semantics=("arbitrary", "arbitrary"),
            vmem_limit_bytes=_vmem_limit(est)),
    )(z, z, z, z, lower_bounds, g_norm)


def _lru_kernel(x_ref, gate_ref, cw_ref, cb_ref, wax_ref, ba_ref, bx_ref, lam_ref,
                o_ref, tail_ref, hc_ref, a_ref, u_ref):
    ts, w = x_ref.shape
    nd = LRU_BLOCK_DIM

    @pl.when(pl.program_id(1) == 0)
    def _():
        tail_ref[...] = jnp.zeros_like(tail_ref)
        hc_ref[...] = jnp.zeros_like(hc_ref)

    x = x_ref[...]
    xx = jnp.concatenate([tail_ref[...], x], axis=0)
    cw = cw_ref[...]
    xc = cb_ref[...] + cw[3:4, :] * x
    for j in range(CONV_WIDTH - 1):
        off = SUBLANES - (CONV_WIDTH - 1) + j
        xc = xc + cw[j:j + 1, :] * xx[off:off + ts, :]
    tail_ref[...] = x[ts - SUBLANES:ts, :]

    nl = -lam_ref[...]
    sp = jnp.maximum(nl, 0.0) + jnp.log1p(jnp.exp(-jnp.abs(nl)))

    xcb = xc.astype(BF16)
    for n in range(LRU_BLOCKS):
        cols = slice(n * nd, (n + 1) * nd)
        rx = jnp.dot(xcb[:, cols], wax_ref[n], preferred_element_type=F32)
        r = jax.nn.sigmoid(rx[:, :nd] + ba_ref[:, cols])
        ig = jax.nn.sigmoid(rx[:, nd:] + bx_ref[:, cols])
        log_a = (-LRU_C) * r * sp[:, cols]
        a = jnp.exp(log_a)
        a_ref[:, cols] = a
        one_m_a2 = -jnp.tanh(log_a) * (a * a + 1.0)
        u_ref[:, cols] = jnp.sqrt(one_m_a2) * (ig * xc[:, cols])

    row = lax.broadcasted_iota(jnp.int32, (SUBLANES, w), 0)

    def slab(i, carry):
        rows = pl.ds(pl.multiple_of(i * SUBLANES, SUBLANES), SUBLANES)
        a = a_ref[rows, :]
        bv = u_ref[rows, :]
        d = 1
        while d < SUBLANES:
            m = row >= d
            bv = jnp.where(m, bv + a * pltpu.roll(bv, d, axis=0), bv)
            a = jnp.where(m, a * pltpu.roll(a, d, axis=0), a)
            d *= 2
        h = bv + a * carry
        u_ref[rows, :] = h
        return jnp.broadcast_to(h[SUBLANES - 1:SUBLANES, :], h.shape)

    carry = lax.fori_loop(0, ts // SUBLANES, slab, hc_ref[...], unroll=2)
    hc_ref[...] = carry
    o_ref[...] = (u_ref[...] * jax.nn.gelu(gate_ref[...])).astype(o_ref.dtype)


def _lru(z, conv_w, conv_b, wax, b_a, b_x, lam, batch, seq):
    t = z.shape[0]
    w = LRU_WIDTH
    ts = LRU_TS
    nt = seq // ts
    xblk = (4 * HGRN_WIDTH) // w
    est = 2 * 2 * ts * w * 4 + 2 * ts * w * 2 + 2 * ts * w * 4 + 2 * wax.size * 2

    def vec(rows):
        return pl.BlockSpec((rows, w), lambda b, i: (0, 0))

    return pl.pallas_call(
        _lru_kernel,
        grid=(batch, nt),
        in_specs=[
            pl.BlockSpec((ts, w), lambda b, i: (b * nt + i, xblk)),
            pl.BlockSpec((ts, w), lambda b, i: (b * nt + i, xblk + 1)),
            vec(CONV_WIDTH), vec(1),
            pl.BlockSpec(wax.shape, lambda b, i: (0, 0, 0)),
            vec(1), vec(1), vec(1),
        ],
        out_specs=pl.BlockSpec((ts, w), lambda b, i: (b * nt + i, 0)),
        out_shape=jax.ShapeDtypeStruct((t, w), BF16),
        scratch_shapes=[pltpu.VMEM((SUBLANES, w), F32), pltpu.VMEM((SUBLANES, w), F32),
                        pltpu.VMEM((ts, w), F32), pltpu.VMEM((ts, w), F32)],
        compiler_params=pltpu.CompilerParams(
            dimension_semantics=("arbitrary", "arbitrary"),
            vmem_limit_bytes=_vmem_limit(est)),
    )(z, z, conv_w, conv_b, wax, b_a, b_x, lam)


def _outproj_kernel(x_ref, oh_ref, ol_ref, wh_ref, wl_ref, o_ref):
    o_ref[...] = (x_ref[...]
                  + jnp.dot(oh_ref[...], wh_ref[...], preferred_element_type=F32)
                  + jnp.dot(ol_ref[...], wl_ref[...], preferred_element_type=F32))


def _out_proj(x, oh, ol, w_out):
    t, d = x.shape
    tm = ROW_TM
    kh, kl = oh.shape[1], ol.shape[1]
    est = 2 * 2 * tm * d * 4 + 2 * tm * (kh + kl) * 2 + 2 * (kh + kl) * d * 2
    return pl.pallas_call(
        _outproj_kernel,
        grid=(t // tm,),
        in_specs=[
            pl.BlockSpec((tm, d), lambda i: (i, 0)),
            pl.BlockSpec((tm, kh), lambda i: (i, 0)),
            pl.BlockSpec((tm, kl), lambda i: (i, 0)),
            pl.BlockSpec((kh, d), lambda i: (0, 0)),
            pl.BlockSpec((kl, d), lambda i: (1, 0)),
        ],
        out_specs=pl.BlockSpec((tm, d), lambda i: (i, 0)),
        out_shape=jax.ShapeDtypeStruct((t, d), F32),
        compiler_params=pltpu.CompilerParams(
            dimension_semantics=("arbitrary",),
            vmem_limit_bytes=_vmem_limit(est)),
    )(x, oh, ol, w_out, w_out)


def _ple_kernel(x_ref, p_ref, gp_ref, wg_ref, bg_ref, wp_ref, gf_ref, o_ref):
    x = x_ref[...]
    h = _rmsnorm(x, gp_ref[...]).astype(BF16)
    gate = jax.nn.sigmoid(
        jnp.dot(h, wg_ref[...], preferred_element_type=F32) + bg_ref[...])
    emb = jnp.dot(p_ref[...].astype(BF16), wp_ref[...], preferred_element_type=F32)
    o_ref[...] = _rmsnorm(x + gate * emb, gf_ref[...])


def _ple(x, p, g_ple, w_gate, b_gate, w_proj, g_final):
    t, d = x.shape
    pd = p.shape[1]
    tm = ROW_TM
    est = 2 * 2 * tm * d * 4 + 2 * tm * pd * 4 + 2 * d * d * 2 + 2 * pd * d * 2 + 2 * tm * d * 4
    row = pl.BlockSpec((1, d), lambda i: (0, 0))
    return pl.pallas_call(
        _ple_kernel,
        grid=(t // tm,),
        in_specs=[
            pl.BlockSpec((tm, d), lambda i: (i, 0)),
            pl.BlockSpec((tm, pd), lambda i: (i, 0)),
            row,
            pl.BlockSpec((d, d), lambda i: (0, 0)),
            row,
            pl.BlockSpec((pd, d), lambda i: (0, 0)),
            row,
        ],
        out_specs=pl.BlockSpec((tm, d), lambda i: (i, 0)),
        out_shape=jax.ShapeDtypeStruct((t, d), F32),
        compiler_params=pltpu.CompilerParams(
            dimension_semantics=("arbitrary",),
            vmem_limit_bytes=_vmem_limit(est)),
    )(x, p, g_ple, w_gate, b_gate, w_proj, g_final)


def kernel(x, p, ffn1_norm, ffn1_w_gate, ffn1_w_up, ffn1_w_down, mix_norm, w_in,
           hgrn_lower_bounds, hgrn_g_norm, conv_w, conv_b, lru_w_a, lru_b_a, lru_w_x,
           lru_b_x, lru_lambda, w_out, ffn2_norm, ffn2_w_gate, ffn2_w_up, ffn2_w_down,
           ple_norm, ple_w_gate, ple_b_gate, ple_w_proj, final_norm):
    batch, seq, d = x.shape
    t = batch * seq
    depth = ffn1_norm.shape[0]
    assert depth == 1, "the shared lower-bound cumsum is specialised to one layer"
    l = 0
    bf = lambda a: a.astype(BF16)
    xt = x.reshape(t, d)

    xt = _ffn(xt, ffn1_norm[l][None], bf(ffn1_w_gate[l]), bf(ffn1_w_up[l]),
              bf(ffn1_w_down[l]))

    z = _in_proj(xt, mix_norm[l][None], bf(w_in[l]))
    oh = _hgrn(z, hgrn_lower_bounds, hgrn_g_norm[l][None], batch, seq)
    wax = bf(jnp.concatenate([lru_w_a[l], lru_w_x[l]], axis=-1))
    ol = _lru(z, conv_w[l], conv_b[l][None], wax, lru_b_a[l][None], lru_b_x[l][None],
              lru_lambda[l][None], batch, seq)
    xt = _out_proj(xt, oh, ol, bf(w_out[l]))

    xt = _ffn(xt, ffn2_norm[l][None], bf(ffn2_w_gate[l]), bf(ffn2_w_up[l]),
              bf(ffn2_w_down[l]))

    out = _ple(xt, p[l].reshape(t, -1), ple_norm[l][None], bf(ple_w_gate[l]),
               ple_b_gate[l][None], bf(ple_w_proj[l]), final_norm[None])
    return out.reshape(batch, seq, d)
```

```python
import functools

import jax
import jax.numpy as jnp
from jax import lax
from jax.experimental import pallas as pl
from jax.experimental.pallas import tpu as pltpu

F32 = jnp.float32
BF16 = jnp.bfloat16

D_MODEL = 2048
D_FF = 5632
PLE_DIM = 256
HGRN_WIDTH = D_MODEL // 2
HEAD_DIM = 128
HGRN_HEADS = HGRN_WIDTH // HEAD_DIM
LRU_WIDTH = D_MODEL - HGRN_WIDTH
LRU_BLOCKS = 8
LRU_BLOCK_DIM = LRU_WIDTH // LRU_BLOCKS
CONV_WIDTH = 4
LRU_C = 8.0
EPS = 1e-6

V7X_VMEM_BYTES = 64 * 1024 * 1024
SUBLANES = 8
LANES = 128

FFN_TM = 512
FFN_TF = 512
PROJ_TM = 1024
PROJ_TN = 512
ROW_TM = 512
HGRN_CHUNK = 64
HGRN_SUB = 16
HGRN_TS = 1024
LRU_TS = 256


def _vmem_limit(nbytes):
    return int(min(V7X_VMEM_BYTES - (4 << 20), nbytes + nbytes // 4 + (2 << 20)))


def _rmsnorm(xf, g):
    ms = jnp.mean(xf * xf, axis=-1, keepdims=True)
    return xf * lax.rsqrt(ms + EPS) * g


def _ffn_kernel(x_ref, g_ref, wg_ref, wu_ref, wd_ref, o_ref, h_ref):
    j = pl.program_id(1)

    @pl.when(j == 0)
    def _():
        x = x_ref[...]
        h_ref[...] = _rmsnorm(x, g_ref[...]).astype(BF16)
        o_ref[...] = x

    h = h_ref[...]
    gate = jnp.dot(h, wg_ref[...], preferred_element_type=F32)
    up = jnp.dot(h, wu_ref[...], preferred_element_type=F32)
    act = (0.5 * (gate * jax.nn.sigmoid(gate) * up)).astype(BF16)
    o_ref[...] += jnp.dot(act, wd_ref[...], preferred_element_type=F32)


def _ffn(x, g, wg, wu, wd):
    t, d = x.shape
    dff = wg.shape[1]
    tm, tf = FFN_TM, FFN_TF
    est = (2 * tm * d * 4) * 2 + tm * d * 2 + 2 * 3 * d * tf * 2 + 3 * tm * tf * 4
    return pl.pallas_call(
        _ffn_kernel,
        grid=(t // tm, dff // tf),
        in_specs=[
            pl.BlockSpec((tm, d), lambda i, j: (i, 0)),
            pl.BlockSpec((1, d), lambda i, j: (0, 0)),
            pl.BlockSpec((d, tf), lambda i, j: (0, j)),
            pl.BlockSpec((d, tf), lambda i, j: (0, j)),
            pl.BlockSpec((tf, d), lambda i, j: (j, 0)),
        ],
        out_specs=pl.BlockSpec((tm, d), lambda i, j: (i, 0)),
        out_shape=jax.ShapeDtypeStruct((t, d), F32),
        scratch_shapes=[pltpu.VMEM((tm, d), BF16)],
        compiler_params=pltpu.CompilerParams(
            dimension_semantics=("arbitrary", "arbitrary"),
            vmem_limit_bytes=_vmem_limit(est)),
    )(x, g, wg, wu, wd)


def _proj_kernel(x_ref, g_ref, w_ref, o_ref, h_ref):
    @pl.when(pl.program_id(1) == 0)
    def _():
        h_ref[...] = _rmsnorm(x_ref[...], g_ref[...]).astype(BF16)

    o_ref[...] = jnp.dot(h_ref[...], w_ref[...], preferred_element_type=F32)


def _in_proj(x, g, w):
    t, d = x.shape
    n = w.shape[1]
    tm, tn = PROJ_TM, PROJ_TN
    est = 2 * tm * d * 4 + tm * d * 2 + 2 * d * tn * 2 + 2 * tm * tn * 4
    return pl.pallas_call(
        _proj_kernel,
        grid=(t // tm, n // tn),
        in_specs=[
            pl.BlockSpec((tm, d), lambda i, j: (i, 0)),
            pl.BlockSpec((1, d), lambda i, j: (0, 0)),
            pl.BlockSpec((d, tn), lambda i, j: (0, j)),
        ],
        out_specs=pl.BlockSpec((tm, tn), lambda i, j: (i, j)),
        out_shape=jax.ShapeDtypeStruct((t, n), F32),
        scratch_shapes=[pltpu.VMEM((tm, d), BF16)],
        compiler_params=pltpu.CompilerParams(
            dimension_semantics=("arbitrary", "arbitrary"),
            vmem_limit_bytes=_vmem_limit(est)),
    )(x, g, w)


def _bcast_row(x, r):
    return jnp.broadcast_to(x[r:r + 1, :], x.shape)


def _hgrn_chunk(q, fp, v, gt, lb, gn, st):
    c, sub = HGRN_CHUNK, HGRN_SUB
    nb = c // sub
    assert nb == 4, "the two off-diagonal levels below are written for four sub-blocks"
    f = lb + (1.0 - lb) * jax.nn.sigmoid(fp)
    lf = jnp.log(f)
    kk = 1.0 - f

    row = lax.broadcasted_iota(jnp.int32, (c, HEAD_DIM), 0)
    rsub = row & (sub - 1)
    cs = lf
    d = 1
    while d < sub:
        cs = cs + jnp.where(rsub >= d, pltpu.roll(cs, d, axis=0), 0.0)
        d *= 2

    cb = [cs[i * sub:(i + 1) * sub] for i in range(nb)]
    tot = [_bcast_row(cb[i], sub - 1) for i in range(nb)]
    mid = [_bcast_row(cb[i], sub // 2 - 1) for i in range(nb)]

    pre = [None] * nb
    acc = None
    for i in range(nb):
        pre[i] = acc
        acc = tot[i] if acc is None else acc + tot[i]
    b_last = acc
    b = jnp.concatenate(
        [cb[i] if pre[i] is None else cb[i] + pre[i] for i in range(nb)], axis=0)
    b_last_c = jnp.concatenate([b_last] * nb, axis=0)

    zeros = jnp.zeros((sub, HEAD_DIM), F32)
    x32 = jnp.concatenate([tot[0] + tot[1] - cb[0], tot[1] - cb[1],
                           cb[2], cb[3] + tot[2]], axis=0)
    x16 = jnp.concatenate([tot[0] - cb[0], cb[1], tot[2] - cb[2], cb[3]], axis=0)
    xd = jnp.concatenate([cb[i] - mid[i] for i in range(nb)], axis=0)

    e32 = jnp.exp(x32)
    e16 = jnp.exp(x16)
    ed = jnp.exp(xd)
    edi = jnp.exp(-xd)

    upper = row >= (c // 2)
    odd = (row & sub) != 0
    qe32 = jnp.where(upper, q * e32, 0.0).astype(BF16)
    ke32 = jnp.where(upper, 0.0, kk * e32).astype(BF16)
    qe16 = jnp.where(odd, q * e16, 0.0).astype(BF16)
    ke16 = jnp.where(odd, 0.0, kk * e16).astype(BF16)
    qed = (q * ed).astype(BF16)
    ked = (kk * edi).astype(BF16)

    nt = (((1,), (1,)), ((), ()))
    s32 = lax.dot_general(qe32, ke32, nt, preferred_element_type=F32)
    s16 = lax.dot_general(qe16, ke16, nt, preferred_element_type=F32)
    sd = lax.dot_general(qed, ked, nt, preferred_element_type=F32)

    ti = lax.broadcasted_iota(jnp.int32, (c, c), 0)
    si = lax.broadcasted_iota(jnp.int32, (c, c), 1)
    same32 = (ti // (2 * sub)) == (si // (2 * sub))
    diag = ((ti // sub) == (si // sub)) & (si <= ti)
    scores = s32 + jnp.where(same32, s16, 0.0) + jnp.where(diag, sd, 0.0)

    vb = v.astype(BF16)
    qhat = (q * jnp.exp(b)).astype(BF16)
    khat = (kk * jnp.exp(b_last_c - b)).astype(BF16)
    o = (lax.dot_general(qhat, st.astype(BF16), nt, preferred_element_type=F32)
         + jnp.dot(scores.astype(BF16), vb, preferred_element_type=F32))

    tn = (((0,), (0,)), ((), ()))
    st_new = (jnp.exp(b_last[0:1, :]) * st
              + lax.dot_general(vb, khat, tn, preferred_element_type=F32))

    y = _rmsnorm(o, gn) * (gt * jax.nn.sigmoid(gt))
    return y, st_new


def _hgrn_kernel(q_ref, f_ref, i_ref, g_ref, lbr_ref, gn_ref, o_ref, st_ref):
    c = HGRN_CHUNK
    ts = q_ref.shape[0]

    @pl.when(pl.program_id(1) == 0)
    def _():
        st_ref[...] = jnp.zeros_like(st_ref)

    lbr = lbr_ref[...]
    mx = jnp.max(lbr, axis=0, keepdims=True)
    ex = jnp.exp(lbr - mx)
    lb_all = ex[0:1, :] / jnp.sum(ex, axis=0, keepdims=True)
    gn = gn_ref[...]

    def body(ci, carry):
        rows = pl.ds(pl.multiple_of(ci * c, c), c)
        for hh in range(HGRN_HEADS):
            cols = slice(hh * HEAD_DIM, (hh + 1) * HEAD_DIM)
            y, st_new = _hgrn_chunk(
                q_ref[rows, cols], f_ref[rows, cols], i_ref[rows, cols],
                g_ref[rows, cols], lb_all[:, cols], gn, st_ref[hh])
            st_ref[hh] = st_new
            o_ref[rows, cols] = y.astype(o_ref.dtype)
        return carry

    lax.fori_loop(0, ts // c, body, 0)


def _hgrn(z, lower_bounds, g_norm, batch, seq):
    t = z.shape[0]
    w = HGRN_WIDTH
    ts = HGRN_TS
    nt = seq // ts
    est = 4 * 2 * ts * w * 4 + 2 * ts * w * 2 + HGRN_HEADS * HEAD_DIM * HEAD_DIM * 4

    def zspec(part):
        return pl.BlockSpec((ts, w), lambda b, i, part=part: (b * nt + i, part))

    return pl.pallas_call(
        _hgrn_kernel,
        grid=(batch, nt),
        in_specs=[zspec(0), zspec(1), zspec(2), zspec(3),
                  pl.BlockSpec((lower_bounds.shape[0], w), lambda b, i: (0, 0)),
                  pl.BlockSpec((1, HEAD_DIM), lambda b, i: (0, 0))],
        out_specs=pl.BlockSpec((ts, w), lambda b, i: (b * nt + i, 0)),
        out_shape=jax.ShapeDtypeStruct((t, w), BF16),
        scratch_shapes=[pltpu.VMEM((HGRN_HEADS, HEAD_DIM, HEAD_DIM), F32)],
        compiler_params=pltpu.CompilerParams(
            dimension_semantics=("arbitrary", "arbitrary"),
            vmem_limit_bytes=_vmem_limit(est)),
    )(z, z, z, z, lower_bounds, g_norm)


def _lru_kernel(x_ref, gate_ref, cw_ref, cb_ref, wax_ref, ba_ref, bx_ref, lam_ref,
                o_ref, tail_ref, hc_ref, a_ref, u_ref):
    ts, w = x_ref.shape
    nd = LRU_BLOCK_DIM

    @pl.when(pl.program_id(1) == 0)
    def _():
        tail_ref[...] = jnp.zeros_like(tail_ref)
        hc_ref[...] = jnp.zeros_like(hc_ref)

    x = x_ref[...]
    xx = jnp.concatenate([tail_ref[...], x], axis=0)
    cw = cw_ref[...]
    xc = cb_ref[...] + cw[3:4, :] * x
    for j in range(CONV_WIDTH - 1):
        off = SUBLANES - (CONV_WIDTH - 1) + j
        xc = xc + cw[j:j + 1, :] * xx[off:off + ts, :]
    tail_ref[...] = x[ts - SUBLANES:ts, :]

    nl = -lam_ref[...]
    sp = jnp.maximum(nl, 0.0) + jnp.log1p(jnp.exp(-jnp.abs(nl)))

    xcb = xc.astype(BF16)
    for n in range(LRU_BLOCKS):
        cols = slice(n * nd, (n + 1) * nd)
        rx = jnp.dot(xcb[:, cols], wax_ref[n], preferred_element_type=F32)
        r = jax.nn.sigmoid(rx[:, :nd] + ba_ref[:, cols])
        ig = jax.nn.sigmoid(rx[:, nd:] + bx_ref[:, cols])
        log_a = (-LRU_C) * r * sp[:, cols]
        a = jnp.exp(log_a)
        a_ref[:, cols] = a
        one_m_a2 = -jnp.tanh(log_a) * (a * a + 1.0)
        u_ref[:, cols] = jnp.sqrt(one_m_a2) * (ig * xc[:, cols])

    row = lax.broadcasted_iota(jnp.int32, (SUBLANES, w), 0)

    def slab(i, carry):
        rows = pl.ds(pl.multiple_of(i * SUBLANES, SUBLANES), SUBLANES)
        a = a_ref[rows, :]
        bv = u_ref[rows, :]
        d = 1
        while d < SUBLANES:
            m = row >= d
            bv = jnp.where(m, bv + a * pltpu.roll(bv, d, axis=0), bv)
            a = jnp.where(m, a * pltpu.roll(a, d, axis=0), a)
            d *= 2
        h = bv + a * carry
        u_ref[rows, :] = h
        return jnp.broadcast_to(h[SUBLANES - 1:SUBLANES, :], h.shape)

    carry = lax.fori_loop(0, ts // SUBLANES, slab, hc_ref[...], unroll=2)
    hc_ref[...] = carry
    o_ref[...] = (u_ref[...] * jax.nn.gelu(gate_ref[...])).astype(o_ref.dtype)


def _lru(z, conv_w, conv_b, wax, b_a, b_x, lam, batch, seq):
    t = z.shape[0]
    w = LRU_WIDTH
    ts = LRU_TS
    nt = seq // ts
    xblk = (4 * HGRN_WIDTH) // w
    est = 2 * 2 * ts * w * 4 + 2 * ts * w * 2 + 2 * ts * w * 4 + 2 * wax.size * 2

    def vec(rows):
        return pl.BlockSpec((rows, w), lambda b, i: (0, 0))

    return pl.pallas_call(
        _lru_kernel,
        grid=(batch, nt),
        in_specs=[
            pl.BlockSpec((ts, w), lambda b, i: (b * nt + i, xblk)),
            pl.BlockSpec((ts, w), lambda b, i: (b * nt + i, xblk + 1)),
            vec(CONV_WIDTH), vec(1),
            pl.BlockSpec(wax.shape, lambda b, i: (0, 0, 0)),
            vec(1), vec(1), vec(1),
        ],
        out_specs=pl.BlockSpec((ts, w), lambda b, i: (b * nt + i, 0)),
        out_shape=jax.ShapeDtypeStruct((t, w), BF16),
        scratch_shapes=[pltpu.VMEM((SUBLANES, w), F32), pltpu.VMEM((SUBLANES, w), F32),
                        pltpu.VMEM((ts, w), F32), pltpu.VMEM((ts, w), F32)],
        compiler_params=pltpu.CompilerParams(
            dimension_semantics=("arbitrary", "arbitrary"),
            vmem_limit_bytes=_vmem_limit(est)),
    )(z, z, conv_w, conv_b, wax, b_a, b_x, lam)


def _outproj_kernel(x_ref, oh_ref, ol_ref, wh_ref, wl_ref, o_ref):
    o_ref[...] = (x_ref[...]
                  + jnp.dot(oh_ref[...], wh_ref[...], preferred_element_type=F32)
                  + jnp.dot(ol_ref[...], wl_ref[...], preferred_element_type=F32))


def _out_proj(x, oh, ol, w_out):
    t, d = x.shape
    tm = ROW_TM
    kh, kl = oh.shape[1], ol.shape[1]
    est = 2 * 2 * tm * d * 4 + 2 * tm * (kh + kl) * 2 + 2 * (kh + kl) * d * 2
    return pl.pallas_call(
        _outproj_kernel,
        grid=(t // tm,),
        in_specs=[
            pl.BlockSpec((tm, d), lambda i: (i, 0)),
            pl.BlockSpec((tm, kh), lambda i: (i, 0)),
            pl.BlockSpec((tm, kl), lambda i: (i, 0)),
            pl.BlockSpec((kh, d), lambda i: (0, 0)),
            pl.BlockSpec((kl, d), lambda i: (1, 0)),
        ],
        out_specs=pl.BlockSpec((tm, d), lambda i: (i, 0)),
        out_shape=jax.ShapeDtypeStruct((t, d), F32),
        compiler_params=pltpu.CompilerParams(
            dimension_semantics=("arbitrary",),
            vmem_limit_bytes=_vmem_limit(est)),
    )(x, oh, ol, w_out, w_out)


def _ple_kernel(x_ref, p_ref, gp_ref, wg_ref, bg_ref, wp_ref, gf_ref, o_ref):
    x = x_ref[...]
    h = _rmsnorm(x, gp_ref[...]).astype(BF16)
    gate = jax.nn.sigmoid(
        jnp.dot(h, wg_ref[...], preferred_element_type=F32) + bg_ref[...])
    emb = jnp.dot(p_ref[...].astype(BF16), wp_ref[...], preferred_element_type=F32)
    o_ref[...] = _rmsnorm(x + gate * emb, gf_ref[...])


def _ple(x, p, g_ple, w_gate, b_gate, w_proj, g_final):
    t, d = x.shape
    pd = p.shape[1]
    tm = ROW_TM
    est = 2 * 2 * tm * d * 4 + 2 * tm * pd * 4 + 2 * d * d * 2 + 2 * pd * d * 2 + 2 * tm * d * 4
    row = pl.BlockSpec((1, d), lambda i: (0, 0))
    return pl.pallas_call(
        _ple_kernel,
        grid=(t // tm,),
        in_specs=[
            pl.BlockSpec((tm, d), lambda i: (i, 0)),
            pl.BlockSpec((tm, pd), lambda i: (i, 0)),
            row,
            pl.BlockSpec((d, d), lambda i: (0, 0)),
            row,
            pl.BlockSpec((pd, d), lambda i: (0, 0)),
            row,
        ],
        out_specs=pl.BlockSpec((tm, d), lambda i: (i, 0)),
        out_shape=jax.ShapeDtypeStruct((t, d), F32),
        compiler_params=pltpu.CompilerParams(
            dimension_semantics=("arbitrary",),
            vmem_limit_bytes=_vmem_limit(est)),
    )(x, p, g_ple, w_gate, b_gate, w_proj, g_final)


def kernel(x, p, ffn1_norm, ffn1_w_gate, ffn1_w_up, ffn1_w_down, mix_norm, w_in,
           hgrn_lower_bounds, hgrn_g_norm, conv_w, conv_b, lru_w_a, lru_b_a, lru_w_x,
           lru_b_x, lru_lambda, w_out, ffn2_norm, ffn2_w_gate, ffn2_w_up, ffn2_w_down,
           ple_norm, ple_w_gate, ple_b_gate, ple_w_proj, final_norm):
    batch, seq, d = x.shape
    t = batch * seq
    depth = ffn1_norm.shape[0]
    assert depth == 1, "the shared lower-bound cumsum is specialised to one layer"
    l = 0
    bf = lambda a: a.astype(BF16)
    xt = x.reshape(t, d)

    xt = _ffn(xt, ffn1_norm[l][None], bf(ffn1_w_gate[l]), bf(ffn1_w_up[l]),
              bf(ffn1_w_down[l]))

    z = _in_proj(xt, mix_norm[l][None], bf(w_in[l]))
    oh = _hgrn(z, hgrn_lower_bounds, hgrn_g_norm[l][None], batch, seq)
    wax = bf(jnp.concatenate([lru_w_a[l], lru_w_x[l]], axis=-1))
    ol = _lru(z, conv_w[l], conv_b[l][None], wax, lru_b_a[l][None], lru_b_x[l][None],
              lru_lambda[l][None], batch, seq)
    xt = _out_proj(xt, oh, ol, bf(w_out[l]))

    xt = _ffn(xt, ffn2_norm[l][None], bf(ffn2_w_gate[l]), bf(ffn2_w_up[l]),
              bf(ffn2_w_down[l]))

    out = _ple(xt, p[l].reshape(t, -1), ple_norm[l][None], bf(ple_w_gate[l]),
               ple_b_gate[l][None], bf(ple_w_proj[l]), final_norm[None])
    return out.reshape(batch, seq, d)
```

```python
import functools

import jax
import jax.numpy as jnp
from jax import lax
from jax.experimental import pallas as pl
from jax.experimental.pallas import tpu as pltpu

F32 = jnp.float32
BF16 = jnp.bfloat16

D_MODEL = 2048
D_FF = 5632
PLE_DIM = 256
HGRN_WIDTH = D_MODEL // 2
HEAD_DIM = 128
HGRN_HEADS = HGRN_WIDTH // HEAD_DIM
LRU_WIDTH = D_MODEL - HGRN_WIDTH
LRU_BLOCKS = 8
LRU_BLOCK_DIM = LRU_WIDTH // LRU_BLOCKS
CONV_WIDTH = 4
LRU_C = 8.0
EPS = 1e-6

V7X_VMEM_BYTES = 64 * 1024 * 1024
SUBLANES = 8
LANES = 128

FFN_TM = 1024
FFN_TF = 256
PROJ_TM = 2048
PROJ_TN = 256
ROW_TM = 512
HGRN_CHUNK = 64
HGRN_SUB = 16
HGRN_TS = 1024
LRU_TS = 256


def _vmem_limit(nbytes):
    return int(min(V7X_VMEM_BYTES - (4 << 20), nbytes + nbytes // 4 + (2 << 20)))


def _rmsnorm(xf, g):
    ms = jnp.mean(xf * xf, axis=-1, keepdims=True)
    return xf * lax.rsqrt(ms + EPS) * g


def _ffn_kernel(x_ref, g_ref, wg_ref, wu_ref, wd_ref, o_ref, h_ref):
    j = pl.program_id(1)

    @pl.when(j == 0)
    def _():
        x = x_ref[...]
        h_ref[...] = _rmsnorm(x, g_ref[...]).astype(BF16)
        o_ref[...] = x

    h = h_ref[...]
    gate = jnp.dot(h, wg_ref[...].astype(BF16), preferred_element_type=F32)
    up = jnp.dot(h, wu_ref[...].astype(BF16), preferred_element_type=F32)
    act = (0.5 * (gate * jax.nn.sigmoid(gate) * up)).astype(BF16)
    o_ref[...] += jnp.dot(act, wd_ref[...].astype(BF16), preferred_element_type=F32)


def _ffn(x, g, wg, wu, wd):
    t, d = x.shape
    dff = wg.shape[1]
    tm, tf = FFN_TM, FFN_TF
    wbytes = wg.dtype.itemsize
    est = (2 * tm * d * 4) * 2 + tm * d * 2 + 3 * d * tf * (2 * wbytes + 2) + 3 * tm * tf * 4
    return pl.pallas_call(
        _ffn_kernel,
        grid=(t // tm, dff // tf),
        in_specs=[
            pl.BlockSpec((tm, d), lambda i, j: (i, 0)),
            pl.BlockSpec((1, d), lambda i, j: (0, 0)),
            pl.BlockSpec((d, tf), lambda i, j: (0, j)),
            pl.BlockSpec((d, tf), lambda i, j: (0, j)),
            pl.BlockSpec((tf, d), lambda i, j: (j, 0)),
        ],
        out_specs=pl.BlockSpec((tm, d), lambda i, j: (i, 0)),
        out_shape=jax.ShapeDtypeStruct((t, d), F32),
        scratch_shapes=[pltpu.VMEM((tm, d), BF16)],
        compiler_params=pltpu.CompilerParams(
            dimension_semantics=("arbitrary", "arbitrary"),
            vmem_limit_bytes=_vmem_limit(est)),
    )(x, g, wg, wu, wd)


def _proj_kernel(x_ref, g_ref, w_ref, o_ref, h_ref):
    @pl.when(pl.program_id(1) == 0)
    def _():
        h_ref[...] = _rmsnorm(x_ref[...], g_ref[...]).astype(BF16)

    o_ref[...] = jnp.dot(h_ref[...], w_ref[...].astype(BF16), preferred_element_type=F32)


def _in_proj(x, g, w):
    t, d = x.shape
    n = w.shape[1]
    tm, tn = PROJ_TM, PROJ_TN
    est = 2 * tm * d * 4 + tm * d * 2 + d * tn * (2 * w.dtype.itemsize + 2) + 3 * tm * tn * 4
    return pl.pallas_call(
        _proj_kernel,
        grid=(t // tm, n // tn),
        in_specs=[
            pl.BlockSpec((tm, d), lambda i, j: (i, 0)),
            pl.BlockSpec((1, d), lambda i, j: (0, 0)),
            pl.BlockSpec((d, tn), lambda i, j: (0, j)),
        ],
        out_specs=pl.BlockSpec((tm, tn), lambda i, j: (i, j)),
        out_shape=jax.ShapeDtypeStruct((t, n), F32),
        scratch_shapes=[pltpu.VMEM((tm, d), BF16)],
        compiler_params=pltpu.CompilerParams(
            dimension_semantics=("arbitrary", "arbitrary"),
            vmem_limit_bytes=_vmem_limit(est)),
    )(x, g, w)


def _bcast_row(x, r):
    return jnp.broadcast_to(x[r:r + 1, :], x.shape)


def _hgrn_chunk(q, fp, v, gt, lb, gn, st):
    c, sub = HGRN_CHUNK, HGRN_SUB
    nb = c // sub
    assert nb == 4, "the two off-diagonal levels below are written for four sub-blocks"
    f = lb + (1.0 - lb) * jax.nn.sigmoid(fp)
    lf = jnp.log(f)
    kk = 1.0 - f

    row = lax.broadcasted_iota(jnp.int32, (c, HEAD_DIM), 0)
    rsub = row & (sub - 1)
    cs = lf
    d = 1
    while d < sub:
        cs = cs + jnp.where(rsub >= d, pltpu.roll(cs, d, axis=0), 0.0)
        d *= 2

    cb = [cs[i * sub:(i + 1) * sub] for i in range(nb)]
    tot = [_bcast_row(cb[i], sub - 1) for i in range(nb)]
    mid = [_bcast_row(cb[i], sub // 2 - 1) for i in range(nb)]

    pre = [None] * nb
    acc = None
    for i in range(nb):
        pre[i] = acc
        acc = tot[i] if acc is None else acc + tot[i]
    b_last = acc
    b = jnp.concatenate(
        [cb[i] if pre[i] is None else cb[i] + pre[i] for i in range(nb)], axis=0)
    b_last_c = jnp.concatenate([b_last] * nb, axis=0)

    zeros = jnp.zeros((sub, HEAD_DIM), F32)
    x32 = jnp.concatenate([tot[0] + tot[1] - cb[0], tot[1] - cb[1],
                           cb[2], cb[3] + tot[2]], axis=0)
    x16 = jnp.concatenate([tot[0] - cb[0], cb[1], tot[2] - cb[2], cb[3]], axis=0)
    xd = jnp.concatenate([cb[i] - mid[i] for i in range(nb)], axis=0)

    e32 = jnp.exp(x32)
    e16 = jnp.exp(x16)
    ed = jnp.exp(xd)
    edi = jnp.exp(-xd)

    upper = row >= (c // 2)
    odd = (row & sub) != 0
    qe32 = jnp.where(upper, q * e32, 0.0).astype(BF16)
    ke32 = jnp.where(upper, 0.0, kk * e32).astype(BF16)
    qe16 = jnp.where(odd, q * e16, 0.0).astype(BF16)
    ke16 = jnp.where(odd, 0.0, kk * e16).astype(BF16)
    qed = (q * ed).astype(BF16)
    ked = (kk * edi).astype(BF16)

    nt = (((1,), (1,)), ((), ()))
    s32 = lax.dot_general(qe32, ke32, nt, preferred_element_type=F32)
    s16 = lax.dot_general(qe16, ke16, nt, preferred_element_type=F32)
    sd = lax.dot_general(qed, ked, nt, preferred_element_type=F32)

    ti = lax.broadcasted_iota(jnp.int32, (c, c), 0)
    si = lax.broadcasted_iota(jnp.int32, (c, c), 1)
    same32 = (ti // (2 * sub)) == (si // (2 * sub))
    diag = ((ti // sub) == (si // sub)) & (si <= ti)
    scores = s32 + jnp.where(same32, s16, 0.0) + jnp.where(diag, sd, 0.0)

    vb = v.astype(BF16)
    qhat = (q * jnp.exp(b)).astype(BF16)
    khat = (kk * jnp.exp(b_last_c - b)).astype(BF16)
    o = (lax.dot_general(qhat, st.astype(BF16), nt, preferred_element_type=F32)
         + jnp.dot(scores.astype(BF16), vb, preferred_element_type=F32))

    tn = (((0,), (0,)), ((), ()))
    st_new = (jnp.exp(b_last[0:1, :]) * st
              + lax.dot_general(vb, khat, tn, preferred_element_type=F32))

    y = _rmsnorm(o, gn) * (gt * jax.nn.sigmoid(gt))
    return y, st_new


def _hgrn_kernel(q_ref, f_ref, i_ref, g_ref, lbr_ref, gn_ref, o_ref, st_ref):
    c = HGRN_CHUNK
    ts = q_ref.shape[0]

    @pl.when(pl.program_id(1) == 0)
    def _():
        st_ref[...] = jnp.zeros_like(st_ref)

    lbr = lbr_ref[...]
    mx = jnp.max(lbr, axis=0, keepdims=True)
    ex = jnp.exp(lbr - mx)
    lb_all = ex[0:1, :] / jnp.sum(ex, axis=0, keepdims=True)
    gn = gn_ref[...]

    def body(ci, carry):
        rows = pl.ds(pl.multiple_of(ci * c, c), c)
        for hh in range(HGRN_HEADS):
            cols = slice(hh * HEAD_DIM, (hh + 1) * HEAD_DIM)
            y, st_new = _hgrn_chunk(
                q_ref[rows, cols], f_ref[rows, cols], i_ref[rows, cols],
                g_ref[rows, cols], lb_all[:, cols], gn, st_ref[hh])
            st_ref[hh] = st_new
            o_ref[rows, cols] = y.astype(o_ref.dtype)
        return carry

    lax.fori_loop(0, ts // c, body, 0)


def _hgrn(z, lower_bounds, g_norm, batch, seq):
    t = z.shape[0]
    w = HGRN_WIDTH
    ts = HGRN_TS
    nt = seq // ts
    est = 4 * 2 * ts * w * 4 + 2 * ts * w * 2 + HGRN_HEADS * HEAD_DIM * HEAD_DIM * 4

    def zspec(part):
        return pl.BlockSpec((ts, w), lambda b, i, part=part: (b * nt + i, part))

    return pl.pallas_call(
        _hgrn_kernel,
        grid=(batch, nt),
        in_specs=[zspec(0), zspec(1), zspec(2), zspec(3),
                  pl.BlockSpec((lower_bounds.shape[0], w), lambda b, i: (0, 0)),
                  pl.BlockSpec((1, HEAD_DIM), lambda b, i: (0, 0))],
        out_specs=pl.BlockSpec((ts, w), lambda b, i: (b * nt + i, 0)),
        out_shape=jax.ShapeDtypeStruct((t, w), BF16),
        scratch_shapes=[pltpu.VMEM((HGRN_HEADS, HEAD_DIM, HEAD_DIM), F32)],
        compiler_params=pltpu.CompilerParams(
            dimension_semantics=("arbitrary", "arbitrary"),
            vmem_limit_bytes=_vmem_limit(est)),
    )(z, z, z, z, lower_bounds, g_norm)


def _lru_kernel(x_ref, gate_ref, cw_ref, cb_ref, wax_ref, ba_ref, bx_ref, lam_ref,
                o_ref, tail_ref, hc_ref, a_ref, u_ref):
    ts, w = x_ref.shape
    nd = LRU_BLOCK_DIM

    @pl.when(pl.program_id(1) == 0)
    def _():
        tail_ref[...] = jnp.zeros_like(tail_ref)
        hc_ref[...] = jnp.zeros_like(hc_ref)

    x = x_ref[...]
    xx = jnp.concatenate([tail_ref[...], x], axis=0)
    cw = cw_ref[...]
    xc = cb_ref[...] + cw[3:4, :] * x
    for j in range(CONV_WIDTH - 1):
        off = SUBLANES - (CONV_WIDTH - 1) + j
        xc = xc + cw[j:j + 1, :] * xx[off:off + ts, :]
    tail_ref[...] = x[ts - SUBLANES:ts, :]

    nl = -lam_ref[...]
    sp = jnp.maximum(nl, 0.0) + jnp.log1p(jnp.exp(-jnp.abs(nl)))

    xcb = xc.astype(BF16)
    for n in range(LRU_BLOCKS):
        cols = slice(n * nd, (n + 1) * nd)
        rx = jnp.dot(xcb[:, cols], wax_ref[n], preferred_element_type=F32)
        r = jax.nn.sigmoid(rx[:, :nd] + ba_ref[:, cols])
        ig = jax.nn.sigmoid(rx[:, nd:] + bx_ref[:, cols])
        log_a = (-LRU_C) * r * sp[:, cols]
        a = jnp.exp(log_a)
        a_ref[:, cols] = a
        one_m_a2 = -jnp.tanh(log_a) * (a * a + 1.0)
        u_ref[:, cols] = jnp.sqrt(one_m_a2) * (ig * xc[:, cols])

    row = lax.broadcasted_iota(jnp.int32, (SUBLANES, w), 0)

    def slab(i, carry):
        rows = pl.ds(pl.multiple_of(i * SUBLANES, SUBLANES), SUBLANES)
        a = a_ref[rows, :]
        bv = u_ref[rows, :]
        d = 1
        while d < SUBLANES:
            m = row >= d
            bv = jnp.where(m, bv + a * pltpu.roll(bv, d, axis=0), bv)
            a = jnp.where(m, a * pltpu.roll(a, d, axis=0), a)
            d *= 2
        h = bv + a * carry
        u_ref[rows, :] = h
        return jnp.broadcast_to(h[SUBLANES - 1:SUBLANES, :], h.shape)

    carry = lax.fori_loop(0, ts // SUBLANES, slab, hc_ref[...], unroll=2)
    hc_ref[...] = carry
    o_ref[...] = (u_ref[...] * jax.nn.gelu(gate_ref[...])).astype(o_ref.dtype)


def _lru(z, conv_w, conv_b, wax, b_a, b_x, lam, batch, seq):
    t = z.shape[0]
    w = LRU_WIDTH
    ts = LRU_TS
    nt = seq // ts
    xblk = (4 * HGRN_WIDTH) // w
    est = 2 * 2 * ts * w * 4 + 2 * ts * w * 2 + 2 * ts * w * 4 + 2 * wax.size * 2

    def vec(rows):
        return pl.BlockSpec((rows, w), lambda b, i: (0, 0))

    return pl.pallas_call(
        _lru_kernel,
        grid=(batch, nt),
        in_specs=[
            pl.BlockSpec((ts, w), lambda b, i: (b * nt + i, xblk)),
            pl.BlockSpec((ts, w), lambda b, i: (b * nt + i, xblk + 1)),
            vec(CONV_WIDTH), vec(1),
            pl.BlockSpec(wax.shape, lambda b, i: (0, 0, 0)),
            vec(1), vec(1), vec(1),
        ],
        out_specs=pl.BlockSpec((ts, w), lambda b, i: (b * nt + i, 0)),
        out_shape=jax.ShapeDtypeStruct((t, w), BF16),
        scratch_shapes=[pltpu.VMEM((SUBLANES, w), F32), pltpu.VMEM((SUBLANES, w), F32),
                        pltpu.VMEM((ts, w), F32), pltpu.VMEM((ts, w), F32)],
        compiler_params=pltpu.CompilerParams(
            dimension_semantics=("arbitrary", "arbitrary"),
            vmem_limit_bytes=_vmem_limit(est)),
    )(z, z, conv_w, conv_b, wax, b_a, b_x, lam)


def _outproj_kernel(x_ref, oh_ref, ol_ref, w_ref, o_ref, wb_ref):
    @pl.when(pl.program_id(0) == 0)
    def _():
        wb_ref[...] = w_ref[...].astype(BF16)

    kh = oh_ref.shape[1]
    o_ref[...] = (x_ref[...]
                  + jnp.dot(oh_ref[...], wb_ref[:kh, :], preferred_element_type=F32)
                  + jnp.dot(ol_ref[...], wb_ref[kh:, :], preferred_element_type=F32))


def _resident(shape):
    return pl.BlockSpec(shape, lambda i: (0,) * len(shape), pipeline_mode=pl.Buffered(1))


def _out_proj(x, oh, ol, w_out):
    t, d = x.shape
    tm = ROW_TM
    kh, kl = oh.shape[1], ol.shape[1]
    est = 2 * 2 * tm * d * 4 + 2 * tm * (kh + kl) * 2 + (kh + kl) * d * (4 + 2)
    return pl.pallas_call(
        _outproj_kernel,
        grid=(t // tm,),
        in_specs=[
            pl.BlockSpec((tm, d), lambda i: (i, 0)),
            pl.BlockSpec((tm, kh), lambda i: (i, 0)),
            pl.BlockSpec((tm, kl), lambda i: (i, 0)),
            _resident(w_out.shape),
        ],
        out_specs=pl.BlockSpec((tm, d), lambda i: (i, 0)),
        out_shape=jax.ShapeDtypeStruct((t, d), F32),
        scratch_shapes=[pltpu.VMEM(w_out.shape, BF16)],
        compiler_params=pltpu.CompilerParams(
            dimension_semantics=("arbitrary",),
            vmem_limit_bytes=_vmem_limit(est)),
    )(x, oh, ol, w_out)


def _ple_kernel(x_ref, p_ref, gp_ref, wg_ref, bg_ref, wp_ref, gf_ref, o_ref,
                wgb_ref, wpb_ref):
    @pl.when(pl.program_id(0) == 0)
    def _():
        wgb_ref[...] = wg_ref[...].astype(BF16)
        wpb_ref[...] = wp_ref[...].astype(BF16)

    x = x_ref[...]
    h = _rmsnorm(x, gp_ref[...]).astype(BF16)
    gate = jax.nn.sigmoid(
        jnp.dot(h, wgb_ref[...], preferred_element_type=F32) + bg_ref[...])
    emb = jnp.dot(p_ref[...].astype(BF16), wpb_ref[...], preferred_element_type=F32)
    o_ref[...] = _rmsnorm(x + gate * emb, gf_ref[...])


def _ple(x, p, g_ple, w_gate, b_gate, w_proj, g_final):
    t, d = x.shape
    pd = p.shape[1]
    tm = ROW_TM
    est = 2 * 2 * tm * d * 4 + 2 * tm * pd * 4 + (d + pd) * d * (4 + 2) + 2 * tm * d * 4
    row = pl.BlockSpec((1, d), lambda i: (0, 0))
    return pl.pallas_call(
        _ple_kernel,
        grid=(t // tm,),
        in_specs=[
            pl.BlockSpec((tm, d), lambda i: (i, 0)),
            pl.BlockSpec((tm, pd), lambda i: (i, 0)),
            row,
            _resident(w_gate.shape),
            row,
            _resident(w_proj.shape),
            row,
        ],
        out_specs=pl.BlockSpec((tm, d), lambda i: (i, 0)),
        out_shape=jax.ShapeDtypeStruct((t, d), F32),
        scratch_shapes=[pltpu.VMEM(w_gate.shape, BF16), pltpu.VMEM(w_proj.shape, BF16)],
        compiler_params=pltpu.CompilerParams(
            dimension_semantics=("arbitrary",),
            vmem_limit_bytes=_vmem_limit(est)),
    )(x, p, g_ple, w_gate, b_gate, w_proj, g_final)


def kernel(x, p, ffn1_norm, ffn1_w_gate, ffn1_w_up, ffn1_w_down, mix_norm, w_in,
           hgrn_lower_bounds, hgrn_g_norm, conv_w, conv_b, lru_w_a, lru_b_a, lru_w_x,
           lru_b_x, lru_lambda, w_out, ffn2_norm, ffn2_w_gate, ffn2_w_up, ffn2_w_down,
           ple_norm, ple_w_gate, ple_b_gate, ple_w_proj, final_norm):
    batch, seq, d = x.shape
    t = batch * seq
    depth = ffn1_norm.shape[0]
    assert depth == 1, "the shared lower-bound cumsum is specialised to one layer"
    l = 0
    bf = lambda a: a.astype(BF16)
    xt = x.reshape(t, d)

    xt = _ffn(xt, ffn1_norm[l][None], ffn1_w_gate[l], ffn1_w_up[l], ffn1_w_down[l])

    z = _in_proj(xt, mix_norm[l][None], w_in[l])
    oh = _hgrn(z, hgrn_lower_bounds, hgrn_g_norm[l][None], batch, seq)
    wax = bf(jnp.concatenate([lru_w_a[l], lru_w_x[l]], axis=-1))
    ol = _lru(z, conv_w[l], conv_b[l][None], wax, lru_b_a[l][None], lru_b_x[l][None],
              lru_lambda[l][None], batch, seq)
    xt = _out_proj(xt, oh, ol, w_out[l])

    xt = _ffn(xt, ffn2_norm[l][None], ffn2_w_gate[l], ffn2_w_up[l], ffn2_w_down[l])

    out = _ple(xt, p[l].reshape(t, -1), ple_norm[l][None], ple_w_gate[l],
               ple_b_gate[l][None], ple_w_proj[l], final_norm[None])
    return out.reshape(batch, seq, d)
```

```python
import functools

import jax
import jax.numpy as jnp
from jax import lax
from jax.experimental import pallas as pl
from jax.experimental.pallas import tpu as pltpu

F32 = jnp.float32
BF16 = jnp.bfloat16

D_MODEL = 2048
D_FF = 5632
PLE_DIM = 256
HGRN_WIDTH = D_MODEL // 2
HEAD_DIM = 128
HGRN_HEADS = HGRN_WIDTH // HEAD_DIM
LRU_WIDTH = D_MODEL - HGRN_WIDTH
LRU_BLOCKS = 8
LRU_BLOCK_DIM = LRU_WIDTH // LRU_BLOCKS
CONV_WIDTH = 4
LRU_C = 8.0
EPS = 1e-6

V7X_VMEM_BYTES = 64 * 1024 * 1024
SUBLANES = 8
LANES = 128

FFN_TM = 1024
FFN_TF = 256
PROJ_TM = 2048
PROJ_TN = 256
ROW_TM = 512
HGRN_CHUNK = 64
HGRN_SUB = 16
HGRN_TS = 1024
LRU_TS = 256


def _vmem_limit(nbytes):
    return int(min(V7X_VMEM_BYTES - (4 << 20), nbytes + nbytes // 4 + (2 << 20)))


def _rmsnorm(xf, g):
    ms = jnp.mean(xf * xf, axis=-1, keepdims=True)
    return xf * lax.rsqrt(ms + EPS) * g


def _ffn_kernel(x_ref, g_ref, wg_ref, wu_ref, wd_ref, o_ref, h_ref):
    j = pl.program_id(1)

    @pl.when(j == 0)
    def _():
        x = x_ref[...]
        h_ref[...] = _rmsnorm(x, g_ref[...]).astype(BF16)
        o_ref[...] = x

    h = h_ref[...]
    gate = jnp.dot(h, wg_ref[...].astype(BF16), preferred_element_type=F32)
    up = jnp.dot(h, wu_ref[...].astype(BF16), preferred_element_type=F32)
    act = (0.5 * (gate * jax.nn.sigmoid(gate) * up)).astype(BF16)
    o_ref[...] += jnp.dot(act, wd_ref[...].astype(BF16), preferred_element_type=F32)


def _ffn(x, g, wg, wu, wd):
    t, d = x.shape
    dff = wg.shape[1]
    tm, tf = FFN_TM, FFN_TF
    wbytes = wg.dtype.itemsize
    est = (2 * tm * d * 4) * 2 + tm * d * 2 + 3 * d * tf * (2 * wbytes + 2) + 3 * tm * tf * 4
    return pl.pallas_call(
        _ffn_kernel,
        grid=(t // tm, dff // tf),
        in_specs=[
            pl.BlockSpec((tm, d), lambda i, j: (i, 0)),
            pl.BlockSpec((1, d), lambda i, j: (0, 0)),
            pl.BlockSpec((d, tf), lambda i, j: (0, j)),
            pl.BlockSpec((d, tf), lambda i, j: (0, j)),
            pl.BlockSpec((tf, d), lambda i, j: (j, 0)),
        ],
        out_specs=pl.BlockSpec((tm, d), lambda i, j: (i, 0)),
        out_shape=jax.ShapeDtypeStruct((t, d), F32),
        scratch_shapes=[pltpu.VMEM((tm, d), BF16)],
        compiler_params=pltpu.CompilerParams(
            dimension_semantics=("arbitrary", "arbitrary"),
            vmem_limit_bytes=_vmem_limit(est)),
    )(x, g, wg, wu, wd)


def _proj_kernel(x_ref, g_ref, w_ref, o_ref, h_ref):
    @pl.when(pl.program_id(1) == 0)
    def _():
        h_ref[...] = _rmsnorm(x_ref[...], g_ref[...]).astype(BF16)

    o_ref[...] = jnp.dot(h_ref[...], w_ref[...].astype(BF16), preferred_element_type=F32)


def _in_proj(x, g, w):
    t, d = x.shape
    n = w.shape[1]
    tm, tn = PROJ_TM, PROJ_TN
    est = 2 * tm * d * 4 + tm * d * 2 + d * tn * (2 * w.dtype.itemsize + 2) + 3 * tm * tn * 4
    return pl.pallas_call(
        _proj_kernel,
        grid=(t // tm, n // tn),
        in_specs=[
            pl.BlockSpec((tm, d), lambda i, j: (i, 0)),
            pl.BlockSpec((1, d), lambda i, j: (0, 0)),
            pl.BlockSpec((d, tn), lambda i, j: (0, j)),
        ],
        out_specs=pl.BlockSpec((tm, tn), lambda i, j: (i, j)),
        out_shape=jax.ShapeDtypeStruct((t, n), F32),
        scratch_shapes=[pltpu.VMEM((tm, d), BF16)],
        compiler_params=pltpu.CompilerParams(
            dimension_semantics=("arbitrary", "arbitrary"),
            vmem_limit_bytes=_vmem_limit(est)),
    )(x, g, w)


def _hgrn_cumsum_matrix():
    c, sub = HGRN_CHUNK, HGRN_SUB
    ti = lax.broadcasted_iota(jnp.int32, (c, c), 0)
    si = lax.broadcasted_iota(jnp.int32, (c, c), 1)
    return jnp.where(((ti // sub) == (si // sub)) & (si <= ti), 1.0, 0.0).astype(BF16)


def _hgrn_chunk(q, fp, v, gt, lb, gn, st_ref):
    c, sub = HGRN_CHUNK, HGRN_SUB
    nb = c // sub
    assert nb == 4, "the two off-diagonal levels below are written for four sub-blocks"
    w = q.shape[1]
    f = lb + (1.0 - lb) * jax.nn.sigmoid(fp)
    lf = jnp.log2(f)
    kk = 1.0 - f

    lmat = _hgrn_cumsum_matrix()
    hi = lf.astype(BF16)
    lo = (lf - hi.astype(F32)).astype(BF16)
    cs = (jnp.dot(lmat, hi, preferred_element_type=F32)
          + jnp.dot(lmat, lo, preferred_element_type=F32))

    blk = lambda x, i: x[i * sub:(i + 1) * sub]
    rows = lambda x: jnp.broadcast_to(x, (sub, w))
    cb = [blk(cs, i) for i in range(nb)]
    tot = [cb[i][sub - 1:sub, :] for i in range(nb)]
    tdec = [jnp.exp2(tot[i]) for i in range(nb)]
    e_in = jnp.exp2(cs)
    e_out = jnp.exp2(jnp.concatenate([tot[i] - cb[i] for i in range(nb)], axis=0))
    xd = jnp.concatenate(
        [cb[i] - cb[i][sub // 2 - 1:sub // 2, :] for i in range(nb)], axis=0)
    qe = q * e_in
    ke = kk * e_out
    qed = (q * jnp.exp2(xd)).astype(BF16)
    ked = (kk * jnp.exp2(-xd)).astype(BF16)

    pre = [None, tdec[0], tdec[0] * tdec[1], tdec[0] * tdec[1] * tdec[2]]
    suf = [tdec[1] * tdec[2] * tdec[3], tdec[2] * tdec[3], tdec[3], None]
    qhat = jnp.concatenate(
        [blk(qe, i) if pre[i] is None else blk(qe, i) * rows(pre[i]) for i in range(nb)],
        axis=0).astype(BF16)
    khat = jnp.concatenate(
        [blk(ke, i) if suf[i] is None else blk(ke, i) * rows(suf[i]) for i in range(nb)],
        axis=0).astype(BF16)
    st_decay = pre[3] * tdec[3]

    qeb = [blk(qe, i).astype(BF16) for i in range(nb)]
    keb = [blk(ke, i).astype(BF16) for i in range(nb)]
    q32_3 = (blk(qe, 3) * rows(tdec[2])).astype(BF16)
    k32_0 = (blk(ke, 0) * rows(tdec[1])).astype(BF16)
    zb = jnp.zeros((sub, w), BF16)
    col = lambda parts: jnp.concatenate(parts, axis=0)
    q_off = [col([zb, zb, qeb[2], q32_3]), col([zb, qeb[1], zb, zb]), col([zb, zb, zb, qeb[3]])]
    k_off = [col([k32_0, keb[1], zb, zb]), col([keb[0], zb, zb, zb]), col([zb, zb, keb[2], zb])]

    vb = v.astype(BF16)
    ti = lax.broadcasted_iota(jnp.int32, (c, c), 0)
    si = lax.broadcasted_iota(jnp.int32, (c, c), 1)
    diag = ((ti // sub) == (si // sub)) & (si <= ti)
    nt = (((1,), (1,)), ((), ()))
    tn = (((0,), (0,)), ((), ()))

    heads = [slice(hh * HEAD_DIM, (hh + 1) * HEAD_DIM) for hh in range(w // HEAD_DIM)]
    scores = []
    for cols in heads:
        a_off = jnp.concatenate([x[:, cols] for x in q_off], axis=1)
        b_off = jnp.concatenate([x[:, cols] for x in k_off], axis=1)
        s_off = lax.dot_general(a_off, b_off, nt, preferred_element_type=F32)
        s_dia = lax.dot_general(qed[:, cols], ked[:, cols], nt, preferred_element_type=F32)
        scores.append((s_off + jnp.where(diag, s_dia, 0.0)).astype(BF16))
    outs = []
    for hh, cols in enumerate(heads):
        o = (lax.dot_general(qhat[:, cols], st_ref[hh].astype(BF16), nt,
                             preferred_element_type=F32)
             + jnp.dot(scores[hh], vb[:, cols], preferred_element_type=F32))
        ms = jnp.mean(o * o, axis=-1, keepdims=True)
        outs.append(o * lax.rsqrt(ms + EPS))
    for hh, cols in enumerate(heads):
        st_ref[hh] = (st_decay[:, cols] * st_ref[hh]
                      + lax.dot_general(vb[:, cols], khat[:, cols], tn,
                                        preferred_element_type=F32))
    y = jnp.concatenate(outs, axis=1) * gn
    return y * (gt * jax.nn.sigmoid(gt))


def _hgrn_kernel(q_ref, f_ref, i_ref, g_ref, lbr_ref, gn_ref, o_ref, st_ref):
    c = HGRN_CHUNK
    ts, w = q_ref.shape

    @pl.when(pl.program_id(1) == 0)
    def _():
        st_ref[...] = jnp.zeros_like(st_ref)

    lbr = lbr_ref[...]
    mx = jnp.max(lbr, axis=0, keepdims=True)
    ex = jnp.exp(lbr - mx)
    lb = ex[0:1, :] / jnp.sum(ex, axis=0, keepdims=True)
    gn = jnp.concatenate([gn_ref[...]] * (w // HEAD_DIM), axis=1)

    def body(ci, carry):
        rows = pl.ds(pl.multiple_of(ci * c, c), c)
        y = _hgrn_chunk(q_ref[rows, :], f_ref[rows, :], i_ref[rows, :], g_ref[rows, :],
                        lb, gn, st_ref)
        o_ref[rows, :] = y.astype(o_ref.dtype)
        return carry

    lax.fori_loop(0, ts // c, body, 0)


def _hgrn(z, lower_bounds, g_norm, batch, seq):
    t = z.shape[0]
    w = HGRN_WIDTH
    ts = HGRN_TS
    nt = seq // ts
    est = 4 * 2 * ts * w * 4 + 2 * ts * w * 2 + HGRN_HEADS * HEAD_DIM * HEAD_DIM * 4

    def zspec(part):
        return pl.BlockSpec((ts, w), lambda b, i, part=part: (b * nt + i, part))

    return pl.pallas_call(
        _hgrn_kernel,
        grid=(batch, nt),
        in_specs=[zspec(0), zspec(1), zspec(2), zspec(3),
                  pl.BlockSpec((lower_bounds.shape[0], w), lambda b, i: (0, 0)),
                  pl.BlockSpec((1, HEAD_DIM), lambda b, i: (0, 0))],
        out_specs=pl.BlockSpec((ts, w), lambda b, i: (b * nt + i, 0)),
        out_shape=jax.ShapeDtypeStruct((t, w), BF16),
        scratch_shapes=[pltpu.VMEM((HGRN_HEADS, HEAD_DIM, HEAD_DIM), F32)],
        compiler_params=pltpu.CompilerParams(
            dimension_semantics=("arbitrary", "arbitrary"),
            vmem_limit_bytes=_vmem_limit(est)),
    )(z, z, z, z, lower_bounds, g_norm)


def _lru_kernel(x_ref, gate_ref, cw_ref, cb_ref, wax_ref, ba_ref, bx_ref, lam_ref,
                o_ref, tail_ref, hc_ref, a_ref, u_ref):
    ts, w = x_ref.shape
    nd = LRU_BLOCK_DIM

    @pl.when(pl.program_id(1) == 0)
    def _():
        tail_ref[...] = jnp.zeros_like(tail_ref)
        hc_ref[...] = jnp.zeros_like(hc_ref)

    x = x_ref[...]
    xx = jnp.concatenate([tail_ref[...], x], axis=0)
    cw = cw_ref[...]
    xc = cb_ref[...] + cw[3:4, :] * x
    for j in range(CONV_WIDTH - 1):
        off = SUBLANES - (CONV_WIDTH - 1) + j
        xc = xc + cw[j:j + 1, :] * xx[off:off + ts, :]
    tail_ref[...] = x[ts - SUBLANES:ts, :]

    nl = -lam_ref[...]
    sp = jnp.maximum(nl, 0.0) + jnp.log1p(jnp.exp(-jnp.abs(nl)))

    xcb = xc.astype(BF16)
    for n in range(LRU_BLOCKS):
        cols = slice(n * nd, (n + 1) * nd)
        rx = jnp.dot(xcb[:, cols], wax_ref[n], preferred_element_type=F32)
        r = jax.nn.sigmoid(rx[:, :nd] + ba_ref[:, cols])
        ig = jax.nn.sigmoid(rx[:, nd:] + bx_ref[:, cols])
        log_a = (-LRU_C) * r * sp[:, cols]
        a = jnp.exp(log_a)
        a_ref[:, cols] = a
        one_m_a2 = -jnp.tanh(log_a) * (a * a + 1.0)
        u_ref[:, cols] = jnp.sqrt(one_m_a2) * (ig * xc[:, cols])

    row = lax.broadcasted_iota(jnp.int32, (SUBLANES, w), 0)

    def slab(i, carry):
        rows = pl.ds(pl.multiple_of(i * SUBLANES, SUBLANES), SUBLANES)
        a = a_ref[rows, :]
        bv = u_ref[rows, :]
        d = 1
        while d < SUBLANES:
            m = row >= d
            bv = jnp.where(m, bv + a * pltpu.roll(bv, d, axis=0), bv)
            a = jnp.where(m, a * pltpu.roll(a, d, axis=0), a)
            d *= 2
        h = bv + a * carry
        u_ref[rows, :] = h
        return jnp.broadcast_to(h[SUBLANES - 1:SUBLANES, :], h.shape)

    carry = lax.fori_loop(0, ts // SUBLANES, slab, hc_ref[...], unroll=2)
    hc_ref[...] = carry
    o_ref[...] = (u_ref[...] * jax.nn.gelu(gate_ref[...])).astype(o_ref.dtype)


def _lru(z, conv_w, conv_b, wax, b_a, b_x, lam, batch, seq):
    t = z.shape[0]
    w = LRU_WIDTH
    ts = LRU_TS
    nt = seq // ts
    xblk = (4 * HGRN_WIDTH) // w
    est = 2 * 2 * ts * w * 4 + 2 * ts * w * 2 + 2 * ts * w * 4 + 2 * wax.size * 2

    def vec(rows):
        return pl.BlockSpec((rows, w), lambda b, i: (0, 0))

    return pl.pallas_call(
        _lru_kernel,
        grid=(batch, nt),
        in_specs=[
            pl.BlockSpec((ts, w), lambda b, i: (b * nt + i, xblk)),
            pl.BlockSpec((ts, w), lambda b, i: (b * nt + i, xblk + 1)),
            vec(CONV_WIDTH), vec(1),
            pl.BlockSpec(wax.shape, lambda b, i: (0, 0, 0)),
            vec(1), vec(1), vec(1),
        ],
        out_specs=pl.BlockSpec((ts, w), lambda b, i: (b * nt + i, 0)),
        out_shape=jax.ShapeDtypeStruct((t, w), BF16),
        scratch_shapes=[pltpu.VMEM((SUBLANES, w), F32), pltpu.VMEM((SUBLANES, w), F32),
                        pltpu.VMEM((ts, w), F32), pltpu.VMEM((ts, w), F32)],
        compiler_params=pltpu.CompilerParams(
            dimension_semantics=("arbitrary", "arbitrary"),
            vmem_limit_bytes=_vmem_limit(est)),
    )(z, z, conv_w, conv_b, wax, b_a, b_x, lam)


def _outproj_kernel(x_ref, oh_ref, ol_ref, w_ref, o_ref, wb_ref):
    @pl.when(pl.program_id(0) == 0)
    def _():
        wb_ref[...] = w_ref[...].astype(BF16)

    kh = oh_ref.shape[1]
    o_ref[...] = (x_ref[...]
                  + jnp.dot(oh_ref[...], wb_ref[:kh, :], preferred_element_type=F32)
                  + jnp.dot(ol_ref[...], wb_ref[kh:, :], preferred_element_type=F32))


def _resident(shape):
    return pl.BlockSpec(shape, lambda i: (0,) * len(shape), pipeline_mode=pl.Buffered(1))


def _out_proj(x, oh, ol, w_out):
    t, d = x.shape
    tm = ROW_TM
    kh, kl = oh.shape[1], ol.shape[1]
    est = 2 * 2 * tm * d * 4 + 2 * tm * (kh + kl) * 2 + (kh + kl) * d * (4 + 2)
    return pl.pallas_call(
        _outproj_kernel,
        grid=(t // tm,),
        in_specs=[
            pl.BlockSpec((tm, d), lambda i: (i, 0)),
            pl.BlockSpec((tm, kh), lambda i: (i, 0)),
            pl.BlockSpec((tm, kl), lambda i: (i, 0)),
            _resident(w_out.shape),
        ],
        out_specs=pl.BlockSpec((tm, d), lambda i: (i, 0)),
        out_shape=jax.ShapeDtypeStruct((t, d), F32),
        scratch_shapes=[pltpu.VMEM(w_out.shape, BF16)],
        compiler_params=pltpu.CompilerParams(
            dimension_semantics=("arbitrary",),
            vmem_limit_bytes=_vmem_limit(est)),
    )(x, oh, ol, w_out)


def _ple_kernel(x_ref, p_ref, gp_ref, wg_ref, bg_ref, wp_ref, gf_ref, o_ref,
                wgb_ref, wpb_ref):
    @pl.when(pl.program_id(0) == 0)
    def _():
        wgb_ref[...] = wg_ref[...].astype(BF16)
        wpb_ref[...] = wp_ref[...].astype(BF16)

    x = x_ref[...]
    h = _rmsnorm(x, gp_ref[...]).astype(BF16)
    gate = jax.nn.sigmoid(
        jnp.dot(h, wgb_ref[...], preferred_element_type=F32) + bg_ref[...])
    emb = jnp.dot(p_ref[...].astype(BF16), wpb_ref[...], preferred_element_type=F32)
    o_ref[...] = _rmsnorm(x + gate * emb, gf_ref[...])


def _ple(x, p, g_ple, w_gate, b_gate, w_proj, g_final):
    t, d = x.shape
    pd = p.shape[1]
    tm = ROW_TM
    est = 2 * 2 * tm * d * 4 + 2 * tm * pd * 4 + (d + pd) * d * (4 + 2) + 2 * tm * d * 4
    row = pl.BlockSpec((1, d), lambda i: (0, 0))
    return pl.pallas_call(
        _ple_kernel,
        grid=(t // tm,),
        in_specs=[
            pl.BlockSpec((tm, d), lambda i: (i, 0)),
            pl.BlockSpec((tm, pd), lambda i: (i, 0)),
            row,
            _resident(w_gate.shape),
            row,
            _resident(w_proj.shape),
            row,
        ],
        out_specs=pl.BlockSpec((tm, d), lambda i: (i, 0)),
        out_shape=jax.ShapeDtypeStruct((t, d), F32),
        scratch_shapes=[pltpu.VMEM(w_gate.shape, BF16), pltpu.VMEM(w_proj.shape, BF16)],
        compiler_params=pltpu.CompilerParams(
            dimension_semantics=("arbitrary",),
            vmem_limit_bytes=_vmem_limit(est)),
    )(x, p, g_ple, w_gate, b_gate, w_proj, g_final)


def kernel(x, p, ffn1_norm, ffn1_w_gate, ffn1_w_up, ffn1_w_down, mix_norm, w_in,
           hgrn_lower_bounds, hgrn_g_norm, conv_w, conv_b, lru_w_a, lru_b_a, lru_w_x,
           lru_b_x, lru_lambda, w_out, ffn2_norm, ffn2_w_gate, ffn2_w_up, ffn2_w_down,
           ple_norm, ple_w_gate, ple_b_gate, ple_w_proj, final_norm):
    batch, seq, d = x.shape
    t = batch * seq
    depth = ffn1_norm.shape[0]
    assert depth == 1, "the shared lower-bound cumsum is specialised to one layer"
    l = 0
    bf = lambda a: a.astype(BF16)
    xt = x.reshape(t, d)

    xt = _ffn(xt, ffn1_norm[l][None], ffn1_w_gate[l], ffn1_w_up[l], ffn1_w_down[l])

    z = _in_proj(xt, mix_norm[l][None], w_in[l])
    oh = _hgrn(z, hgrn_lower_bounds, hgrn_g_norm[l][None], batch, seq)
    wax = bf(jnp.concatenate([lru_w_a[l], lru_w_x[l]], axis=-1))
    ol = _lru(z, conv_w[l], conv_b[l][None], wax, lru_b_a[l][None], lru_b_x[l][None],
              lru_lambda[l][None], batch, seq)
    xt = _out_proj(xt, oh, ol, w_out[l])

    xt = _ffn(xt, ffn2_norm[l][None], ffn2_w_gate[l], ffn2_w_up[l], ffn2_w_down[l])

    out = _ple(xt, p[l].reshape(t, -1), ple_norm[l][None], ple_w_gate[l],
               ple_b_gate[l][None], ple_w_proj[l], final_norm[None])
    return out.reshape(batch, seq, d)
```

```python
import functools

import jax
import jax.numpy as jnp
from jax import lax
from jax.experimental import pallas as pl
from jax.experimental.pallas import tpu as pltpu

F32 = jnp.float32
BF16 = jnp.bfloat16

D_MODEL = 2048
D_FF = 5632
PLE_DIM = 256
HGRN_WIDTH = D_MODEL // 2
HEAD_DIM = 128
HGRN_HEADS = HGRN_WIDTH // HEAD_DIM
LRU_WIDTH = D_MODEL - HGRN_WIDTH
LRU_BLOCKS = 8
LRU_BLOCK_DIM = LRU_WIDTH // LRU_BLOCKS
CONV_WIDTH = 4
LRU_C = 8.0
EPS = 1e-6
LOG2_E = 1.4426950408889634
SQRT_2_OVER_PI = 0.7978845608028654

V7X_VMEM_BYTES = 64 * 1024 * 1024
SUBLANES = 8
LANES = 128

FFN_TM = 1024
FFN_TF = 256
PROJ_TM = 2048
PROJ_TN = 256
ROW_TM = 512
HGRN_CHUNK = 64
HGRN_SUB = 16
MIX_TS = 1024


def _vmem_limit(nbytes):
    return int(min(V7X_VMEM_BYTES - (4 << 20), nbytes + nbytes // 4 + (2 << 20)))


def _sigmoid(x):
    return 1.0 / (1.0 + jnp.exp2(x * (-LOG2_E)))


def _gelu_tanh(x):
    return x * _sigmoid((2.0 * SQRT_2_OVER_PI) * (x * (1.0 + 0.044715 * (x * x))))


def _rmsnorm(xf, g):
    ms = jnp.mean(xf * xf, axis=-1, keepdims=True)
    return xf * lax.rsqrt(ms + EPS) * g


def _ffn_kernel(x_ref, g_ref, wg_ref, wu_ref, wd_ref, o_ref, h_ref):
    j = pl.program_id(1)

    @pl.when(j == 0)
    def _():
        x = x_ref[...]
        h_ref[...] = _rmsnorm(x, g_ref[...]).astype(BF16)
        o_ref[...] = x

    h = h_ref[...]
    gate = jnp.dot(h, wg_ref[...].astype(BF16), preferred_element_type=F32)
    up = jnp.dot(h, wu_ref[...].astype(BF16), preferred_element_type=F32)
    act = (0.5 * (gate * jax.nn.sigmoid(gate) * up)).astype(BF16)
    o_ref[...] += jnp.dot(act, wd_ref[...].astype(BF16), preferred_element_type=F32)


def _ffn(x, g, wg, wu, wd):
    t, d = x.shape
    dff = wg.shape[1]
    tm, tf = FFN_TM, FFN_TF
    wbytes = wg.dtype.itemsize
    est = (2 * tm * d * 4) * 2 + tm * d * 2 + 3 * d * tf * (2 * wbytes + 2) + 3 * tm * tf * 4
    return pl.pallas_call(
        _ffn_kernel,
        grid=(t // tm, dff // tf),
        in_specs=[
            pl.BlockSpec((tm, d), lambda i, j: (i, 0)),
            pl.BlockSpec((1, d), lambda i, j: (0, 0)),
            pl.BlockSpec((d, tf), lambda i, j: (0, j)),
            pl.BlockSpec((d, tf), lambda i, j: (0, j)),
            pl.BlockSpec((tf, d), lambda i, j: (j, 0)),
        ],
        out_specs=pl.BlockSpec((tm, d), lambda i, j: (i, 0)),
        out_shape=jax.ShapeDtypeStruct((t, d), F32),
        scratch_shapes=[pltpu.VMEM((tm, d), BF16)],
        compiler_params=pltpu.CompilerParams(
            dimension_semantics=("arbitrary", "arbitrary"),
            vmem_limit_bytes=_vmem_limit(est)),
    )(x, g, wg, wu, wd)


def _proj_kernel(x_ref, g_ref, w_ref, o_ref, h_ref):
    @pl.when(pl.program_id(1) == 0)
    def _():
        h_ref[...] = _rmsnorm(x_ref[...], g_ref[...]).astype(BF16)

    o_ref[...] = jnp.dot(h_ref[...], w_ref[...].astype(BF16), preferred_element_type=F32)


def _in_proj(x, g, w):
    t, d = x.shape
    n = w.shape[1]
    tm, tn = PROJ_TM, PROJ_TN
    est = 2 * tm * d * 4 + tm * d * 2 + d * tn * (2 * w.dtype.itemsize + 2) + 3 * tm * tn * 4
    return pl.pallas_call(
        _proj_kernel,
        grid=(t // tm, n // tn),
        in_specs=[
            pl.BlockSpec((tm, d), lambda i, j: (i, 0)),
            pl.BlockSpec((1, d), lambda i, j: (0, 0)),
            pl.BlockSpec((d, tn), lambda i, j: (0, j)),
        ],
        out_specs=pl.BlockSpec((tm, tn), lambda i, j: (i, j)),
        out_shape=jax.ShapeDtypeStruct((t, n), F32),
        scratch_shapes=[pltpu.VMEM((tm, d), BF16)],
        compiler_params=pltpu.CompilerParams(
            dimension_semantics=("arbitrary", "arbitrary"),
            vmem_limit_bytes=_vmem_limit(est)),
    )(x, g, w)


def _prefix_scan8(x):
    r = lax.broadcasted_iota(jnp.int32, x.shape, x.ndim - 2)
    d = 1
    while d < SUBLANES:
        x = x + jnp.where(r >= d, pltpu.roll(x, d, axis=x.ndim - 2), 0.0)
        d *= 2
    return x


def _hgrn_unit(zq, zf, zv, zg, lb, gn, st_ref, unit, fill):
    c, sub = HGRN_CHUNK, HGRN_SUB
    rr = zq.shape[0]
    nc = rr // c
    nv = c // SUBLANES
    assert c == 4 * sub and sub == 2 * SUBLANES
    shp4 = (nc, nv, SUBLANES, LANES)
    to4 = lambda x: x.reshape(shp4)
    f = lb + (1.0 - lb) * _sigmoid(zf)
    lf = to4(jnp.log2(f))
    kk = to4(1.0 - f)
    q = to4(zq)

    p8 = _prefix_scan8(lf)
    t8 = jnp.broadcast_to(p8[:, :, SUBLANES - 1:SUBLANES, :], shp4)
    s8 = t8 - p8
    tv = [t8[:, v] for v in range(nv)]
    pv = [p8[:, v] for v in range(nv)]
    sv = [s8[:, v] for v in range(nv)]
    p16 = [pv[v] if v % 2 == 0 else pv[v] + tv[v - 1] for v in range(nv)]
    s16 = [sv[v] + tv[v + 1] if v % 2 == 0 else sv[v] for v in range(nv)]
    t16 = [tv[2 * i] + tv[2 * i + 1] for i in range(nv // 2)]
    p32 = [p16[v] if (v // 2) % 2 == 0 else p16[v] + t16[v // 2 - 1] for v in range(nv)]
    s32 = [s16[v] + t16[v // 2 + 1] if (v // 2) % 2 == 0 else s16[v] for v in range(nv)]
    t32 = [t16[0] + t16[1], t16[2] + t16[3]]
    p64 = [p32[v] if v < nv // 2 else p32[v] + t32[0] for v in range(nv)]
    s64 = [s32[v] + t32[1] if v < nv // 2 else s32[v] for v in range(nv)]
    t64 = t32[0] + t32[1]

    st4 = lambda parts: jnp.stack(parts, axis=1)
    x32 = st4([s32[v] if v < nv // 2 else p32[v] for v in range(nv)])
    x16 = st4([s16[v] if (v // 2) % 2 == 0 else p16[v] for v in range(nv)])
    xd = st4([-sv[v] if v % 2 == 0 else pv[v] for v in range(nv)])
    qk32 = st4([kk[:, v] if v < nv // 2 else q[:, v] for v in range(nv)])
    qk16 = st4([kk[:, v] if (v // 2) % 2 == 0 else q[:, v] for v in range(nv)])

    to3 = lambda x: x.reshape(nc, c, LANES)
    m32 = to3(qk32 * jnp.exp2(x32)).astype(BF16)
    m16 = to3(qk16 * jnp.exp2(x16)).astype(BF16)
    qed = to3(q * jnp.exp2(xd)).astype(BF16)
    ked = to3(kk * jnp.exp2(-xd)).astype(BF16)
    qhat = to3(q * jnp.exp2(st4(p64))).astype(BF16)
    khat = to3(kk * jnp.exp2(st4(s64))).astype(BF16)
    dec = jnp.exp2(t64)
    vb = zv.reshape(nc, c, LANES).astype(BF16)

    fill()
    zb = jnp.zeros((sub, LANES), BF16)
    blk = lambda x, i: x[i * sub:(i + 1) * sub]
    col = lambda parts: jnp.concatenate(parts, axis=0)
    ti = lax.broadcasted_iota(jnp.int32, (c, c), 0)
    si = lax.broadcasted_iota(jnp.int32, (c, c), 1)
    diag = ((ti // sub) == (si // sub)) & (si <= ti)
    nt = (((1,), (1,)), ((), ()))
    tn = (((0,), (0,)), ((), ()))

    scores = []
    for ci in range(nc):
        a32, a16 = m32[ci], m16[ci]
        q_off = jnp.concatenate([
            col([zb, zb, blk(a32, 2), blk(a32, 3)]),
            col([zb, blk(a16, 1), zb, zb]),
            col([zb, zb, zb, blk(a16, 3)])], axis=1)
        k_off = jnp.concatenate([
            col([blk(a32, 0), blk(a32, 1), zb, zb]),
            col([blk(a16, 0), zb, zb, zb]),
            col([zb, zb, blk(a16, 2), zb])], axis=1)
        s_off = lax.dot_general(q_off, k_off, nt, preferred_element_type=F32)
        s_dia = lax.dot_general(qed[ci], ked[ci], nt, preferred_element_type=F32)
        scores.append((s_off + jnp.where(diag, s_dia, 0.0)).astype(BF16))

    fill()
    upd = [lax.dot_general(vb[ci], khat[ci], tn, preferred_element_type=F32)
           for ci in range(nc)]
    fill()
    st = st_ref[unit]
    states = []
    for ci in range(nc):
        states.append(st.astype(BF16))
        st = dec[ci, 0:1, :] * st + upd[ci]
    st_ref[unit] = st

    outs = []
    for ci in range(nc):
        outs.append(lax.dot_general(qhat[ci], states[ci], nt, preferred_element_type=F32)
                    + jnp.dot(scores[ci], vb[ci], preferred_element_type=F32))
    fill()
    o = jnp.concatenate(outs, axis=0)
    ms = jnp.mean(o * o, axis=-1, keepdims=True)
    return o * lax.rsqrt(ms + EPS) * gn * (zg * _sigmoid(zg))


def _lru_unit(zx, zgate, cw, cb, wax, ba, bx, lam, tail_ref, hc_ref, unit, fill):
    rr = zx.shape[0]
    nd = LRU_BLOCK_DIM
    tail = tail_ref[unit]
    xx = jnp.concatenate([tail, zx], axis=0)
    xc = cb + cw[CONV_WIDTH - 1:CONV_WIDTH, :] * zx
    for j in range(CONV_WIDTH - 1):
        off = SUBLANES - (CONV_WIDTH - 1) + j
        xc = xc + cw[j:j + 1, :] * xx[off:off + rr, :]
    tail_ref[unit] = zx[rr - SUBLANES:rr, :]

    fill()
    nl = -lam
    sp = jnp.maximum(nl, 0.0) + jnp.log1p(jnp.exp(-jnp.abs(nl)))
    rx = jnp.dot(xc.astype(BF16), wax, preferred_element_type=F32)
    r = _sigmoid(rx[:, :nd] + ba)
    ig = _sigmoid(rx[:, nd:] + bx)
    log_a = r * ((-LRU_C) * sp)
    a = jnp.exp(log_a)
    u = jnp.sqrt(-jnp.tanh(log_a) * (a * a + 1.0)) * (ig * xc)

    fill()
    ns = rr // SUBLANES
    a3 = a.reshape(ns, SUBLANES, LANES)
    b3 = u.reshape(ns, SUBLANES, LANES)
    row = lax.broadcasted_iota(jnp.int32, a3.shape, 1)
    d = 1
    while d < SUBLANES:
        m = row >= d
        b3 = jnp.where(m, b3 + a3 * pltpu.roll(b3, d, axis=1), b3)
        a3 = jnp.where(m, a3 * pltpu.roll(a3, d, axis=1), a3)
        d *= 2
    fill()
    carry = hc_ref[unit]
    hs = []
    for s in range(ns):
        h = b3[s] + a3[s] * carry
        hs.append(h)
        carry = jnp.broadcast_to(h[SUBLANES - 1:SUBLANES, :], h.shape)
    hc_ref[unit] = carry
    fill()
    h = jnp.concatenate(hs, axis=0)
    return h * _gelu_tanh(zgate)


def _mix_kernel(x_ref, zq_ref, zf_ref, zv_ref, zg_ref, zx_ref, zgate_ref, lbr_ref, gn_ref,
                cw_ref, cb_ref, wax_ref, ba_ref, bx_ref, lam_ref, wh_ref, wl_ref,
                o_ref, st_ref, tail_ref, hc_ref, prev_ref, *, tiles_per_seq):
    j = pl.program_id(1)
    nu = pl.num_programs(1) - 1
    n_fill = 8
    cw_out = o_ref.shape[1] // n_fill

    def project_prev(k):
        cols = slice(k * cw_out, (k + 1) * cw_out)
        w = jnp.concatenate([wh_ref[:, cols], wl_ref[:, cols]], axis=0).astype(BF16)
        o_ref[:, cols] += jnp.dot(prev_ref[(j + 1) % 2], w, preferred_element_type=F32)

    def mix(project):
        pending = list(range(n_fill)) if project else []

        def fill():
            if pending:
                project_prev(pending.pop(0))

        lbr = lbr_ref[...]
        ex = jnp.exp(lbr - jnp.max(lbr, axis=0, keepdims=True))
        lb = ex[0:1, :] / jnp.sum(ex, axis=0, keepdims=True)
        oh = _hgrn_unit(zq_ref[...], zf_ref[...], zv_ref[...], zg_ref[...], lb, gn_ref[...],
                        st_ref, j, fill)
        ol = _lru_unit(zx_ref[...], zgate_ref[...], cw_ref[...], cb_ref[...], wax_ref[0],
                       ba_ref[...], bx_ref[...], lam_ref[...], tail_ref, hc_ref, j, fill)
        assert not pending
        prev_ref[j % 2] = jnp.concatenate([oh, ol], axis=1).astype(BF16)

    @pl.when(j == 0)
    def _():
        @pl.when(pl.program_id(0) % tiles_per_seq == 0)
        def _():
            st_ref[...] = jnp.zeros_like(st_ref)
            tail_ref[...] = jnp.zeros_like(tail_ref)
            hc_ref[...] = jnp.zeros_like(hc_ref)

        o_ref[...] = x_ref[...]
        mix(False)

    @pl.when((j > 0) & (j < nu))
    def _():
        mix(True)

    @pl.when(j == nu)
    def _():
        for k in range(n_fill):
            project_prev(k)


def _mix_out(x, z, lower_bounds, g_norm, conv_w, conv_b, wax, b_a, b_x, lam, w_out, seq):
    t, d = x.shape
    ts = MIX_TS
    nu = HGRN_HEADS
    assert nu == LRU_BLOCKS and HEAD_DIM == LRU_BLOCK_DIM == LANES and seq % ts == 0
    est = 2 * 2 * ts * d * 4 + 6 * 2 * ts * LANES * 4 + 2 * 2 * LANES * d * 4 + 4 * (1 << 20)
    unit = lambda j: jnp.minimum(j, nu - 1)
    prev = lambda j: jnp.maximum(j - 1, 0)

    def zspec(part):
        return pl.BlockSpec((ts, LANES), lambda i, j, part=part: (i, part * nu + unit(j)))

    def vec(rows):
        return pl.BlockSpec((rows, LANES), lambda i, j: (0, unit(j)))

    return pl.pallas_call(
        functools.partial(_mix_kernel, tiles_per_seq=seq // ts),
        grid=(t // ts, nu + 1),
        in_specs=[pl.BlockSpec((ts, d), lambda i, j: (i, 0)),
                  zspec(0), zspec(1), zspec(2), zspec(3), zspec(4), zspec(5),
                  vec(lower_bounds.shape[0]),
                  pl.BlockSpec((1, HEAD_DIM), lambda i, j: (0, 0)),
                  vec(CONV_WIDTH), vec(1),
                  pl.BlockSpec((1,) + wax.shape[1:], lambda i, j: (unit(j), 0, 0)),
                  vec(1), vec(1), vec(1),
                  pl.BlockSpec((LANES, d), lambda i, j: (prev(j), 0)),
                  pl.BlockSpec((LANES, d), lambda i, j: (nu + prev(j), 0))],
        out_specs=pl.BlockSpec((ts, d), lambda i, j: (i, 0)),
        out_shape=jax.ShapeDtypeStruct((t, d), F32),
        scratch_shapes=[pltpu.VMEM((nu, HEAD_DIM, HEAD_DIM), F32),
                        pltpu.VMEM((nu, SUBLANES, LANES), F32),
                        pltpu.VMEM((nu, SUBLANES, LANES), F32),
                        pltpu.VMEM((2, ts, 2 * LANES), BF16)],
        compiler_params=pltpu.CompilerParams(
            dimension_semantics=("arbitrary", "arbitrary"),
            vmem_limit_bytes=_vmem_limit(est)),
    )(x, z, z, z, z, z, z, lower_bounds, g_norm, conv_w, conv_b, wax, b_a, b_x, lam,
      w_out, w_out)


def _resident(shape):
    return pl.BlockSpec(shape, lambda i: (0,) * len(shape), pipeline_mode=pl.Buffered(1))


def _ple_kernel(x_ref, p_ref, gp_ref, wg_ref, bg_ref, wp_ref, gf_ref, o_ref,
                wgb_ref, wpb_ref):
    @pl.when(pl.program_id(0) == 0)
    def _():
        wgb_ref[...] = wg_ref[...].astype(BF16)
        wpb_ref[...] = wp_ref[...].astype(BF16)

    x = x_ref[...]
    h = _rmsnorm(x, gp_ref[...]).astype(BF16)
    gate = jax.nn.sigmoid(
        jnp.dot(h, wgb_ref[...], preferred_element_type=F32) + bg_ref[...])
    emb = jnp.dot(p_ref[...].astype(BF16), wpb_ref[...], preferred_element_type=F32)
    o_ref[...] = _rmsnorm(x + gate * emb, gf_ref[...])


def _ple(x, p, g_ple, w_gate, b_gate, w_proj, g_final):
    t, d = x.shape
    pd = p.shape[1]
    tm = ROW_TM
    est = 2 * 2 * tm * d * 4 + 2 * tm * pd * 4 + (d + pd) * d * (4 + 2) + 2 * tm * d * 4
    row = pl.BlockSpec((1, d), lambda i: (0, 0))
    return pl.pallas_call(
        _ple_kernel,
        grid=(t // tm,),
        in_specs=[
            pl.BlockSpec((tm, d), lambda i: (i, 0)),
            pl.BlockSpec((tm, pd), lambda i: (i, 0)),
            row,
            _resident(w_gate.shape),
            row,
            _resident(w_proj.shape),
            row,
        ],
        out_specs=pl.BlockSpec((tm, d), lambda i: (i, 0)),
        out_shape=jax.ShapeDtypeStruct((t, d), F32),
        scratch_shapes=[pltpu.VMEM(w_gate.shape, BF16), pltpu.VMEM(w_proj.shape, BF16)],
        compiler_params=pltpu.CompilerParams(
            dimension_semantics=("arbitrary",),
            vmem_limit_bytes=_vmem_limit(est)),
    )(x, p, g_ple, w_gate, b_gate, w_proj, g_final)


def kernel(x, p, ffn1_norm, ffn1_w_gate, ffn1_w_up, ffn1_w_down, mix_norm, w_in,
           hgrn_lower_bounds, hgrn_g_norm, conv_w, conv_b, lru_w_a, lru_b_a, lru_w_x,
           lru_b_x, lru_lambda, w_out, ffn2_norm, ffn2_w_gate, ffn2_w_up, ffn2_w_down,
           ple_norm, ple_w_gate, ple_b_gate, ple_w_proj, final_norm):
    batch, seq, d = x.shape
    t = batch * seq
    depth = ffn1_norm.shape[0]
    assert depth == 1, "the shared lower-bound cumsum is specialised to one layer"
    l = 0
    bf = lambda a: a.astype(BF16)
    xt = x.reshape(t, d)

    xt = _ffn(xt, ffn1_norm[l][None], ffn1_w_gate[l], ffn1_w_up[l], ffn1_w_down[l])

    z = _in_proj(xt, mix_norm[l][None], w_in[l])
    wax = bf(jnp.concatenate([lru_w_a[l], lru_w_x[l]], axis=-1))
    xt = _mix_out(xt, z, hgrn_lower_bounds, hgrn_g_norm[l][None], conv_w[l], conv_b[l][None],
                  wax, lru_b_a[l][None], lru_b_x[l][None], lru_lambda[l][None], w_out[l], seq)

    xt = _ffn(xt, ffn2_norm[l][None], ffn2_w_gate[l], ffn2_w_up[l], ffn2_w_down[l])

    out = _ple(xt, p[l].reshape(t, -1), ple_norm[l][None], ple_w_gate[l],
               ple_b_gate[l][None], ple_w_proj[l], final_norm[None])
    return out.reshape(batch, seq, d)
```

```python
import functools

import jax
import jax.numpy as jnp
from jax import lax
from jax.experimental import pallas as pl
from jax.experimental.pallas import tpu as pltpu

F32 = jnp.float32
BF16 = jnp.bfloat16

D_MODEL = 2048
D_FF = 5632
PLE_DIM = 256
HGRN_WIDTH = D_MODEL // 2
HEAD_DIM = 128
HGRN_HEADS = HGRN_WIDTH // HEAD_DIM
LRU_WIDTH = D_MODEL - HGRN_WIDTH
LRU_BLOCKS = 8
LRU_BLOCK_DIM = LRU_WIDTH // LRU_BLOCKS
CONV_WIDTH = 4
LRU_C = 8.0
EPS = 1e-6
LOG2_E = 1.4426950408889634
SQRT_2_OVER_PI = 0.7978845608028654

V7X_VMEM_BYTES = 64 * 1024 * 1024
SUBLANES = 8
LANES = 128

FFN_TM = 1024
FFN_TF = 256
PROJ_TM = 2048
PROJ_TN = 256
ROW_TM = 512
HGRN_CHUNK = 64
HGRN_SUB = 16
MIX_TS = 1024


def _vmem_limit(nbytes):
    return int(min(V7X_VMEM_BYTES - (4 << 20), nbytes + nbytes // 4 + (2 << 20)))


def _sigmoid(x):
    return 1.0 / (1.0 + jnp.exp2(x * (-LOG2_E)))


def _gelu_tanh(x):
    return x * _sigmoid((2.0 * SQRT_2_OVER_PI) * (x * (1.0 + 0.044715 * (x * x))))


def _rmsnorm(xf, g):
    ms = jnp.mean(xf * xf, axis=-1, keepdims=True)
    return xf * lax.rsqrt(ms + EPS) * g


def _ffn_kernel(x_ref, g_ref, wg_ref, wu_ref, wd_ref, o_ref, h_ref):
    j = pl.program_id(1)

    @pl.when(j == 0)
    def _():
        x = x_ref[...]
        h_ref[...] = _rmsnorm(x, g_ref[...]).astype(BF16)
        o_ref[...] = x

    h = h_ref[...]
    gate = jnp.dot(h, wg_ref[...].astype(BF16), preferred_element_type=F32)
    up = jnp.dot(h, wu_ref[...].astype(BF16), preferred_element_type=F32)
    act = (0.5 * (gate * jax.nn.sigmoid(gate) * up)).astype(BF16)
    o_ref[...] += jnp.dot(act, wd_ref[...].astype(BF16), preferred_element_type=F32)


def _ffn(x, g, wg, wu, wd):
    t, d = x.shape
    dff = wg.shape[1]
    tm, tf = FFN_TM, FFN_TF
    wbytes = wg.dtype.itemsize
    est = (2 * tm * d * 4) * 2 + tm * d * 2 + 3 * d * tf * (2 * wbytes + 2) + 3 * tm * tf * 4
    return pl.pallas_call(
        _ffn_kernel,
        grid=(t // tm, dff // tf),
        in_specs=[
            pl.BlockSpec((tm, d), lambda i, j: (i, 0)),
            pl.BlockSpec((1, d), lambda i, j: (0, 0)),
            pl.BlockSpec((d, tf), lambda i, j: (0, j)),
            pl.BlockSpec((d, tf), lambda i, j: (0, j)),
            pl.BlockSpec((tf, d), lambda i, j: (j, 0)),
        ],
        out_specs=pl.BlockSpec((tm, d), lambda i, j: (i, 0)),
        out_shape=jax.ShapeDtypeStruct((t, d), F32),
        scratch_shapes=[pltpu.VMEM((tm, d), BF16)],
        compiler_params=pltpu.CompilerParams(
            dimension_semantics=("arbitrary", "arbitrary"),
            vmem_limit_bytes=_vmem_limit(est)),
    )(x, g, wg, wu, wd)


def _proj_kernel(x_ref, g_ref, w_ref, o_ref, h_ref):
    @pl.when(pl.program_id(1) == 0)
    def _():
        h_ref[...] = _rmsnorm(x_ref[...], g_ref[...]).astype(BF16)

    o_ref[...] = jnp.dot(h_ref[...], w_ref[...].astype(BF16), preferred_element_type=F32)


def _in_proj(x, g, w):
    t, d = x.shape
    n = w.shape[1]
    tm, tn = PROJ_TM, PROJ_TN
    est = 2 * tm * d * 4 + tm * d * 2 + d * tn * (2 * w.dtype.itemsize + 2) + 3 * tm * tn * 4
    return pl.pallas_call(
        _proj_kernel,
        grid=(t // tm, n // tn),
        in_specs=[
            pl.BlockSpec((tm, d), lambda i, j: (i, 0)),
            pl.BlockSpec((1, d), lambda i, j: (0, 0)),
            pl.BlockSpec((d, tn), lambda i, j: (0, j)),
        ],
        out_specs=pl.BlockSpec((tm, tn), lambda i, j: (i, j)),
        out_shape=jax.ShapeDtypeStruct((t, n), F32),
        scratch_shapes=[pltpu.VMEM((tm, d), BF16)],
        compiler_params=pltpu.CompilerParams(
            dimension_semantics=("arbitrary", "arbitrary"),
            vmem_limit_bytes=_vmem_limit(est)),
    )(x, g, w)


def _prefix_scan8(x):
    r = lax.broadcasted_iota(jnp.int32, x.shape, x.ndim - 2)
    d = 1
    while d < SUBLANES:
        x = x + jnp.where(r >= d, pltpu.roll(x, d, axis=x.ndim - 2), 0.0)
        d *= 2
    return x


def _hgrn_unit(zq, zf, zv, zg, lb, gn, st_ref, unit):
    c, sub = HGRN_CHUNK, HGRN_SUB
    rr = zq.shape[0]
    nc = rr // c
    nv = c // SUBLANES
    assert c == 4 * sub and sub == 2 * SUBLANES
    shp4 = (nc, nv, SUBLANES, LANES)
    to4 = lambda x: x.reshape(shp4)
    f = lb + (1.0 - lb) * _sigmoid(zf)
    lf = to4(jnp.log2(f))
    kk = to4(1.0 - f)
    q = to4(zq)

    p8 = _prefix_scan8(lf)
    t8 = jnp.broadcast_to(p8[:, :, SUBLANES - 1:SUBLANES, :], shp4)
    s8 = t8 - p8
    tv = [t8[:, v] for v in range(nv)]
    pv = [p8[:, v] for v in range(nv)]
    sv = [s8[:, v] for v in range(nv)]
    p16 = [pv[v] if v % 2 == 0 else pv[v] + tv[v - 1] for v in range(nv)]
    s16 = [sv[v] + tv[v + 1] if v % 2 == 0 else sv[v] for v in range(nv)]
    t16 = [tv[2 * i] + tv[2 * i + 1] for i in range(nv // 2)]
    p32 = [p16[v] if (v // 2) % 2 == 0 else p16[v] + t16[v // 2 - 1] for v in range(nv)]
    s32 = [s16[v] + t16[v // 2 + 1] if (v // 2) % 2 == 0 else s16[v] for v in range(nv)]
    t32 = [t16[0] + t16[1], t16[2] + t16[3]]
    p64 = [p32[v] if v < nv // 2 else p32[v] + t32[0] for v in range(nv)]
    s64 = [s32[v] + t32[1] if v < nv // 2 else s32[v] for v in range(nv)]
    t64 = t32[0] + t32[1]

    st4 = lambda parts: jnp.stack(parts, axis=1)
    x32 = st4([s32[v] if v < nv // 2 else p32[v] for v in range(nv)])
    x16 = st4([s16[v] if (v // 2) % 2 == 0 else p16[v] for v in range(nv)])
    xd = st4([-sv[v] if v % 2 == 0 else pv[v] for v in range(nv)])
    qk32 = st4([kk[:, v] if v < nv // 2 else q[:, v] for v in range(nv)])
    qk16 = st4([kk[:, v] if (v // 2) % 2 == 0 else q[:, v] for v in range(nv)])

    to3 = lambda x: x.reshape(nc, c, LANES)
    m32 = to3(qk32 * jnp.exp2(x32)).astype(BF16)
    m16 = to3(qk16 * jnp.exp2(x16)).astype(BF16)
    qed = to3(q * jnp.exp2(xd)).astype(BF16)
    ked = to3(kk * jnp.exp2(-xd)).astype(BF16)
    qhat = to3(q * jnp.exp2(st4(p64))).astype(BF16)
    khat = to3(kk * jnp.exp2(st4(s64))).astype(BF16)
    dec = jnp.exp2(t64)
    vb = zv.reshape(nc, c, LANES).astype(BF16)

    zb = jnp.zeros((sub, LANES), BF16)
    blk = lambda x, i: x[i * sub:(i + 1) * sub]
    col = lambda parts: jnp.concatenate(parts, axis=0)
    ti = lax.broadcasted_iota(jnp.int32, (c, c), 0)
    si = lax.broadcasted_iota(jnp.int32, (c, c), 1)
    diag = ((ti // sub) == (si // sub)) & (si <= ti)
    nt = (((1,), (1,)), ((), ()))
    tn = (((0,), (0,)), ((), ()))

    scores = []
    for ci in range(nc):
        a32, a16 = m32[ci], m16[ci]
        q_off = jnp.concatenate([
            col([zb, zb, blk(a32, 2), blk(a32, 3)]),
            col([zb, blk(a16, 1), zb, zb]),
            col([zb, zb, zb, blk(a16, 3)])], axis=1)
        k_off = jnp.concatenate([
            col([blk(a32, 0), blk(a32, 1), zb, zb]),
            col([blk(a16, 0), zb, zb, zb]),
            col([zb, zb, blk(a16, 2), zb])], axis=1)
        s_off = lax.dot_general(q_off, k_off, nt, preferred_element_type=F32)
        s_dia = lax.dot_general(qed[ci], ked[ci], nt, preferred_element_type=F32)
        scores.append((s_off + jnp.where(diag, s_dia, 0.0)).astype(BF16))

    upd = [lax.dot_general(vb[ci], khat[ci], tn, preferred_element_type=F32)
           for ci in range(nc)]
    st = st_ref[unit]
    states = []
    for ci in range(nc):
        states.append(st.astype(BF16))
        st = dec[ci, 0:1, :] * st + upd[ci]
    st_ref[unit] = st

    outs = []
    for ci in range(nc):
        outs.append(lax.dot_general(qhat[ci], states[ci], nt, preferred_element_type=F32)
                    + jnp.dot(scores[ci], vb[ci], preferred_element_type=F32))
    o = jnp.concatenate(outs, axis=0)
    ms = jnp.mean(o * o, axis=-1, keepdims=True)
    return o * lax.rsqrt(ms + EPS) * gn * (zg * _sigmoid(zg))


def _lru_unit(zx, zgate, cw, cb, wax, ba, bx, lam, tail_ref, hc_ref, unit):
    rr = zx.shape[0]
    nd = LRU_BLOCK_DIM
    tail = tail_ref[unit]
    xx = jnp.concatenate([tail, zx], axis=0)
    xc = cb + cw[CONV_WIDTH - 1:CONV_WIDTH, :] * zx
    for j in range(CONV_WIDTH - 1):
        off = SUBLANES - (CONV_WIDTH - 1) + j
        xc = xc + cw[j:j + 1, :] * xx[off:off + rr, :]
    tail_ref[unit] = zx[rr - SUBLANES:rr, :]

    nl = -lam
    sp = jnp.maximum(nl, 0.0) + jnp.log1p(jnp.exp(-jnp.abs(nl)))
    rx = jnp.dot(xc.astype(BF16), wax, preferred_element_type=F32)
    r = _sigmoid(rx[:, :nd] + ba)
    ig = _sigmoid(rx[:, nd:] + bx)
    log_a = r * ((-LRU_C) * sp)
    a = jnp.exp(log_a)
    u = jnp.sqrt(-jnp.tanh(log_a) * (a * a + 1.0)) * (ig * xc)

    ns = rr // SUBLANES
    a3 = a.reshape(ns, SUBLANES, LANES)
    b3 = u.reshape(ns, SUBLANES, LANES)
    row = lax.broadcasted_iota(jnp.int32, a3.shape, 1)
    d = 1
    while d < SUBLANES:
        m = row >= d
        b3 = jnp.where(m, b3 + a3 * pltpu.roll(b3, d, axis=1), b3)
        a3 = jnp.where(m, a3 * pltpu.roll(a3, d, axis=1), a3)
        d *= 2
    carry = hc_ref[unit]
    hs = []
    for s in range(ns):
        h = b3[s] + a3[s] * carry
        hs.append(h)
        carry = jnp.broadcast_to(h[SUBLANES - 1:SUBLANES, :], h.shape)
    hc_ref[unit] = carry
    h = jnp.concatenate(hs, axis=0)
    return h * _gelu_tanh(zgate)


def _mixer_kernel(zq_ref, zf_ref, zv_ref, zg_ref, zx_ref, zgate_ref, lbr_ref, gn_ref,
                  cw_ref, cb_ref, wax_ref, ba_ref, bx_ref, lam_ref,
                  oh_ref, ol_ref, st_ref, tail_ref, hc_ref, *, tiles_per_seq):
    unit = pl.program_id(1)

    @pl.when(pl.program_id(0) % tiles_per_seq == 0)
    def _():
        st_ref[unit] = jnp.zeros(st_ref.shape[1:], F32)
        tail_ref[unit] = jnp.zeros(tail_ref.shape[1:], F32)
        hc_ref[unit] = jnp.zeros(hc_ref.shape[1:], F32)

    lbr = lbr_ref[...]
    ex = jnp.exp(lbr - jnp.max(lbr, axis=0, keepdims=True))
    lb = ex[0:1, :] / jnp.sum(ex, axis=0, keepdims=True)
    oh = _hgrn_unit(zq_ref[...], zf_ref[...], zv_ref[...], zg_ref[...], lb, gn_ref[...],
                    st_ref, unit)
    oh_ref[...] = oh.astype(oh_ref.dtype)
    ol = _lru_unit(zx_ref[...], zgate_ref[...], cw_ref[...], cb_ref[...], wax_ref[0],
                   ba_ref[...], bx_ref[...], lam_ref[...], tail_ref, hc_ref, unit)
    ol_ref[...] = ol.astype(ol_ref.dtype)


def _mixer(z, lower_bounds, g_norm, conv_w, conv_b, wax, b_a, b_x, lam, seq):
    t = z.shape[0]
    ts = MIX_TS
    nu = HGRN_HEADS
    assert nu == LRU_BLOCKS and HEAD_DIM == LRU_BLOCK_DIM == LANES and seq % ts == 0
    est = 6 * 2 * ts * LANES * 4 + 2 * 2 * ts * LANES * 2 + 4 * (1 << 20)

    def zspec(part):
        return pl.BlockSpec((ts, LANES), lambda i, j, part=part: (i, part * nu + j))

    def vec(rows):
        return pl.BlockSpec((rows, LANES), lambda i, j: (0, j))

    return pl.pallas_call(
        functools.partial(_mixer_kernel, tiles_per_seq=seq // ts),
        grid=(t // ts, nu),
        in_specs=[zspec(0), zspec(1), zspec(2), zspec(3), zspec(4), zspec(5),
                  vec(lower_bounds.shape[0]),
                  pl.BlockSpec((1, HEAD_DIM), lambda i, j: (0, 0)),
                  vec(CONV_WIDTH), vec(1),
                  pl.BlockSpec((1,) + wax.shape[1:], lambda i, j: (j, 0, 0)),
                  vec(1), vec(1), vec(1)],
        out_specs=[pl.BlockSpec((ts, LANES), lambda i, j: (i, j)),
                   pl.BlockSpec((ts, LANES), lambda i, j: (i, j))],
        out_shape=[jax.ShapeDtypeStruct((t, HGRN_WIDTH), BF16),
                   jax.ShapeDtypeStruct((t, LRU_WIDTH), BF16)],
        scratch_shapes=[pltpu.VMEM((nu, HEAD_DIM, HEAD_DIM), F32),
                        pltpu.VMEM((nu, SUBLANES, LANES), F32),
                        pltpu.VMEM((nu, SUBLANES, LANES), F32)],
        compiler_params=pltpu.CompilerParams(
            dimension_semantics=("arbitrary", "arbitrary"),
            vmem_limit_bytes=_vmem_limit(est)),
    )(z, z, z, z, z, z, lower_bounds, g_norm, conv_w, conv_b, wax, b_a, b_x, lam)


def _resident(shape):
    return pl.BlockSpec(shape, lambda i: (0,) * len(shape), pipeline_mode=pl.Buffered(1))


def _outproj_kernel(x_ref, oh_ref, ol_ref, w_ref, o_ref, wb_ref):
    @pl.when(pl.program_id(0) == 0)
    def _():
        wb_ref[...] = w_ref[...].astype(BF16)

    kh = oh_ref.shape[1]
    o_ref[...] = (x_ref[...]
                  + jnp.dot(oh_ref[...], wb_ref[:kh, :], preferred_element_type=F32)
                  + jnp.dot(ol_ref[...], wb_ref[kh:, :], preferred_element_type=F32))


def _out_proj(x, oh, ol, w_out):
    t, d = x.shape
    tm = ROW_TM
    kh, kl = oh.shape[1], ol.shape[1]
    est = 2 * 2 * tm * d * 4 + 2 * tm * (kh + kl) * 2 + (kh + kl) * d * (4 + 2)
    return pl.pallas_call(
        _outproj_kernel,
        grid=(t // tm,),
        in_specs=[
            pl.BlockSpec((tm, d), lambda i: (i, 0)),
            pl.BlockSpec((tm, kh), lambda i: (i, 0)),
            pl.BlockSpec((tm, kl), lambda i: (i, 0)),
            _resident(w_out.shape),
        ],
        out_specs=pl.BlockSpec((tm, d), lambda i: (i, 0)),
        out_shape=jax.ShapeDtypeStruct((t, d), F32),
        scratch_shapes=[pltpu.VMEM(w_out.shape, BF16)],
        compiler_params=pltpu.CompilerParams(
            dimension_semantics=("arbitrary",),
            vmem_limit_bytes=_vmem_limit(est)),
    )(x, oh, ol, w_out)


def _ple_kernel(x_ref, p_ref, gp_ref, wg_ref, bg_ref, wp_ref, gf_ref, o_ref,
                wgb_ref, wpb_ref):
    @pl.when(pl.program_id(0) == 0)
    def _():
        wgb_ref[...] = wg_ref[...].astype(BF16)
        wpb_ref[...] = wp_ref[...].astype(BF16)

    x = x_ref[...]
    h = _rmsnorm(x, gp_ref[...]).astype(BF16)
    gate = jax.nn.sigmoid(
        jnp.dot(h, wgb_ref[...], preferred_element_type=F32) + bg_ref[...])
    emb = jnp.dot(p_ref[...].astype(BF16), wpb_ref[...], preferred_element_type=F32)
    o_ref[...] = _rmsnorm(x + gate * emb, gf_ref[...])


def _ple(x, p, g_ple, w_gate, b_gate, w_proj, g_final):
    t, d = x.shape
    pd = p.shape[1]
    tm = ROW_TM
    est = 2 * 2 * tm * d * 4 + 2 * tm * pd * 4 + (d + pd) * d * (4 + 2) + 2 * tm * d * 4
    row = pl.BlockSpec((1, d), lambda i: (0, 0))
    return pl.pallas_call(
        _ple_kernel,
        grid=(t // tm,),
        in_specs=[
            pl.BlockSpec((tm, d), lambda i: (i, 0)),
            pl.BlockSpec((tm, pd), lambda i: (i, 0)),
            row,
            _resident(w_gate.shape),
            row,
            _resident(w_proj.shape),
            row,
        ],
        out_specs=pl.BlockSpec((tm, d), lambda i: (i, 0)),
        out_shape=jax.ShapeDtypeStruct((t, d), F32),
        scratch_shapes=[pltpu.VMEM(w_gate.shape, BF16), pltpu.VMEM(w_proj.shape, BF16)],
        compiler_params=pltpu.CompilerParams(
            dimension_semantics=("arbitrary",),
            vmem_limit_bytes=_vmem_limit(est)),
    )(x, p, g_ple, w_gate, b_gate, w_proj, g_final)


def kernel(x, p, ffn1_norm, ffn1_w_gate, ffn1_w_up, ffn1_w_down, mix_norm, w_in,
           hgrn_lower_bounds, hgrn_g_norm, conv_w, conv_b, lru_w_a, lru_b_a, lru_w_x,
           lru_b_x, lru_lambda, w_out, ffn2_norm, ffn2_w_gate, ffn2_w_up, ffn2_w_down,
           ple_norm, ple_w_gate, ple_b_gate, ple_w_proj, final_norm):
    batch, seq, d = x.shape
    t = batch * seq
    depth = ffn1_norm.shape[0]
    assert depth == 1, "the shared lower-bound cumsum is specialised to one layer"
    l = 0
    bf = lambda a: a.astype(BF16)
    xt = x.reshape(t, d)

    xt = _ffn(xt, ffn1_norm[l][None], ffn1_w_gate[l], ffn1_w_up[l], ffn1_w_down[l])

    z = _in_proj(xt, mix_norm[l][None], w_in[l])
    wax = bf(jnp.concatenate([lru_w_a[l], lru_w_x[l]], axis=-1))
    oh, ol = _mixer(z, hgrn_lower_bounds, hgrn_g_norm[l][None], conv_w[l], conv_b[l][None], wax,
                    lru_b_a[l][None], lru_b_x[l][None], lru_lambda[l][None], seq)
    xt = _out_proj(xt, oh, ol, w_out[l])

    xt = _ffn(xt, ffn2_norm[l][None], ffn2_w_gate[l], ffn2_w_up[l], ffn2_w_down[l])

    out = _ple(xt, p[l].reshape(t, -1), ple_norm[l][None], ple_w_gate[l],
               ple_b_gate[l][None], ple_w_proj[l], final_norm[None])
    return out.reshape(batch, seq, d)
```

```python
import functools

import jax
import jax.numpy as jnp
from jax import lax
from jax.experimental import pallas as pl
from jax.experimental.pallas import tpu as pltpu

F32 = jnp.float32
BF16 = jnp.bfloat16

D_MODEL = 2048
D_FF = 5632
PLE_DIM = 256
HGRN_WIDTH = D_MODEL // 2
HEAD_DIM = 128
HGRN_HEADS = HGRN_WIDTH // HEAD_DIM
LRU_WIDTH = D_MODEL - HGRN_WIDTH
LRU_BLOCKS = 8
LRU_BLOCK_DIM = LRU_WIDTH // LRU_BLOCKS
CONV_WIDTH = 4
LRU_C = 8.0
EPS = 1e-6
LOG2_E = 1.4426950408889634
SQRT_2_OVER_PI = 0.7978845608028654

V7X_VMEM_BYTES = 64 * 1024 * 1024
SUBLANES = 8
LANES = 128

FFN_TM = 1024
FFN_TF = 256
PROJ_TM = 2048
PROJ_TN = 256
ROW_TM = 512
HGRN_CHUNK = 64
HGRN_SUB = 16
MIX_TS = 1024


def _vmem_limit(nbytes):
    return int(min(V7X_VMEM_BYTES - (4 << 20), nbytes + nbytes // 4 + (2 << 20)))


def _sigmoid(x):
    return 1.0 / (1.0 + jnp.exp2(x * (-LOG2_E)))


def _gelu_tanh(x):
    return x * _sigmoid((2.0 * SQRT_2_OVER_PI) * (x * (1.0 + 0.044715 * (x * x))))


def _rmsnorm(xf, g):
    ms = jnp.mean(xf * xf, axis=-1, keepdims=True)
    return xf * lax.rsqrt(ms + EPS) * g


def _ffn_kernel(x_ref, g_ref, wg_ref, wu_ref, wd_ref, o_ref, h_ref):
    j = pl.program_id(1)

    @pl.when(j == 0)
    def _():
        x = x_ref[...]
        h_ref[...] = _rmsnorm(x, g_ref[...]).astype(BF16)
        o_ref[...] = x

    h = h_ref[...]
    gate = jnp.dot(h, wg_ref[...].astype(BF16), preferred_element_type=F32)
    up = jnp.dot(h, wu_ref[...].astype(BF16), preferred_element_type=F32)
    act = (0.5 * (gate * jax.nn.sigmoid(gate) * up)).astype(BF16)
    o_ref[...] += jnp.dot(act, wd_ref[...].astype(BF16), preferred_element_type=F32)


def _ffn(x, g, wg, wu, wd):
    t, d = x.shape
    dff = wg.shape[1]
    tm, tf = FFN_TM, FFN_TF
    wbytes = wg.dtype.itemsize
    est = (2 * tm * d * 4) * 2 + tm * d * 2 + 3 * d * tf * (2 * wbytes + 2) + 3 * tm * tf * 4
    return pl.pallas_call(
        _ffn_kernel,
        grid=(t // tm, dff // tf),
        in_specs=[
            pl.BlockSpec((tm, d), lambda i, j: (i, 0)),
            pl.BlockSpec((1, d), lambda i, j: (0, 0)),
            pl.BlockSpec((d, tf), lambda i, j: (0, j)),
            pl.BlockSpec((d, tf), lambda i, j: (0, j)),
            pl.BlockSpec((tf, d), lambda i, j: (j, 0)),
        ],
        out_specs=pl.BlockSpec((tm, d), lambda i, j: (i, 0)),
        out_shape=jax.ShapeDtypeStruct((t, d), F32),
        scratch_shapes=[pltpu.VMEM((tm, d), BF16)],
        compiler_params=pltpu.CompilerParams(
            dimension_semantics=("arbitrary", "arbitrary"),
            vmem_limit_bytes=_vmem_limit(est)),
    )(x, g, wg, wu, wd)


def _proj_kernel(x_ref, g_ref, w_ref, o_ref, h_ref):
    @pl.when(pl.program_id(1) == 0)
    def _():
        h_ref[...] = _rmsnorm(x_ref[...], g_ref[...]).astype(BF16)

    o_ref[...] = jnp.dot(h_ref[...], w_ref[...].astype(BF16), preferred_element_type=F32)


def _in_proj(x, g, w):
    t, d = x.shape
    n = w.shape[1]
    tm, tn = PROJ_TM, PROJ_TN
    est = 2 * tm * d * 4 + tm * d * 2 + d * tn * (2 * w.dtype.itemsize + 2) + 3 * tm * tn * 4
    return pl.pallas_call(
        _proj_kernel,
        grid=(t // tm, n // tn),
        in_specs=[
            pl.BlockSpec((tm, d), lambda i, j: (i, 0)),
            pl.BlockSpec((1, d), lambda i, j: (0, 0)),
            pl.BlockSpec((d, tn), lambda i, j: (0, j)),
        ],
        out_specs=pl.BlockSpec((tm, tn), lambda i, j: (i, j)),
        out_shape=jax.ShapeDtypeStruct((t, n), F32),
        scratch_shapes=[pltpu.VMEM((tm, d), BF16)],
        compiler_params=pltpu.CompilerParams(
            dimension_semantics=("arbitrary", "arbitrary"),
            vmem_limit_bytes=_vmem_limit(est)),
    )(x, g, w)


def _prefix_scan8(x):
    r = lax.broadcasted_iota(jnp.int32, x.shape, x.ndim - 2)
    d = 1
    while d < SUBLANES:
        x = x + jnp.where(r >= d, pltpu.roll(x, d, axis=x.ndim - 2), 0.0)
        d *= 2
    return x


def _hgrn_unit(zq, zf, zv, zg, lb, gn, st_ref, unit):
    c, sub = HGRN_CHUNK, HGRN_SUB
    rr = zq.shape[0]
    nc = rr // c
    nv = c // SUBLANES
    assert c == 4 * sub and sub == 2 * SUBLANES
    shp4 = (nc, nv, SUBLANES, LANES)
    to4 = lambda x: x.reshape(shp4)
    f = lb + (1.0 - lb) * _sigmoid(zf)
    lf = to4(jnp.log2(f))
    kk = to4(1.0 - f)
    q = to4(zq)

    p8 = _prefix_scan8(lf)
    t8 = jnp.broadcast_to(p8[:, :, SUBLANES - 1:SUBLANES, :], shp4)
    s8 = t8 - p8
    tv = [t8[:, v] for v in range(nv)]
    pv = [p8[:, v] for v in range(nv)]
    sv = [s8[:, v] for v in range(nv)]
    p16 = [pv[v] if v % 2 == 0 else pv[v] + tv[v - 1] for v in range(nv)]
    s16 = [sv[v] + tv[v + 1] if v % 2 == 0 else sv[v] for v in range(nv)]
    t16 = [tv[2 * i] + tv[2 * i + 1] for i in range(nv // 2)]
    p32 = [p16[v] if (v // 2) % 2 == 0 else p16[v] + t16[v // 2 - 1] for v in range(nv)]
    s32 = [s16[v] + t16[v // 2 + 1] if (v // 2) % 2 == 0 else s16[v] for v in range(nv)]
    t32 = [t16[0] + t16[1], t16[2] + t16[3]]
    p64 = [p32[v] if v < nv // 2 else p32[v] + t32[0] for v in range(nv)]
    s64 = [s32[v] + t32[1] if v < nv // 2 else s32[v] for v in range(nv)]
    t64 = t32[0] + t32[1]

    st4 = lambda parts: jnp.stack(parts, axis=1)
    x32 = st4([s32[v] if v < nv // 2 else p32[v] for v in range(nv)])
    x16 = st4([s16[v] if (v // 2) % 2 == 0 else p16[v] for v in range(nv)])
    xd = st4([-sv[v] if v % 2 == 0 else pv[v] for v in range(nv)])
    qk32 = st4([kk[:, v] if v < nv // 2 else q[:, v] for v in range(nv)])
    qk16 = st4([kk[:, v] if (v // 2) % 2 == 0 else q[:, v] for v in range(nv)])

    to3 = lambda x: x.reshape(nc, c, LANES)
    m32 = to3(qk32 * jnp.exp2(x32)).astype(BF16)
    m16 = to3(qk16 * jnp.exp2(x16)).astype(BF16)
    qed = to3(q * jnp.exp2(xd)).astype(BF16)
    ked = to3(kk * jnp.exp2(-xd)).astype(BF16)
    qhat = to3(q * jnp.exp2(st4(p64))).astype(BF16)
    khat = to3(kk * jnp.exp2(st4(s64))).astype(BF16)
    dec = jnp.exp2(t64)
    vb = zv.reshape(nc, c, LANES).astype(BF16)

    zb = jnp.zeros((sub, LANES), BF16)
    blk = lambda x, i: x[i * sub:(i + 1) * sub]
    col = lambda parts: jnp.concatenate(parts, axis=0)
    ti = lax.broadcasted_iota(jnp.int32, (c, c), 0)
    si = lax.broadcasted_iota(jnp.int32, (c, c), 1)
    diag = ((ti // sub) == (si // sub)) & (si <= ti)
    nt = (((1,), (1,)), ((), ()))
    tn = (((0,), (0,)), ((), ()))

    scores = []
    for ci in range(nc):
        a32, a16 = m32[ci], m16[ci]
        q_off = jnp.concatenate([
            col([zb, zb, blk(a32, 2), blk(a32, 3)]),
            col([zb, blk(a16, 1), zb, zb]),
            col([zb, zb, zb, blk(a16, 3)])], axis=1)
        k_off = jnp.concatenate([
            col([blk(a32, 0), blk(a32, 1), zb, zb]),
            col([blk(a16, 0), zb, zb, zb]),
            col([zb, zb, blk(a16, 2), zb])], axis=1)
        s_off = lax.dot_general(q_off, k_off, nt, preferred_element_type=F32)
        s_dia = lax.dot_general(qed[ci], ked[ci], nt, preferred_element_type=F32)
        scores.append((s_off + jnp.where(diag, s_dia, 0.0)).astype(BF16))

    upd = [lax.dot_general(vb[ci], khat[ci], tn, preferred_element_type=F32)
           for ci in range(nc)]
    st = st_ref[unit]
    states = []
    for ci in range(nc):
        states.append(st.astype(BF16))
        st = dec[ci, 0:1, :] * st + upd[ci]
    st_ref[unit] = st

    outs = []
    for ci in range(nc):
        outs.append(lax.dot_general(qhat[ci], states[ci], nt, preferred_element_type=F32)
                    + jnp.dot(scores[ci], vb[ci], preferred_element_type=F32))
    o = jnp.concatenate(outs, axis=0)
    ms = jnp.mean(o * o, axis=-1, keepdims=True)
    return o * lax.rsqrt(ms + EPS) * gn * (zg * _sigmoid(zg))


def _lru_unit(zx, zgate, cw, cb, wax, ba, bx, lam, tail_ref, hc_ref, unit):
    rr = zx.shape[0]
    nd = LRU_BLOCK_DIM
    tail = tail_ref[unit]
    xx = jnp.concatenate([tail, zx], axis=0)
    xc = cb + cw[CONV_WIDTH - 1:CONV_WIDTH, :] * zx
    for j in range(CONV_WIDTH - 1):
        off = SUBLANES - (CONV_WIDTH - 1) + j
        xc = xc + cw[j:j + 1, :] * xx[off:off + rr, :]
    tail_ref[unit] = zx[rr - SUBLANES:rr, :]

    nl = -lam
    sp = jnp.maximum(nl, 0.0) + jnp.log1p(jnp.exp(-jnp.abs(nl)))
    rx = jnp.dot(xc.astype(BF16), wax, preferred_element_type=F32)
    r = _sigmoid(rx[:, :nd] + ba)
    ig = _sigmoid(rx[:, nd:] + bx)
    log_a = r * ((-LRU_C) * sp)
    a = jnp.exp(log_a)
    u = jnp.sqrt(-jnp.tanh(log_a) * (a * a + 1.0)) * (ig * xc)

    ns = rr // SUBLANES
    a3 = a.reshape(ns, SUBLANES, LANES)
    b3 = u.reshape(ns, SUBLANES, LANES)
    row = lax.broadcasted_iota(jnp.int32, a3.shape, 1)
    d = 1
    while d < SUBLANES:
        m = row >= d
        b3 = jnp.where(m, b3 + a3 * pltpu.roll(b3, d, axis=1), b3)
        a3 = jnp.where(m, a3 * pltpu.roll(a3, d, axis=1), a3)
        d *= 2
    carry = hc_ref[unit]
    hs = []
    for s in range(ns):
        h = b3[s] + a3[s] * carry
        hs.append(h)
        carry = jnp.broadcast_to(h[SUBLANES - 1:SUBLANES, :], h.shape)
    hc_ref[unit] = carry
    h = jnp.concatenate(hs, axis=0)
    return h * _gelu_tanh(zgate)


def _mixer_kernel(zq_ref, zf_ref, zv_ref, zg_ref, zx_ref, zgate_ref, lbr_ref, gn_ref,
                  cw_ref, cb_ref, wax_ref, ba_ref, bx_ref, lam_ref,
                  o_ref, st_ref, tail_ref, hc_ref, *, tiles_per_seq):
    unit = pl.program_id(1)

    @pl.when(pl.program_id(0) % tiles_per_seq == 0)
    def _():
        st_ref[unit] = jnp.zeros(st_ref.shape[1:], F32)
        tail_ref[unit] = jnp.zeros(tail_ref.shape[1:], F32)
        hc_ref[unit] = jnp.zeros(hc_ref.shape[1:], F32)

    lbr = lbr_ref[...]
    ex = jnp.exp(lbr - jnp.max(lbr, axis=0, keepdims=True))
    lb = ex[0:1, :] / jnp.sum(ex, axis=0, keepdims=True)
    oh = _hgrn_unit(zq_ref[...], zf_ref[...], zv_ref[...], zg_ref[...], lb, gn_ref[...],
                    st_ref, unit)
    o_ref[:, :LANES] = oh.astype(o_ref.dtype)
    ol = _lru_unit(zx_ref[...], zgate_ref[...], cw_ref[...], cb_ref[...], wax_ref[0],
                   ba_ref[...], bx_ref[...], lam_ref[...], tail_ref, hc_ref, unit)
    o_ref[:, LANES:] = ol.astype(o_ref.dtype)


def _mixer(z, lower_bounds, g_norm, conv_w, conv_b, wax, b_a, b_x, lam, seq):
    t = z.shape[0]
    ts = MIX_TS
    nu = HGRN_HEADS
    assert nu == LRU_BLOCKS and HEAD_DIM == LRU_BLOCK_DIM == LANES and seq % ts == 0
    est = 6 * 2 * ts * LANES * 4 + 2 * ts * 2 * LANES * 2 + 4 * (1 << 20)
    est += t * (HGRN_WIDTH + LRU_WIDTH) * 2

    def zspec(part):
        return pl.BlockSpec((ts, LANES), lambda i, j, part=part: (i, part * nu + j))

    def vec(rows):
        return pl.BlockSpec((rows, LANES), lambda i, j: (0, j))

    return pl.pallas_call(
        functools.partial(_mixer_kernel, tiles_per_seq=seq // ts),
        grid=(t // ts, nu),
        in_specs=[zspec(0), zspec(1), zspec(2), zspec(3), zspec(4), zspec(5),
                  vec(lower_bounds.shape[0]),
                  pl.BlockSpec((1, HEAD_DIM), lambda i, j: (0, 0)),
                  vec(CONV_WIDTH), vec(1),
                  pl.BlockSpec((1,) + wax.shape[1:], lambda i, j: (j, 0, 0)),
                  vec(1), vec(1), vec(1)],
        out_specs=pl.BlockSpec((ts, 2 * LANES), lambda i, j: (i, j)),
        out_shape=jax.ShapeDtypeStruct((t, HGRN_WIDTH + LRU_WIDTH), BF16),
        scratch_shapes=[pltpu.VMEM((nu, HEAD_DIM, HEAD_DIM), F32),
                        pltpu.VMEM((nu, SUBLANES, LANES), F32),
                        pltpu.VMEM((nu, SUBLANES, LANES), F32)],
        compiler_params=pltpu.CompilerParams(
            dimension_semantics=("arbitrary", "arbitrary"),
            vmem_limit_bytes=_vmem_limit(est)),
    )(z, z, z, z, z, z, lower_bounds, g_norm, conv_w, conv_b, wax, b_a, b_x, lam)


def _resident(shape):
    return pl.BlockSpec(shape, lambda i: (0,) * len(shape), pipeline_mode=pl.Buffered(1))


def _outproj_kernel(x_ref, o_ref_in, w_ref, out_ref, wb_ref):
    @pl.when(pl.program_id(0) == 0)
    def _():
        for u in range(HGRN_HEADS):
            wb_ref[pl.ds(2 * u * LANES, LANES), :] = (
                w_ref[pl.ds(u * LANES, LANES), :].astype(BF16))
            wb_ref[pl.ds((2 * u + 1) * LANES, LANES), :] = (
                w_ref[pl.ds(HGRN_WIDTH + u * LANES, LANES), :].astype(BF16))

    out_ref[...] = x_ref[...] + jnp.dot(o_ref_in[...], wb_ref[...],
                                        preferred_element_type=F32)


def _out_proj(x, o, w_out):
    t, d = x.shape
    tm = ROW_TM
    k = o.shape[1]
    est = 2 * 2 * tm * d * 4 + 2 * tm * k * 2 + k * d * (4 + 2)
    return pl.pallas_call(
        _outproj_kernel,
        grid=(t // tm,),
        in_specs=[
            pl.BlockSpec((tm, d), lambda i: (i, 0)),
            pl.BlockSpec((tm, k), lambda i: (i, 0)),
            _resident(w_out.shape),
        ],
        out_specs=pl.BlockSpec((tm, d), lambda i: (i, 0)),
        out_shape=jax.ShapeDtypeStruct((t, d), F32),
        scratch_shapes=[pltpu.VMEM(w_out.shape, BF16)],
        compiler_params=pltpu.CompilerParams(
            dimension_semantics=("arbitrary",),
            vmem_limit_bytes=_vmem_limit(est)),
    )(x, o, w_out)


def _ple_kernel(x_ref, p_ref, gp_ref, wg_ref, bg_ref, wp_ref, gf_ref, o_ref,
                wgb_ref, wpb_ref):
    @pl.when(pl.program_id(0) == 0)
    def _():
        wgb_ref[...] = wg_ref[...].astype(BF16)
        wpb_ref[...] = wp_ref[...].astype(BF16)

    x = x_ref[...]
    h = _rmsnorm(x, gp_ref[...]).astype(BF16)
    gate = jax.nn.sigmoid(
        jnp.dot(h, wgb_ref[...], preferred_element_type=F32) + bg_ref[...])
    emb = jnp.dot(p_ref[...].astype(BF16), wpb_ref[...], preferred_element_type=F32)
    o_ref[...] = _rmsnorm(x + gate * emb, gf_ref[...])


def _ple(x, p, g_ple, w_gate, b_gate, w_proj, g_final):
    t, d = x.shape
    pd = p.shape[1]
    tm = ROW_TM
    est = 2 * 2 * tm * d * 4 + 2 * tm * pd * 4 + (d + pd) * d * (4 + 2) + 2 * tm * d * 4
    row = pl.BlockSpec((1, d), lambda i: (0, 0))
    return pl.pallas_call(
        _ple_kernel,
        grid=(t // tm,),
        in_specs=[
            pl.BlockSpec((tm, d), lambda i: (i, 0)),
            pl.BlockSpec((tm, pd), lambda i: (i, 0)),
            row,
            _resident(w_gate.shape),
            row,
            _resident(w_proj.shape),
            row,
        ],
        out_specs=pl.BlockSpec((tm, d), lambda i: (i, 0)),
        out_shape=jax.ShapeDtypeStruct((t, d), F32),
        scratch_shapes=[pltpu.VMEM(w_gate.shape, BF16), pltpu.VMEM(w_proj.shape, BF16)],
        compiler_params=pltpu.CompilerParams(
            dimension_semantics=("arbitrary",),
            vmem_limit_bytes=_vmem_limit(est)),
    )(x, p, g_ple, w_gate, b_gate, w_proj, g_final)


def kernel(x, p, ffn1_norm, ffn1_w_gate, ffn1_w_up, ffn1_w_down, mix_norm, w_in,
           hgrn_lower_bounds, hgrn_g_norm, conv_w, conv_b, lru_w_a, lru_b_a, lru_w_x,
           lru_b_x, lru_lambda, w_out, ffn2_norm, ffn2_w_gate, ffn2_w_up, ffn2_w_down,
           ple_norm, ple_w_gate, ple_b_gate, ple_w_proj, final_norm):
    batch, seq, d = x.shape
    t = batch * seq
    depth = ffn1_norm.shape[0]
    assert depth == 1, "the shared lower-bound cumsum is specialised to one layer"
    l = 0
    bf = lambda a: a.astype(BF16)
    xt = x.reshape(t, d)

    xt = _ffn(xt, ffn1_norm[l][None], ffn1_w_gate[l], ffn1_w_up[l], ffn1_w_down[l])

    z = _in_proj(xt, mix_norm[l][None], w_in[l])
    wax = bf(jnp.concatenate([lru_w_a[l], lru_w_x[l]], axis=-1))
    mixed = _mixer(z, hgrn_lower_bounds, hgrn_g_norm[l][None], conv_w[l], conv_b[l][None], wax,
                   lru_b_a[l][None], lru_b_x[l][None], lru_lambda[l][None], seq)
    xt = _out_proj(xt, mixed, w_out[l])

    xt = _ffn(xt, ffn2_norm[l][None], ffn2_w_gate[l], ffn2_w_up[l], ffn2_w_down[l])

    out = _ple(xt, p[l].reshape(t, -1), ple_norm[l][None], ple_w_gate[l],
               ple_b_gate[l][None], ple_w_proj[l], final_norm[None])
    return out.reshape(batch, seq, d)
```

```python
import functools

import jax
import jax.numpy as jnp
from jax import lax
from jax.experimental import pallas as pl
from jax.experimental.pallas import tpu as pltpu

F32 = jnp.float32
BF16 = jnp.bfloat16

D_MODEL = 2048
D_FF = 5632
PLE_DIM = 256
HGRN_WIDTH = D_MODEL // 2
HEAD_DIM = 128
HGRN_HEADS = HGRN_WIDTH // HEAD_DIM
LRU_WIDTH = D_MODEL - HGRN_WIDTH
LRU_BLOCKS = 8
LRU_BLOCK_DIM = LRU_WIDTH // LRU_BLOCKS
CONV_WIDTH = 4
LRU_C = 8.0
EPS = 1e-6
LOG2_E = 1.4426950408889634
SQRT_2_OVER_PI = 0.7978845608028654

V7X_VMEM_BYTES = 64 * 1024 * 1024
SUBLANES = 8
LANES = 128

FFN_TM = 1024
FFN_TF = 256
PROJ_TM = 2048
PROJ_TN = 256
ROW_TM = 512
HGRN_CHUNK = 64
HGRN_SUB = 16
MIX_TS = 1024


def _vmem_limit(nbytes):
    return int(min(V7X_VMEM_BYTES - (4 << 20), nbytes + nbytes // 4 + (2 << 20)))


def _sigmoid(x):
    return 1.0 / (1.0 + jnp.exp2(x * (-LOG2_E)))


def _gelu_tanh(x):
    return x * _sigmoid((2.0 * SQRT_2_OVER_PI) * (x * (1.0 + 0.044715 * (x * x))))


def _rmsnorm(xf, g):
    ms = jnp.mean(xf * xf, axis=-1, keepdims=True)
    return xf * lax.rsqrt(ms + EPS) * g


def _ffn_kernel(x_ref, g_ref, wg_ref, wu_ref, wd_ref, o_ref, h_ref):
    j = pl.program_id(1)

    def step(first):
        if first:
            h_ref[...] = _rmsnorm(x_ref[...], g_ref[...]).astype(BF16)
        h = h_ref[...]
        gate = jnp.dot(h, wg_ref[...].astype(BF16), preferred_element_type=F32)
        up = jnp.dot(h, wu_ref[...].astype(BF16), preferred_element_type=F32)
        act = (0.5 * (gate * jax.nn.sigmoid(gate) * up)).astype(BF16)
        part = jnp.dot(act, wd_ref[...].astype(BF16), preferred_element_type=F32)
        if first:
            o_ref[...] = x_ref[...] + part
        else:
            o_ref[...] += part

    pl.when(j == 0)(functools.partial(step, True))
    pl.when(j > 0)(functools.partial(step, False))


def _ffn(x, g, wg, wu, wd):
    t, d = x.shape
    dff = wg.shape[1]
    tm, tf = FFN_TM, FFN_TF
    wbytes = wg.dtype.itemsize
    est = (2 * tm * d * 4) * 2 + tm * d * 2 + 3 * d * tf * (2 * wbytes + 2) + 3 * tm * tf * 4
    return pl.pallas_call(
        _ffn_kernel,
        grid=(t // tm, dff // tf),
        in_specs=[
            pl.BlockSpec((tm, d), lambda i, j: (i, 0)),
            pl.BlockSpec((1, d), lambda i, j: (0, 0)),
            pl.BlockSpec((d, tf), lambda i, j: (0, j)),
            pl.BlockSpec((d, tf), lambda i, j: (0, j)),
            pl.BlockSpec((tf, d), lambda i, j: (j, 0)),
        ],
        out_specs=pl.BlockSpec((tm, d), lambda i, j: (i, 0)),
        out_shape=jax.ShapeDtypeStruct((t, d), F32),
        scratch_shapes=[pltpu.VMEM((tm, d), BF16)],
        compiler_params=pltpu.CompilerParams(
            dimension_semantics=("arbitrary", "arbitrary"),
            vmem_limit_bytes=_vmem_limit(est)),
    )(x, g, wg, wu, wd)


def _proj_kernel(x_ref, g_ref, w_ref, o_ref, h_ref):
    @pl.when(pl.program_id(1) == 0)
    def _():
        h_ref[...] = _rmsnorm(x_ref[...], g_ref[...]).astype(BF16)

    o_ref[...] = jnp.dot(h_ref[...], w_ref[...].astype(BF16), preferred_element_type=F32)


def _in_proj(x, g, w):
    t, d = x.shape
    n = w.shape[1]
    tm, tn = PROJ_TM, PROJ_TN
    est = 2 * tm * d * 4 + tm * d * 2 + d * tn * (2 * w.dtype.itemsize + 2) + 3 * tm * tn * 4
    return pl.pallas_call(
        _proj_kernel,
        grid=(t // tm, n // tn),
        in_specs=[
            pl.BlockSpec((tm, d), lambda i, j: (i, 0)),
            pl.BlockSpec((1, d), lambda i, j: (0, 0)),
            pl.BlockSpec((d, tn), lambda i, j: (0, j)),
        ],
        out_specs=pl.BlockSpec((tm, tn), lambda i, j: (i, j)),
        out_shape=jax.ShapeDtypeStruct((t, n), F32),
        scratch_shapes=[pltpu.VMEM((tm, d), BF16)],
        compiler_params=pltpu.CompilerParams(
            dimension_semantics=("arbitrary", "arbitrary"),
            vmem_limit_bytes=_vmem_limit(est)),
    )(x, g, w)


def _prefix_scan8(x):
    r = lax.broadcasted_iota(jnp.int32, x.shape, x.ndim - 2)
    d = 1
    while d < SUBLANES:
        x = x + jnp.where(r >= d, pltpu.roll(x, d, axis=x.ndim - 2), 0.0)
        d *= 2
    return x


def _hgrn_unit(zq, zf, zv, zg, lb, gn, st_ref, unit):
    c, sub = HGRN_CHUNK, HGRN_SUB
    rr = zq.shape[0]
    nc = rr // c
    nv = c // SUBLANES
    assert c == 4 * sub and sub == 2 * SUBLANES
    shp4 = (nc, nv, SUBLANES, LANES)
    to4 = lambda x: x.reshape(shp4)
    f = lb + (1.0 - lb) * _sigmoid(zf)
    lf = to4(jnp.log2(f))
    kk = to4(1.0 - f)
    q = to4(zq)

    p8 = _prefix_scan8(lf)
    t8 = jnp.broadcast_to(p8[:, :, SUBLANES - 1:SUBLANES, :], shp4)
    s8 = t8 - p8
    tv = [t8[:, v] for v in range(nv)]
    pv = [p8[:, v] for v in range(nv)]
    sv = [s8[:, v] for v in range(nv)]
    p16 = [pv[v] if v % 2 == 0 else pv[v] + tv[v - 1] for v in range(nv)]
    s16 = [sv[v] + tv[v + 1] if v % 2 == 0 else sv[v] for v in range(nv)]
    t16 = [tv[2 * i] + tv[2 * i + 1] for i in range(nv // 2)]
    p32 = [p16[v] if (v // 2) % 2 == 0 else p16[v] + t16[v // 2 - 1] for v in range(nv)]
    s32 = [s16[v] + t16[v // 2 + 1] if (v // 2) % 2 == 0 else s16[v] for v in range(nv)]
    t32 = [t16[0] + t16[1], t16[2] + t16[3]]
    p64 = [p32[v] if v < nv // 2 else p32[v] + t32[0] for v in range(nv)]
    s64 = [s32[v] + t32[1] if v < nv // 2 else s32[v] for v in range(nv)]
    t64 = t32[0] + t32[1]

    st4 = lambda parts: jnp.stack(parts, axis=1)
    x32 = st4([s32[v] if v < nv // 2 else p32[v] for v in range(nv)])
    x16 = st4([s16[v] if (v // 2) % 2 == 0 else p16[v] for v in range(nv)])
    xd = st4([-sv[v] if v % 2 == 0 else pv[v] for v in range(nv)])
    qk32 = st4([kk[:, v] if v < nv // 2 else q[:, v] for v in range(nv)])
    qk16 = st4([kk[:, v] if (v // 2) % 2 == 0 else q[:, v] for v in range(nv)])

    to3 = lambda x: x.reshape(nc, c, LANES)
    m32 = to3(qk32 * jnp.exp2(x32)).astype(BF16)
    m16 = to3(qk16 * jnp.exp2(x16)).astype(BF16)
    qed = to3(q * jnp.exp2(xd)).astype(BF16)
    ked = to3(kk * jnp.exp2(-xd)).astype(BF16)
    qhat = to3(q * jnp.exp2(st4(p64))).astype(BF16)
    khat = to3(kk * jnp.exp2(st4(s64))).astype(BF16)
    dec = jnp.exp2(t64)
    vb = zv.reshape(nc, c, LANES).astype(BF16)

    zb = jnp.zeros((sub, LANES), BF16)
    blk = lambda x, i: x[i * sub:(i + 1) * sub]
    col = lambda parts: jnp.concatenate(parts, axis=0)
    ti = lax.broadcasted_iota(jnp.int32, (c, c), 0)
    si = lax.broadcasted_iota(jnp.int32, (c, c), 1)
    diag = ((ti // sub) == (si // sub)) & (si <= ti)
    nt = (((1,), (1,)), ((), ()))
    tn = (((0,), (0,)), ((), ()))

    scores = []
    for ci in range(nc):
        a32, a16 = m32[ci], m16[ci]
        q_off = jnp.concatenate([
            col([zb, zb, blk(a32, 2), blk(a32, 3)]),
            col([zb, blk(a16, 1), zb, zb]),
            col([zb, zb, zb, blk(a16, 3)])], axis=1)
        k_off = jnp.concatenate([
            col([blk(a32, 0), blk(a32, 1), zb, zb]),
            col([blk(a16, 0), zb, zb, zb]),
            col([zb, zb, blk(a16, 2), zb])], axis=1)
        s_off = lax.dot_general(q_off, k_off, nt, preferred_element_type=F32)
        s_dia = lax.dot_general(qed[ci], ked[ci], nt, preferred_element_type=F32)
        scores.append((s_off + jnp.where(diag, s_dia, 0.0)).astype(BF16))

    upd = [lax.dot_general(vb[ci], khat[ci], tn, preferred_element_type=F32)
           for ci in range(nc)]
    st = st_ref[unit]
    states = []
    for ci in range(nc):
        states.append(st.astype(BF16))
        st = dec[ci, 0:1, :] * st + upd[ci]
    st_ref[unit] = st

    outs = []
    for ci in range(nc):
        outs.append(lax.dot_general(qhat[ci], states[ci], nt, preferred_element_type=F32)
                    + jnp.dot(scores[ci], vb[ci], preferred_element_type=F32))
    o = jnp.concatenate(outs, axis=0)
    ms = jnp.mean(o * o, axis=-1, keepdims=True)
    return o * lax.rsqrt(ms + EPS) * gn * (zg * _sigmoid(zg))


def _lru_unit(zx, zgate, cw, cb, wax, ba, bx, lam, tail_ref, hc_ref, unit, slot):
    rr = zx.shape[0]
    nd = LRU_BLOCK_DIM
    tail = tail_ref[1 - slot, unit]
    xx = jnp.concatenate([tail, zx], axis=0)
    xc = cb + cw[CONV_WIDTH - 1:CONV_WIDTH, :] * zx
    for j in range(CONV_WIDTH - 1):
        off = SUBLANES - (CONV_WIDTH - 1) + j
        xc = xc + cw[j:j + 1, :] * xx[off:off + rr, :]
    tail_ref[slot, unit] = zx[rr - SUBLANES:rr, :]

    nl = -lam
    sp = jnp.maximum(nl, 0.0) + jnp.log1p(jnp.exp(-jnp.abs(nl)))
    rx = jnp.dot(xc.astype(BF16), wax, preferred_element_type=F32)
    r = _sigmoid(rx[:, :nd] + ba)
    ig = _sigmoid(rx[:, nd:] + bx)
    log_a = r * ((-LRU_C) * sp)
    a = jnp.exp(log_a)
    u = jnp.sqrt(-jnp.tanh(log_a) * (a * a + 1.0)) * (ig * xc)

    ns = rr // SUBLANES
    a3 = a.reshape(ns, SUBLANES, LANES)
    b3 = u.reshape(ns, SUBLANES, LANES)
    row = lax.broadcasted_iota(jnp.int32, a3.shape, 1)
    d = 1
    while d < SUBLANES:
        m = row >= d
        b3 = jnp.where(m, b3 + a3 * pltpu.roll(b3, d, axis=1), b3)
        a3 = jnp.where(m, a3 * pltpu.roll(a3, d, axis=1), a3)
        d *= 2
    carry = hc_ref[unit]
    hs = []
    for s in range(ns):
        h = b3[s] + a3[s] * carry
        hs.append(h)
        carry = jnp.broadcast_to(h[SUBLANES - 1:SUBLANES, :], h.shape)
    hc_ref[unit] = carry
    h = jnp.concatenate(hs, axis=0)
    return h * _gelu_tanh(zgate)


def _mixer_kernel(zq_ref, zf_ref, zv_ref, zg_ref, zx_ref, zgate_ref, lbr_ref, gn_ref,
                  cw_ref, cb_ref, wax_ref, ba_ref, bx_ref, lam_ref,
                  o_ref, st_ref, tail_ref, hc_ref, *, tiles_per_seq):
    unit = pl.program_id(1)
    slot = pl.program_id(0) % 2

    @pl.when(pl.program_id(0) % tiles_per_seq == 0)
    def _():
        st_ref[unit] = jnp.zeros(st_ref.shape[1:], F32)
        tail_ref[1 - slot, unit] = jnp.zeros(tail_ref.shape[2:], F32)
        hc_ref[unit] = jnp.zeros(hc_ref.shape[1:], F32)

    lbr = lbr_ref[...]
    ex = jnp.exp(lbr - jnp.max(lbr, axis=0, keepdims=True))
    lb = ex[0:1, :] / jnp.sum(ex, axis=0, keepdims=True)
    oh = _hgrn_unit(zq_ref[...], zf_ref[...], zv_ref[...], zg_ref[...], lb, gn_ref[...],
                    st_ref, unit)
    o_ref[:, :LANES] = oh.astype(o_ref.dtype)
    ol = _lru_unit(zx_ref[...], zgate_ref[...], cw_ref[...], cb_ref[...], wax_ref[0],
                   ba_ref[...], bx_ref[...], lam_ref[...], tail_ref, hc_ref, unit, slot)
    o_ref[:, LANES:] = ol.astype(o_ref.dtype)


def _mixer(z, lower_bounds, g_norm, conv_w, conv_b, wax, b_a, b_x, lam, seq):
    t = z.shape[0]
    ts = MIX_TS
    nu = HGRN_HEADS
    assert nu == LRU_BLOCKS and HEAD_DIM == LRU_BLOCK_DIM == LANES and seq % ts == 0
    est = 6 * 2 * ts * LANES * 4 + 2 * ts * 2 * LANES * 2 + 4 * (1 << 20)
    est += t * (HGRN_WIDTH + LRU_WIDTH) * 2

    def zspec(part):
        return pl.BlockSpec((ts, LANES), lambda i, j, part=part: (i, part * nu + j))

    def vec(rows):
        return pl.BlockSpec((rows, LANES), lambda i, j: (0, j))

    return pl.pallas_call(
        functools.partial(_mixer_kernel, tiles_per_seq=seq // ts),
        grid=(t // ts, nu),
        in_specs=[zspec(0), zspec(1), zspec(2), zspec(3), zspec(4), zspec(5),
                  vec(lower_bounds.shape[0]),
                  pl.BlockSpec((1, HEAD_DIM), lambda i, j: (0, 0)),
                  vec(CONV_WIDTH), vec(1),
                  pl.BlockSpec((1,) + wax.shape[1:], lambda i, j: (j, 0, 0)),
                  vec(1), vec(1), vec(1)],
        out_specs=pl.BlockSpec((ts, 2 * LANES), lambda i, j: (i, j)),
        out_shape=jax.ShapeDtypeStruct((t, HGRN_WIDTH + LRU_WIDTH), BF16),
        scratch_shapes=[pltpu.VMEM((nu, HEAD_DIM, HEAD_DIM), F32),
                        pltpu.VMEM((2, nu, SUBLANES, LANES), F32),
                        pltpu.VMEM((nu, SUBLANES, LANES), F32)],
        compiler_params=pltpu.CompilerParams(
            dimension_semantics=("arbitrary", "arbitrary"),
            vmem_limit_bytes=_vmem_limit(est)),
    )(z, z, z, z, z, z, lower_bounds, g_norm, conv_w, conv_b, wax, b_a, b_x, lam)


def _resident(shape):
    return pl.BlockSpec(shape, lambda i: (0,) * len(shape), pipeline_mode=pl.Buffered(1))


def _outproj_kernel(x_ref, o_ref_in, w_ref, out_ref, wb_ref):
    @pl.when(pl.program_id(0) == 0)
    def _():
        for u in range(HGRN_HEADS):
            wb_ref[pl.ds(2 * u * LANES, LANES), :] = (
                w_ref[pl.ds(u * LANES, LANES), :].astype(BF16))
            wb_ref[pl.ds((2 * u + 1) * LANES, LANES), :] = (
                w_ref[pl.ds(HGRN_WIDTH + u * LANES, LANES), :].astype(BF16))

    out_ref[...] = x_ref[...] + jnp.dot(o_ref_in[...], wb_ref[...],
                                        preferred_element_type=F32)


def _out_proj(x, o, w_out):
    t, d = x.shape
    tm = ROW_TM
    k = o.shape[1]
    est = 2 * 2 * tm * d * 4 + 2 * tm * k * 2 + k * d * (4 + 2)
    return pl.pallas_call(
        _outproj_kernel,
        grid=(t // tm,),
        in_specs=[
            pl.BlockSpec((tm, d), lambda i: (i, 0)),
            pl.BlockSpec((tm, k), lambda i: (i, 0)),
            _resident(w_out.shape),
        ],
        out_specs=pl.BlockSpec((tm, d), lambda i: (i, 0)),
        out_shape=jax.ShapeDtypeStruct((t, d), F32),
        scratch_shapes=[pltpu.VMEM(w_out.shape, BF16)],
        compiler_params=pltpu.CompilerParams(
            dimension_semantics=("arbitrary",),
            vmem_limit_bytes=_vmem_limit(est)),
    )(x, o, w_out)


def _ple_kernel(x_ref, p_ref, gp_ref, wg_ref, bg_ref, wp_ref, gf_ref, o_ref,
                wgb_ref, wpb_ref):
    @pl.when(pl.program_id(0) == 0)
    def _():
        wgb_ref[...] = wg_ref[...].astype(BF16)
        wpb_ref[...] = wp_ref[...].astype(BF16)

    x = x_ref[...]
    h = _rmsnorm(x, gp_ref[...]).astype(BF16)
    gate = jax.nn.sigmoid(
        jnp.dot(h, wgb_ref[...], preferred_element_type=F32) + bg_ref[...])
    emb = jnp.dot(p_ref[...].astype(BF16), wpb_ref[...], preferred_element_type=F32)
    o_ref[...] = _rmsnorm(x + gate * emb, gf_ref[...])


def _ple(x, p, g_ple, w_gate, b_gate, w_proj, g_final):
    t, d = x.shape
    pd = p.shape[1]
    tm = ROW_TM
    est = 2 * 2 * tm * d * 4 + 2 * tm * pd * 4 + (d + pd) * d * (4 + 2) + 2 * tm * d * 4
    row = pl.BlockSpec((1, d), lambda i: (0, 0))
    return pl.pallas_call(
        _ple_kernel,
        grid=(t // tm,),
        in_specs=[
            pl.BlockSpec((tm, d), lambda i: (i, 0)),
            pl.BlockSpec((tm, pd), lambda i: (i, 0)),
            row,
            _resident(w_gate.shape),
            row,
            _resident(w_proj.shape),
            row,
        ],
        out_specs=pl.BlockSpec((tm, d), lambda i: (i, 0)),
        out_shape=jax.ShapeDtypeStruct((t, d), F32),
        scratch_shapes=[pltpu.VMEM(w_gate.shape, BF16), pltpu.VMEM(w_proj.shape, BF16)],
        compiler_params=pltpu.CompilerParams(
            dimension_semantics=("arbitrary",),
            vmem_limit_bytes=_vmem_limit(est)),
    )(x, p, g_ple, w_gate, b_gate, w_proj, g_final)


def kernel(x, p, ffn1_norm, ffn1_w_gate, ffn1_w_up, ffn1_w_down, mix_norm, w_in,
           hgrn_lower_bounds, hgrn_g_norm, conv_w, conv_b, lru_w_a, lru_b_a, lru_w_x,
           lru_b_x, lru_lambda, w_out, ffn2_norm, ffn2_w_gate, ffn2_w_up, ffn2_w_down,
           ple_norm, ple_w_gate, ple_b_gate, ple_w_proj, final_norm):
    batch, seq, d = x.shape
    t = batch * seq
    depth = ffn1_norm.shape[0]
    assert depth == 1, "the shared lower-bound cumsum is specialised to one layer"
    l = 0
    bf = lambda a: a.astype(BF16)
    xt = x.reshape(t, d)

    xt = _ffn(xt, ffn1_norm[l][None], ffn1_w_gate[l], ffn1_w_up[l], ffn1_w_down[l])

    z = _in_proj(xt, mix_norm[l][None], w_in[l])
    wax = bf(jnp.concatenate([lru_w_a[l], lru_w_x[l]], axis=-1))
    mixed = _mixer(z, hgrn_lower_bounds, hgrn_g_norm[l][None], conv_w[l], conv_b[l][None], wax,
                   lru_b_a[l][None], lru_b_x[l][None], lru_lambda[l][None], seq)
    xt = _out_proj(xt, mixed, w_out[l])

    xt = _ffn(xt, ffn2_norm[l][None], ffn2_w_gate[l], ffn2_w_up[l], ffn2_w_down[l])

    out = _ple(xt, p[l].reshape(t, -1), ple_norm[l][None], ple_w_gate[l],
               ple_b_gate[l][None], ple_w_proj[l], final_norm[None])
    return out.reshape(batch, seq, d)
```

```python
import functools

import jax
import jax.numpy as jnp
from jax import lax
from jax.experimental import pallas as pl
from jax.experimental.pallas import tpu as pltpu

F32 = jnp.float32
BF16 = jnp.bfloat16

D_MODEL = 2048
D_FF = 5632
PLE_DIM = 256
HGRN_WIDTH = D_MODEL // 2
HEAD_DIM = 128
HGRN_HEADS = HGRN_WIDTH // HEAD_DIM
LRU_WIDTH = D_MODEL - HGRN_WIDTH
LRU_BLOCKS = 8
LRU_BLOCK_DIM = LRU_WIDTH // LRU_BLOCKS
CONV_WIDTH = 4
LRU_C = 8.0
EPS = 1e-6
LOG2_E = 1.4426950408889634
SQRT_2_OVER_PI = 0.7978845608028654

V7X_VMEM_BYTES = 64 * 1024 * 1024
SUBLANES = 8
LANES = 128

FFN_TM = 1024
FFN_TF = 256
PROJ_TM = 2048
PROJ_TN = 512
ROW_TM = 512
HGRN_CHUNK = 64
HGRN_SUB = 16
MIX_TS = 1024


def _vmem_limit(nbytes):
    return int(min(V7X_VMEM_BYTES - (2 << 20), nbytes + nbytes // 4 + (2 << 20)))


def _sigmoid(x):
    return 1.0 / (1.0 + jnp.exp2(x * (-LOG2_E)))


def _gelu_tanh(x):
    return x * _sigmoid((2.0 * SQRT_2_OVER_PI) * (x * (1.0 + 0.044715 * (x * x))))


def _rmsnorm(xf, g):
    ms = jnp.mean(xf * xf, axis=-1, keepdims=True)
    return xf * lax.rsqrt(ms + EPS) * g


def _ffn_kernel(x_ref, g_ref, wg_ref, wu_ref, wd_ref, o_ref, h_ref):
    j = pl.program_id(1)

    def step(first):
        if first:
            h_ref[...] = _rmsnorm(x_ref[...], g_ref[...]).astype(BF16)
        h = h_ref[...]
        gate = jnp.dot(h, wg_ref[...].astype(BF16), preferred_element_type=F32)
        up = jnp.dot(h, wu_ref[...].astype(BF16), preferred_element_type=F32)
        act = (0.5 * (gate * jax.nn.sigmoid(gate) * up)).astype(BF16)
        part = jnp.dot(act, wd_ref[...].astype(BF16), preferred_element_type=F32)
        if first:
            o_ref[...] = x_ref[...] + part
        else:
            o_ref[...] += part

    pl.when(j == 0)(functools.partial(step, True))
    pl.when(j > 0)(functools.partial(step, False))


def _ffn(x, g, wg, wu, wd):
    t, d = x.shape
    dff = wg.shape[1]
    tm, tf = FFN_TM, FFN_TF
    wbytes = wg.dtype.itemsize
    est = (2 * tm * d * 4) * 2 + tm * d * 2 + 3 * d * tf * (2 * wbytes + 2) + 3 * tm * tf * 4
    return pl.pallas_call(
        _ffn_kernel,
        grid=(t // tm, dff // tf),
        in_specs=[
            pl.BlockSpec((tm, d), lambda i, j: (i, 0)),
            pl.BlockSpec((1, d), lambda i, j: (0, 0)),
            pl.BlockSpec((d, tf), lambda i, j: (0, j)),
            pl.BlockSpec((d, tf), lambda i, j: (0, j)),
            pl.BlockSpec((tf, d), lambda i, j: (j, 0)),
        ],
        out_specs=pl.BlockSpec((tm, d), lambda i, j: (i, 0)),
        out_shape=jax.ShapeDtypeStruct((t, d), F32),
        scratch_shapes=[pltpu.VMEM((tm, d), BF16)],
        compiler_params=pltpu.CompilerParams(
            dimension_semantics=("arbitrary", "arbitrary"),
            vmem_limit_bytes=_vmem_limit(est)),
    )(x, g, wg, wu, wd)


def _proj_kernel(x_ref, g_ref, w_ref, o_ref, h_ref):
    @pl.when(pl.program_id(1) == 0)
    def _():
        h_ref[...] = _rmsnorm(x_ref[...], g_ref[...]).astype(BF16)

    o_ref[...] = jnp.dot(h_ref[...], w_ref[...].astype(BF16), preferred_element_type=F32)


def _in_proj(x, g, w):
    t, d = x.shape
    n = w.shape[1]
    tm, tn = PROJ_TM, PROJ_TN
    est = 2 * tm * d * 4 + tm * d * 2 + d * tn * (2 * w.dtype.itemsize + 2) + 3 * tm * tn * 4
    return pl.pallas_call(
        _proj_kernel,
        grid=(t // tm, n // tn),
        in_specs=[
            pl.BlockSpec((tm, d), lambda i, j: (i, 0)),
            pl.BlockSpec((1, d), lambda i, j: (0, 0)),
            pl.BlockSpec((d, tn), lambda i, j: (0, j)),
        ],
        out_specs=pl.BlockSpec((tm, tn), lambda i, j: (i, j)),
        out_shape=jax.ShapeDtypeStruct((t, n), F32),
        scratch_shapes=[pltpu.VMEM((tm, d), BF16)],
        compiler_params=pltpu.CompilerParams(
            dimension_semantics=("arbitrary", "arbitrary"),
            vmem_limit_bytes=_vmem_limit(est)),
    )(x, g, w)


def _prefix_scan8(x):
    r = lax.broadcasted_iota(jnp.int32, x.shape, x.ndim - 2)
    d = 1
    while d < SUBLANES:
        x = x + jnp.where(r >= d, pltpu.roll(x, d, axis=x.ndim - 2), 0.0)
        d *= 2
    return x


def _hgrn_unit(zq, zf, zv, zg, lb, gn, st_ref, unit):
    c, sub = HGRN_CHUNK, HGRN_SUB
    rr = zq.shape[0]
    nc = rr // c
    nv = c // SUBLANES
    assert c == 4 * sub and sub == 2 * SUBLANES
    shp4 = (nc, nv, SUBLANES, LANES)
    to4 = lambda x: x.reshape(shp4)
    f = lb + (1.0 - lb) * _sigmoid(zf)
    lf = to4(jnp.log2(f))
    kk = to4(1.0 - f)
    q = to4(zq)

    p8 = _prefix_scan8(lf)
    t8 = jnp.broadcast_to(p8[:, :, SUBLANES - 1:SUBLANES, :], shp4)
    s8 = t8 - p8
    tv = [t8[:, v] for v in range(nv)]
    pv = [p8[:, v] for v in range(nv)]
    sv = [s8[:, v] for v in range(nv)]
    p16 = [pv[v] if v % 2 == 0 else pv[v] + tv[v - 1] for v in range(nv)]
    s16 = [sv[v] + tv[v + 1] if v % 2 == 0 else sv[v] for v in range(nv)]
    t16 = [tv[2 * i] + tv[2 * i + 1] for i in range(nv // 2)]
    p32 = [p16[v] if (v // 2) % 2 == 0 else p16[v] + t16[v // 2 - 1] for v in range(nv)]
    s32 = [s16[v] + t16[v // 2 + 1] if (v // 2) % 2 == 0 else s16[v] for v in range(nv)]
    t32 = [t16[0] + t16[1], t16[2] + t16[3]]
    p64 = [p32[v] if v < nv // 2 else p32[v] + t32[0] for v in range(nv)]
    s64 = [s32[v] + t32[1] if v < nv // 2 else s32[v] for v in range(nv)]
    t64 = t32[0] + t32[1]

    st4 = lambda parts: jnp.stack(parts, axis=1)
    x32 = st4([s32[v] if v < nv // 2 else p32[v] for v in range(nv)])
    x16 = st4([s16[v] if (v // 2) % 2 == 0 else p16[v] for v in range(nv)])
    xd = st4([-sv[v] if v % 2 == 0 else pv[v] for v in range(nv)])
    qk32 = st4([kk[:, v] if v < nv // 2 else q[:, v] for v in range(nv)])
    qk16 = st4([kk[:, v] if (v // 2) % 2 == 0 else q[:, v] for v in range(nv)])

    to3 = lambda x: x.reshape(nc, c, LANES)
    m32 = to3(qk32 * jnp.exp2(x32)).astype(BF16)
    m16 = to3(qk16 * jnp.exp2(x16)).astype(BF16)
    qed = to3(q * jnp.exp2(xd)).astype(BF16)
    ked = to3(kk * jnp.exp2(-xd)).astype(BF16)
    qhat = to3(q * jnp.exp2(st4(p64))).astype(BF16)
    khat = to3(kk * jnp.exp2(st4(s64))).astype(BF16)
    dec = jnp.exp2(t64)
    vb = zv.reshape(nc, c, LANES).astype(BF16)

    zb = jnp.zeros((sub, LANES), BF16)
    blk = lambda x, i: x[i * sub:(i + 1) * sub]
    col = lambda parts: jnp.concatenate(parts, axis=0)
    ti = lax.broadcasted_iota(jnp.int32, (c, c), 0)
    si = lax.broadcasted_iota(jnp.int32, (c, c), 1)
    diag = ((ti // sub) == (si // sub)) & (si <= ti)
    nt = (((1,), (1,)), ((), ()))
    tn = (((0,), (0,)), ((), ()))

    scores = []
    for ci in range(nc):
        a32, a16 = m32[ci], m16[ci]
        q_off = jnp.concatenate([
            col([zb, zb, blk(a32, 2), blk(a32, 3)]),
            col([zb, blk(a16, 1), zb, zb]),
            col([zb, zb, zb, blk(a16, 3)])], axis=1)
        k_off = jnp.concatenate([
            col([blk(a32, 0), blk(a32, 1), zb, zb]),
            col([blk(a16, 0), zb, zb, zb]),
            col([zb, zb, blk(a16, 2), zb])], axis=1)
        s_off = lax.dot_general(q_off, k_off, nt, preferred_element_type=F32)
        s_dia = lax.dot_general(qed[ci], ked[ci], nt, preferred_element_type=F32)
        scores.append((s_off + jnp.where(diag, s_dia, 0.0)).astype(BF16))

    upd = [lax.dot_general(vb[ci], khat[ci], tn, preferred_element_type=F32)
           for ci in range(nc)]
    st = st_ref[unit]
    states = []
    for ci in range(nc):
        states.append(st.astype(BF16))
        st = dec[ci, 0:1, :] * st + upd[ci]
    st_ref[unit] = st

    outs = []
    for ci in range(nc):
        outs.append(lax.dot_general(qhat[ci], states[ci], nt, preferred_element_type=F32)
                    + jnp.dot(scores[ci], vb[ci], preferred_element_type=F32))
    o = jnp.concatenate(outs, axis=0)
    ms = jnp.mean(o * o, axis=-1, keepdims=True)
    return o * lax.rsqrt(ms + EPS) * gn * (zg * _sigmoid(zg))


def _lru_unit(zx, zgate, cw, cb, wax, ba, bx, lam, tail_ref, hc_ref, unit, slot):
    rr = zx.shape[0]
    nd = LRU_BLOCK_DIM
    tail = tail_ref[1 - slot, unit]
    xx = jnp.concatenate([tail, zx], axis=0)
    xc = cb + cw[CONV_WIDTH - 1:CONV_WIDTH, :] * zx
    for j in range(CONV_WIDTH - 1):
        off = SUBLANES - (CONV_WIDTH - 1) + j
        xc = xc + cw[j:j + 1, :] * xx[off:off + rr, :]
    tail_ref[slot, unit] = zx[rr - SUBLANES:rr, :]

    nl = -lam
    sp = jnp.maximum(nl, 0.0) + jnp.log1p(jnp.exp(-jnp.abs(nl)))
    rx = jnp.dot(xc.astype(BF16), wax, preferred_element_type=F32)
    r = _sigmoid(rx[:, :nd] + ba)
    ig = _sigmoid(rx[:, nd:] + bx)
    log_a = r * ((-LRU_C) * sp)
    a = jnp.exp(log_a)
    u = jnp.sqrt(-jnp.tanh(log_a) * (a * a + 1.0)) * (ig * xc)

    ns = rr // SUBLANES
    a3 = a.reshape(ns, SUBLANES, LANES)
    b3 = u.reshape(ns, SUBLANES, LANES)
    row = lax.broadcasted_iota(jnp.int32, a3.shape, 1)
    d = 1
    while d < SUBLANES:
        m = row >= d
        b3 = jnp.where(m, b3 + a3 * pltpu.roll(b3, d, axis=1), b3)
        a3 = jnp.where(m, a3 * pltpu.roll(a3, d, axis=1), a3)
        d *= 2
    carry = hc_ref[unit]
    hs = []
    for s in range(ns):
        h = b3[s] + a3[s] * carry
        hs.append(h)
        carry = jnp.broadcast_to(h[SUBLANES - 1:SUBLANES, :], h.shape)
    hc_ref[unit] = carry
    h = jnp.concatenate(hs, axis=0)
    return h * _gelu_tanh(zgate)


def _mixer_kernel(zq_ref, zf_ref, zv_ref, zg_ref, zx_ref, zgate_ref, lbr_ref, gn_ref,
                  cw_ref, cb_ref, wax_ref, ba_ref, bx_ref, lam_ref,
                  o_ref, st_ref, tail_ref, hc_ref, *, tiles_per_seq):
    unit = pl.program_id(1)
    slot = pl.program_id(0) % 2

    @pl.when(pl.program_id(0) % tiles_per_seq == 0)
    def _():
        st_ref[unit] = jnp.zeros(st_ref.shape[1:], F32)
        tail_ref[1 - slot, unit] = jnp.zeros(tail_ref.shape[2:], F32)
        hc_ref[unit] = jnp.zeros(hc_ref.shape[1:], F32)

    lbr = lbr_ref[...]
    ex = jnp.exp(lbr - jnp.max(lbr, axis=0, keepdims=True))
    lb = ex[0:1, :] / jnp.sum(ex, axis=0, keepdims=True)
    oh = _hgrn_unit(zq_ref[...], zf_ref[...], zv_ref[...], zg_ref[...], lb, gn_ref[...],
                    st_ref, unit)
    o_ref[:, :LANES] = oh.astype(o_ref.dtype)
    ol = _lru_unit(zx_ref[...], zgate_ref[...], cw_ref[...], cb_ref[...], wax_ref[0],
                   ba_ref[...], bx_ref[...], lam_ref[...], tail_ref, hc_ref, unit, slot)
    o_ref[:, LANES:] = ol.astype(o_ref.dtype)


def _mixer(z, lower_bounds, g_norm, conv_w, conv_b, wax, b_a, b_x, lam, seq):
    t = z.shape[0]
    ts = MIX_TS
    nu = HGRN_HEADS
    assert nu == LRU_BLOCKS and HEAD_DIM == LRU_BLOCK_DIM == LANES and seq % ts == 0
    est = 6 * 2 * ts * LANES * 4 + 2 * ts * 2 * LANES * 2 + 4 * (1 << 20)
    est += t * (HGRN_WIDTH + LRU_WIDTH) * 2

    def zspec(part):
        return pl.BlockSpec((ts, LANES), lambda i, j, part=part: (i, part * nu + j))

    def vec(rows):
        return pl.BlockSpec((rows, LANES), lambda i, j: (0, j))

    return pl.pallas_call(
        functools.partial(_mixer_kernel, tiles_per_seq=seq // ts),
        grid=(t // ts, nu),
        in_specs=[zspec(0), zspec(1), zspec(2), zspec(3), zspec(4), zspec(5),
                  vec(lower_bounds.shape[0]),
                  pl.BlockSpec((1, HEAD_DIM), lambda i, j: (0, 0)),
                  vec(CONV_WIDTH), vec(1),
                  pl.BlockSpec((1,) + wax.shape[1:], lambda i, j: (j, 0, 0)),
                  vec(1), vec(1), vec(1)],
        out_specs=pl.BlockSpec((ts, 2 * LANES), lambda i, j: (i, j)),
        out_shape=jax.ShapeDtypeStruct((t, HGRN_WIDTH + LRU_WIDTH), BF16),
        scratch_shapes=[pltpu.VMEM((nu, HEAD_DIM, HEAD_DIM), F32),
                        pltpu.VMEM((2, nu, SUBLANES, LANES), F32),
                        pltpu.VMEM((nu, SUBLANES, LANES), F32)],
        compiler_params=pltpu.CompilerParams(
            dimension_semantics=("arbitrary", "arbitrary"),
            vmem_limit_bytes=_vmem_limit(est)),
    )(z, z, z, z, z, z, lower_bounds, g_norm, conv_w, conv_b, wax, b_a, b_x, lam)


def _resident(shape):
    return pl.BlockSpec(shape, lambda i: (0,) * len(shape), pipeline_mode=pl.Buffered(1))


def _outproj_kernel(x_ref, o_ref_in, w_ref, out_ref, wb_ref):
    @pl.when(pl.program_id(0) == 0)
    def _():
        for u in range(HGRN_HEADS):
            wb_ref[pl.ds(2 * u * LANES, LANES), :] = (
                w_ref[pl.ds(u * LANES, LANES), :].astype(BF16))
            wb_ref[pl.ds((2 * u + 1) * LANES, LANES), :] = (
                w_ref[pl.ds(HGRN_WIDTH + u * LANES, LANES), :].astype(BF16))

    out_ref[...] = x_ref[...] + jnp.dot(o_ref_in[...], wb_ref[...],
                                        preferred_element_type=F32)


def _out_proj(x, o, w_out):
    t, d = x.shape
    tm = ROW_TM
    k = o.shape[1]
    est = 2 * 2 * tm * d * 4 + 2 * tm * k * 2 + k * d * (4 + 2)
    return pl.pallas_call(
        _outproj_kernel,
        grid=(t // tm,),
        in_specs=[
            pl.BlockSpec((tm, d), lambda i: (i, 0)),
            pl.BlockSpec((tm, k), lambda i: (i, 0)),
            _resident(w_out.shape),
        ],
        out_specs=pl.BlockSpec((tm, d), lambda i: (i, 0)),
        out_shape=jax.ShapeDtypeStruct((t, d), F32),
        scratch_shapes=[pltpu.VMEM(w_out.shape, BF16)],
        compiler_params=pltpu.CompilerParams(
            dimension_semantics=("arbitrary",),
            vmem_limit_bytes=_vmem_limit(est)),
    )(x, o, w_out)


def _ple_kernel(x_ref, p_ref, gp_ref, wg_ref, bg_ref, wp_ref, gf_ref, o_ref,
                wgb_ref, wpb_ref):
    @pl.when(pl.program_id(0) == 0)
    def _():
        wgb_ref[...] = wg_ref[...].astype(BF16)
        wpb_ref[...] = wp_ref[...].astype(BF16)

    x = x_ref[...]
    h = _rmsnorm(x, gp_ref[...]).astype(BF16)
    gate = jax.nn.sigmoid(
        jnp.dot(h, wgb_ref[...], preferred_element_type=F32) + bg_ref[...])
    emb = jnp.dot(p_ref[...].astype(BF16), wpb_ref[...], preferred_element_type=F32)
    o_ref[...] = _rmsnorm(x + gate * emb, gf_ref[...])


def _ple(x, p, g_ple, w_gate, b_gate, w_proj, g_final):
    t, d = x.shape
    pd = p.shape[1]
    tm = ROW_TM
    est = 2 * 2 * tm * d * 4 + 2 * tm * pd * 4 + (d + pd) * d * (4 + 2) + 2 * tm * d * 4
    row = pl.BlockSpec((1, d), lambda i: (0, 0))
    return pl.pallas_call(
        _ple_kernel,
        grid=(t // tm,),
        in_specs=[
            pl.BlockSpec((tm, d), lambda i: (i, 0)),
            pl.BlockSpec((tm, pd), lambda i: (i, 0)),
            row,
            _resident(w_gate.shape),
            row,
            _resident(w_proj.shape),
            row,
        ],
        out_specs=pl.BlockSpec((tm, d), lambda i: (i, 0)),
        out_shape=jax.ShapeDtypeStruct((t, d), F32),
        scratch_shapes=[pltpu.VMEM(w_gate.shape, BF16), pltpu.VMEM(w_proj.shape, BF16)],
        compiler_params=pltpu.CompilerParams(
            dimension_semantics=("arbitrary",),
            vmem_limit_bytes=_vmem_limit(est)),
    )(x, p, g_ple, w_gate, b_gate, w_proj, g_final)


def kernel(x, p, ffn1_norm, ffn1_w_gate, ffn1_w_up, ffn1_w_down, mix_norm, w_in,
           hgrn_lower_bounds, hgrn_g_norm, conv_w, conv_b, lru_w_a, lru_b_a, lru_w_x,
           lru_b_x, lru_lambda, w_out, ffn2_norm, ffn2_w_gate, ffn2_w_up, ffn2_w_down,
           ple_norm, ple_w_gate, ple_b_gate, ple_w_proj, final_norm):
    batch, seq, d = x.shape
    t = batch * seq
    depth = ffn1_norm.shape[0]
    assert depth == 1, "the shared lower-bound cumsum is specialised to one layer"
    l = 0
    bf = lambda a: a.astype(BF16)
    xt = x.reshape(t, d)

    xt = _ffn(xt, ffn1_norm[l][None], ffn1_w_gate[l], ffn1_w_up[l], ffn1_w_down[l])

    z = _in_proj(xt, mix_norm[l][None], w_in[l])
    wax = bf(jnp.concatenate([lru_w_a[l], lru_w_x[l]], axis=-1))
    mixed = _mixer(z, hgrn_lower_bounds, hgrn_g_norm[l][None], conv_w[l], conv_b[l][None], wax,
                   lru_b_a[l][None], lru_b_x[l][None], lru_lambda[l][None], seq)
    xt = _out_proj(xt, mixed, w_out[l])

    xt = _ffn(xt, ffn2_norm[l][None], ffn2_w_gate[l], ffn2_w_up[l], ffn2_w_down[l])

    out = _ple(xt, p[l].reshape(t, -1), ple_norm[l][None], ple_w_gate[l],
               ple_b_gate[l][None], ple_w_proj[l], final_norm[None])
    return out.reshape(batch, seq, d)
```

```python
import functools

import jax
import jax.numpy as jnp
from jax import lax
from jax.experimental import pallas as pl
from jax.experimental.pallas import tpu as pltpu

F32 = jnp.float32
BF16 = jnp.bfloat16

D_MODEL = 2048
D_FF = 5632
PLE_DIM = 256
HGRN_WIDTH = D_MODEL // 2
HEAD_DIM = 128
HGRN_HEADS = HGRN_WIDTH // HEAD_DIM
LRU_WIDTH = D_MODEL - HGRN_WIDTH
LRU_BLOCKS = 8
LRU_BLOCK_DIM = LRU_WIDTH // LRU_BLOCKS
CONV_WIDTH = 4
LRU_C = 8.0
EPS = 1e-6
LOG2_E = 1.4426950408889634
SQRT_2_OVER_PI = 0.7978845608028654

V7X_VMEM_BYTES = 64 * 1024 * 1024
SUBLANES = 8
LANES = 128

FFN_TM = 1024
FFN_TF = 256
PROJ_TM = 2048
PROJ_TN = 512
ROW_TM = 512
HGRN_CHUNK = 64
HGRN_SUB = 16
MIX_TS = 2048


def _vmem_limit(nbytes):
    return int(min(V7X_VMEM_BYTES - (2 << 20), nbytes + nbytes // 4 + (2 << 20)))


def _sigmoid(x):
    return 1.0 / (1.0 + jnp.exp2(x * (-LOG2_E)))


def _gelu_tanh(x):
    k0 = -2.0 * SQRT_2_OVER_PI * LOG2_E
    return x / (1.0 + jnp.exp2(x * (k0 + (k0 * 0.044715) * (x * x))))


def _rmsnorm(xf, g):
    ms = jnp.mean(xf * xf, axis=-1, keepdims=True)
    return xf * lax.rsqrt(ms + EPS) * g


def _ffn_kernel(x_ref, g_ref, wg_ref, wu_ref, wd_ref, o_ref, h_ref):
    j = pl.program_id(1)

    def step(first):
        if first:
            h_ref[...] = _rmsnorm(x_ref[...], g_ref[...]).astype(BF16)
        h = h_ref[...]
        gate = jnp.dot(h, wg_ref[...].astype(BF16), preferred_element_type=F32)
        up = jnp.dot(h, wu_ref[...].astype(BF16), preferred_element_type=F32)
        act = (0.5 * (gate * jax.nn.sigmoid(gate) * up)).astype(BF16)
        part = jnp.dot(act, wd_ref[...].astype(BF16), preferred_element_type=F32)
        if first:
            o_ref[...] = x_ref[...] + part
        else:
            o_ref[...] += part

    pl.when(j == 0)(functools.partial(step, True))
    pl.when(j > 0)(functools.partial(step, False))


def _ffn(x, g, wg, wu, wd):
    t, d = x.shape
    dff = wg.shape[1]
    tm, tf = FFN_TM, FFN_TF
    wbytes = wg.dtype.itemsize
    est = (2 * tm * d * 4) * 2 + tm * d * 2 + 3 * d * tf * (2 * wbytes + 2) + 3 * tm * tf * 4
    return pl.pallas_call(
        _ffn_kernel,
        grid=(t // tm, dff // tf),
        in_specs=[
            pl.BlockSpec((tm, d), lambda i, j: (i, 0)),
            pl.BlockSpec((1, d), lambda i, j: (0, 0)),
            pl.BlockSpec((d, tf), lambda i, j: (0, j)),
            pl.BlockSpec((d, tf), lambda i, j: (0, j)),
            pl.BlockSpec((tf, d), lambda i, j: (j, 0)),
        ],
        out_specs=pl.BlockSpec((tm, d), lambda i, j: (i, 0)),
        out_shape=jax.ShapeDtypeStruct((t, d), F32),
        scratch_shapes=[pltpu.VMEM((tm, d), BF16)],
        compiler_params=pltpu.CompilerParams(
            dimension_semantics=("arbitrary", "arbitrary"),
            vmem_limit_bytes=_vmem_limit(est)),
    )(x, g, wg, wu, wd)


def _proj_kernel(x_ref, g_ref, w_ref, o_ref, h_ref):
    @pl.when(pl.program_id(1) == 0)
    def _():
        h_ref[...] = _rmsnorm(x_ref[...], g_ref[...]).astype(BF16)

    o_ref[...] = jnp.dot(h_ref[...], w_ref[...].astype(BF16), preferred_element_type=F32)


def _in_proj(x, g, w):
    t, d = x.shape
    n = w.shape[1]
    tm, tn = PROJ_TM, PROJ_TN
    est = 2 * tm * d * 4 + tm * d * 2 + d * tn * (2 * w.dtype.itemsize + 2) + 3 * tm * tn * 4
    return pl.pallas_call(
        _proj_kernel,
        grid=(t // tm, n // tn),
        in_specs=[
            pl.BlockSpec((tm, d), lambda i, j: (i, 0)),
            pl.BlockSpec((1, d), lambda i, j: (0, 0)),
            pl.BlockSpec((d, tn), lambda i, j: (0, j)),
        ],
        out_specs=pl.BlockSpec((tm, tn), lambda i, j: (i, j)),
        out_shape=jax.ShapeDtypeStruct((t, n), F32),
        scratch_shapes=[pltpu.VMEM((tm, d), BF16)],
        compiler_params=pltpu.CompilerParams(
            dimension_semantics=("arbitrary", "arbitrary"),
            vmem_limit_bytes=_vmem_limit(est)),
    )(x, g, w)


def _hgrn_unit(zq, zf, zv, zg, lb, gn, st_ref, lf_scr, p_scr, t_scr, unit):
    c, sub = HGRN_CHUNK, HGRN_SUB
    rr = zq.shape[0]
    nc = rr // c
    nv = c // SUBLANES
    assert c == 4 * sub and sub == 2 * SUBLANES
    shp4 = (nc, nv, SUBLANES, LANES)
    to4 = lambda x: x.reshape(shp4)
    f = lb + (1.0 - lb) * _sigmoid(zf)
    lf = jnp.log2(f)
    kk = to4(1.0 - f)
    q = to4(zq)

    nvr = rr // SUBLANES
    vreg_row = lambda r: pl.ds(r, nvr, stride=SUBLANES)
    lf_scr[...] = lf
    acc = lf_scr[vreg_row(0), :]
    p_scr[vreg_row(0), :] = acc
    for r in range(1, SUBLANES):
        acc = acc + lf_scr[vreg_row(r), :]
        p_scr[vreg_row(r), :] = acc
    for r in range(SUBLANES):
        t_scr[vreg_row(r), :] = acc
    p8 = to4(p_scr[...])
    t8 = to4(t_scr[...])
    s8 = t8 - p8
    tv = [t8[:, v] for v in range(nv)]
    pv = [p8[:, v] for v in range(nv)]
    sv = [s8[:, v] for v in range(nv)]
    p16 = [pv[v] if v % 2 == 0 else pv[v] + tv[v - 1] for v in range(nv)]
    s16 = [sv[v] + tv[v + 1] if v % 2 == 0 else sv[v] for v in range(nv)]
    t16 = [tv[2 * i] + tv[2 * i + 1] for i in range(nv // 2)]
    p32 = [p16[v] if (v // 2) % 2 == 0 else p16[v] + t16[v // 2 - 1] for v in range(nv)]
    s32 = [s16[v] + t16[v // 2 + 1] if (v // 2) % 2 == 0 else s16[v] for v in range(nv)]
    t32 = [t16[0] + t16[1], t16[2] + t16[3]]
    p64 = [p32[v] if v < nv // 2 else p32[v] + t32[0] for v in range(nv)]
    s64 = [s32[v] + t32[1] if v < nv // 2 else s32[v] for v in range(nv)]
    t64 = t32[0] + t32[1]

    st4 = lambda parts: jnp.stack(parts, axis=1)
    x32 = st4([s32[v] if v < nv // 2 else p32[v] for v in range(nv)])
    x16 = st4([s16[v] if (v // 2) % 2 == 0 else p16[v] for v in range(nv)])
    xd = st4([-sv[v] if v % 2 == 0 else pv[v] for v in range(nv)])
    qk32 = st4([kk[:, v] if v < nv // 2 else q[:, v] for v in range(nv)])
    qk16 = st4([kk[:, v] if (v // 2) % 2 == 0 else q[:, v] for v in range(nv)])

    to3 = lambda x: x.reshape(nc, c, LANES)
    m32 = to3(qk32 * jnp.exp2(x32)).astype(BF16)
    m16 = to3(qk16 * jnp.exp2(x16)).astype(BF16)
    qed = to3(q * jnp.exp2(xd)).astype(BF16)
    ked = to3(kk * jnp.exp2(-xd)).astype(BF16)
    qhat = to3(q * jnp.exp2(st4(p64))).astype(BF16)
    khat = to3(kk * jnp.exp2(st4(s64))).astype(BF16)
    dec = jnp.exp2(t64)
    vb = zv.reshape(nc, c, LANES).astype(BF16)

    zb = jnp.zeros((sub, LANES), BF16)
    blk = lambda x, i: x[i * sub:(i + 1) * sub]
    col = lambda parts: jnp.concatenate(parts, axis=0)
    ti = lax.broadcasted_iota(jnp.int32, (c, c), 0)
    si = lax.broadcasted_iota(jnp.int32, (c, c), 1)
    diag = ((ti // sub) == (si // sub)) & (si <= ti)
    nt = (((1,), (1,)), ((), ()))
    tn = (((0,), (0,)), ((), ()))

    scores = []
    for ci in range(nc):
        a32, a16 = m32[ci], m16[ci]
        q_off = jnp.concatenate([
            col([zb, zb, blk(a32, 2), blk(a32, 3)]),
            col([zb, blk(a16, 1), zb, zb]),
            col([zb, zb, zb, blk(a16, 3)])], axis=1)
        k_off = jnp.concatenate([
            col([blk(a32, 0), blk(a32, 1), zb, zb]),
            col([blk(a16, 0), zb, zb, zb]),
            col([zb, zb, blk(a16, 2), zb])], axis=1)
        s_off = lax.dot_general(q_off, k_off, nt, preferred_element_type=F32)
        s_dia = lax.dot_general(qed[ci], ked[ci], nt, preferred_element_type=F32)
        scores.append((s_off + jnp.where(diag, s_dia, 0.0)).astype(BF16))

    upd = [lax.dot_general(vb[ci], khat[ci], tn, preferred_element_type=F32)
           for ci in range(nc)]
    st = st_ref[unit]
    states = []
    for ci in range(nc):
        states.append(st.astype(BF16))
        st = dec[ci, 0:1, :] * st + upd[ci]
    st_ref[unit] = st

    outs = []
    for ci in range(nc):
        outs.append(lax.dot_general(qhat[ci], states[ci], nt, preferred_element_type=F32)
                    + jnp.dot(scores[ci], vb[ci], preferred_element_type=F32))
    o = jnp.concatenate(outs, axis=0)
    ms = jnp.mean(o * o, axis=-1, keepdims=True)
    return o * lax.rsqrt(ms + EPS) * gn * (zg * _sigmoid(zg))


def _scan_rows(a, b, carry):
    n = a.shape[0] // SUBLANES
    a3 = a.reshape(n, SUBLANES, LANES)
    b3 = b.reshape(n, SUBLANES, LANES)
    row = lax.broadcasted_iota(jnp.int32, a3.shape, 1)
    d = 1
    while d < SUBLANES:
        m = row >= d
        b3 = jnp.where(m, b3 + a3 * pltpu.roll(b3, d, axis=1), b3)
        a3 = jnp.where(m, a3 * pltpu.roll(a3, d, axis=1), a3)
        d *= 2
    es = []
    for s in range(n):
        e = b3[s] + a3[s] * carry
        es.append(e)
        carry = jnp.broadcast_to(e[SUBLANES - 1:SUBLANES, :], e.shape)
    return jnp.concatenate(es, axis=0), carry


def _lru_unit(zx, zgate, cw, cb, wax, ba, bx, lam, tail_ref, hc_ref, a_scr, u_scr, h_scr,
              x_scr, unit, slot):
    rr = zx.shape[0]
    nd = LRU_BLOCK_DIM
    x_scr[0:SUBLANES, :] = tail_ref[1 - slot, unit]
    x_scr[SUBLANES:, :] = zx
    xc = cb + cw[CONV_WIDTH - 1:CONV_WIDTH, :] * zx
    for j in range(CONV_WIDTH - 1):
        off = SUBLANES - (CONV_WIDTH - 1) + j
        xc = xc + cw[j:j + 1, :] * x_scr[pl.ds(off, rr), :]
    tail_ref[slot, unit] = zx[rr - SUBLANES:rr, :]

    nl = -lam
    sp = jnp.maximum(nl, 0.0) + jnp.log1p(jnp.exp(-jnp.abs(nl)))
    rx = jnp.dot(xc.astype(BF16), wax, preferred_element_type=F32)
    r = _sigmoid(rx[:, :nd] + ba)
    ig = _sigmoid(rx[:, nd:] + bx)
    log_a = r * ((-LRU_C) * sp)
    a = jnp.exp(log_a)
    u = jnp.sqrt(-jnp.tanh(log_a) * (a * a + 1.0)) * (ig * xc)

    ns = rr // SUBLANES
    a_scr[...] = a
    u_scr[...] = u
    slab_row = lambda ref, r: ref[pl.ds(r, ns, stride=SUBLANES), :]
    hz, az = [slab_row(u_scr, 0)], [slab_row(a_scr, 0)]
    for r in range(1, SUBLANES):
        ar = slab_row(a_scr, r)
        hz.append(ar * hz[-1] + slab_row(u_scr, r))
        az.append(ar * az[-1])
    carry0 = hc_ref[unit]
    ends, carry = _scan_rows(az[-1], hz[-1], carry0)
    hc_ref[unit] = carry
    first = lax.broadcasted_iota(jnp.int32, ends.shape, 0) == 0
    cin = jnp.where(first, carry0[0:1, :], pltpu.roll(ends, 1, axis=0))
    for r in range(SUBLANES):
        h_scr[pl.ds(r, ns, stride=SUBLANES), :] = hz[r] + az[r] * cin
    h = h_scr[...]
    return h * _gelu_tanh(zgate)


def _mixer_kernel(zq_ref, zf_ref, zv_ref, zg_ref, zx_ref, zgate_ref, lbr_ref, gn_ref,
                  cw_ref, cb_ref, wax_ref, ba_ref, bx_ref, lam_ref,
                  o_ref, st_ref, tail_ref, hc_ref, lf_scr, p_scr, t_scr, a_scr, u_scr, h_scr, x_scr,
                  *, tiles_per_seq):
    unit = pl.program_id(1)
    slot = pl.program_id(0) % 2

    @pl.when(pl.program_id(0) % tiles_per_seq == 0)
    def _():
        st_ref[unit] = jnp.zeros(st_ref.shape[1:], F32)
        tail_ref[1 - slot, unit] = jnp.zeros(tail_ref.shape[2:], F32)
        hc_ref[unit] = jnp.zeros(hc_ref.shape[1:], F32)

    lbr = lbr_ref[...]
    ex = jnp.exp(lbr - jnp.max(lbr, axis=0, keepdims=True))
    lb = ex[0:1, :] / jnp.sum(ex, axis=0, keepdims=True)
    oh = _hgrn_unit(zq_ref[...], zf_ref[...], zv_ref[...], zg_ref[...], lb, gn_ref[...],
                    st_ref, lf_scr, p_scr, t_scr, unit)
    o_ref[:, :LANES] = oh.astype(o_ref.dtype)
    ol = _lru_unit(zx_ref[...], zgate_ref[...], cw_ref[...], cb_ref[...], wax_ref[0],
                   ba_ref[...], bx_ref[...], lam_ref[...], tail_ref, hc_ref,
                   a_scr, u_scr, h_scr, x_scr, unit, slot)
    o_ref[:, LANES:] = ol.astype(o_ref.dtype)


def _mixer(z, lower_bounds, g_norm, conv_w, conv_b, wax, b_a, b_x, lam, seq):
    t = z.shape[0]
    ts = MIX_TS
    nu = HGRN_HEADS
    assert nu == LRU_BLOCKS and HEAD_DIM == LRU_BLOCK_DIM == LANES and seq % ts == 0
    est = 6 * 2 * ts * LANES * 4 + 2 * ts * 2 * LANES * 2 + 7 * ts * LANES * 4 + 4 * (1 << 20)
    est += t * (HGRN_WIDTH + LRU_WIDTH) * 2

    def zspec(part):
        return pl.BlockSpec((ts, LANES), lambda i, j, part=part: (i, part * nu + j))

    def vec(rows):
        return pl.BlockSpec((rows, LANES), lambda i, j: (0, j))

    return pl.pallas_call(
        functools.partial(_mixer_kernel, tiles_per_seq=seq // ts),
        grid=(t // ts, nu),
        in_specs=[zspec(0), zspec(1), zspec(2), zspec(3), zspec(4), zspec(5),
                  vec(lower_bounds.shape[0]),
                  pl.BlockSpec((1, HEAD_DIM), lambda i, j: (0, 0)),
                  vec(CONV_WIDTH), vec(1),
                  pl.BlockSpec((1,) + wax.shape[1:], lambda i, j: (j, 0, 0)),
                  vec(1), vec(1), vec(1)],
        out_specs=pl.BlockSpec((ts, 2 * LANES), lambda i, j: (i, j)),
        out_shape=jax.ShapeDtypeStruct((t, HGRN_WIDTH + LRU_WIDTH), BF16),
        scratch_shapes=[pltpu.VMEM((nu, HEAD_DIM, HEAD_DIM), F32),
                        pltpu.VMEM((2, nu, SUBLANES, LANES), F32),
                        pltpu.VMEM((nu, SUBLANES, LANES), F32)]
        + [pltpu.VMEM((ts, LANES), F32)] * 6 + [pltpu.VMEM((ts + SUBLANES, LANES), F32)],
        compiler_params=pltpu.CompilerParams(
            dimension_semantics=("arbitrary", "arbitrary"),
            vmem_limit_bytes=_vmem_limit(est)),
    )(z, z, z, z, z, z, lower_bounds, g_norm, conv_w, conv_b, wax, b_a, b_x, lam)


def _resident(shape):
    return pl.BlockSpec(shape, lambda i: (0,) * len(shape), pipeline_mode=pl.Buffered(1))


def _outproj_kernel(x_ref, o_ref_in, w_ref, out_ref, wb_ref):
    @pl.when(pl.program_id(0) == 0)
    def _():
        for u in range(HGRN_HEADS):
            wb_ref[pl.ds(2 * u * LANES, LANES), :] = (
                w_ref[pl.ds(u * LANES, LANES), :].astype(BF16))
            wb_ref[pl.ds((2 * u + 1) * LANES, LANES), :] = (
                w_ref[pl.ds(HGRN_WIDTH + u * LANES, LANES), :].astype(BF16))

    out_ref[...] = x_ref[...] + jnp.dot(o_ref_in[...], wb_ref[...],
                                        preferred_element_type=F32)


def _out_proj(x, o, w_out):
    t, d = x.shape
    tm = ROW_TM
    k = o.shape[1]
    est = 2 * 2 * tm * d * 4 + 2 * tm * k * 2 + k * d * (4 + 2)
    return pl.pallas_call(
        _outproj_kernel,
        grid=(t // tm,),
        in_specs=[
            pl.BlockSpec((tm, d), lambda i: (i, 0)),
            pl.BlockSpec((tm, k), lambda i: (i, 0)),
            _resident(w_out.shape),
        ],
        out_specs=pl.BlockSpec((tm, d), lambda i: (i, 0)),
        out_shape=jax.ShapeDtypeStruct((t, d), F32),
        scratch_shapes=[pltpu.VMEM(w_out.shape, BF16)],
        compiler_params=pltpu.CompilerParams(
            dimension_semantics=("arbitrary",),
            vmem_limit_bytes=_vmem_limit(est)),
    )(x, o, w_out)


def _ple_kernel(x_ref, p_ref, gp_ref, wg_ref, bg_ref, wp_ref, gf_ref, o_ref,
                wgb_ref, wpb_ref):
    @pl.when(pl.program_id(0) == 0)
    def _():
        wgb_ref[...] = wg_ref[...].astype(BF16)
        wpb_ref[...] = wp_ref[...].astype(BF16)

    x = x_ref[...]
    h = _rmsnorm(x, gp_ref[...]).astype(BF16)
    gate = jax.nn.sigmoid(
        jnp.dot(h, wgb_ref[...], preferred_element_type=F32) + bg_ref[...])
    emb = jnp.dot(p_ref[...].astype(BF16), wpb_ref[...], preferred_element_type=F32)
    o_ref[...] = _rmsnorm(x + gate * emb, gf_ref[...])


def _ple(x, p, g_ple, w_gate, b_gate, w_proj, g_final):
    t, d = x.shape
    pd = p.shape[1]
    tm = ROW_TM
    est = 2 * 2 * tm * d * 4 + 2 * tm * pd * 4 + (d + pd) * d * (4 + 2) + 2 * tm * d * 4
    row = pl.BlockSpec((1, d), lambda i: (0, 0))
    return pl.pallas_call(
        _ple_kernel,
        grid=(t // tm,),
        in_specs=[
            pl.BlockSpec((tm, d), lambda i: (i, 0)),
            pl.BlockSpec((tm, pd), lambda i: (i, 0)),
            row,
            _resident(w_gate.shape),
            row,
            _resident(w_proj.shape),
            row,
        ],
        out_specs=pl.BlockSpec((tm, d), lambda i: (i, 0)),
        out_shape=jax.ShapeDtypeStruct((t, d), F32),
        scratch_shapes=[pltpu.VMEM(w_gate.shape, BF16), pltpu.VMEM(w_proj.shape, BF16)],
        compiler_params=pltpu.CompilerParams(
            dimension_semantics=("arbitrary",),
            vmem_limit_bytes=_vmem_limit(est)),
    )(x, p, g_ple, w_gate, b_gate, w_proj, g_final)


def kernel(x, p, ffn1_norm, ffn1_w_gate, ffn1_w_up, ffn1_w_down, mix_norm, w_in,
           hgrn_lower_bounds, hgrn_g_norm, conv_w, conv_b, lru_w_a, lru_b_a, lru_w_x,
           lru_b_x, lru_lambda, w_out, ffn2_norm, ffn2_w_gate, ffn2_w_up, ffn2_w_down,
           ple_norm, ple_w_gate, ple_b_gate, ple_w_proj, final_norm):
    batch, seq, d = x.shape
    t = batch * seq
    depth = ffn1_norm.shape[0]
    assert depth == 1, "the shared lower-bound cumsum is specialised to one layer"
    l = 0
    bf = lambda a: a.astype(BF16)
    xt = x.reshape(t, d)

    xt = _ffn(xt, ffn1_norm[l][None], ffn1_w_gate[l], ffn1_w_up[l], ffn1_w_down[l])

    z = _in_proj(xt, mix_norm[l][None], w_in[l])
    wax = bf(jnp.concatenate([lru_w_a[l], lru_w_x[l]], axis=-1))
    mixed = _mixer(z, hgrn_lower_bounds, hgrn_g_norm[l][None], conv_w[l], conv_b[l][None], wax,
                   lru_b_a[l][None], lru_b_x[l][None], lru_lambda[l][None], seq)
    xt = _out_proj(xt, mixed, w_out[l])

    xt = _ffn(xt, ffn2_norm[l][None], ffn2_w_gate[l], ffn2_w_up[l], ffn2_w_down[l])

    out = _ple(xt, p[l].reshape(t, -1), ple_norm[l][None], ple_w_gate[l],
               ple_b_gate[l][None], ple_w_proj[l], final_norm[None])
    return out.reshape(batch, seq, d)
```

```python
import functools

import jax
import jax.numpy as jnp
from jax import lax
from jax.experimental import pallas as pl
from jax.experimental.pallas import tpu as pltpu

F32 = jnp.float32
BF16 = jnp.bfloat16

D_MODEL = 2048
D_FF = 5632
PLE_DIM = 256
HGRN_WIDTH = D_MODEL // 2
HEAD_DIM = 128
HGRN_HEADS = HGRN_WIDTH // HEAD_DIM
LRU_WIDTH = D_MODEL - HGRN_WIDTH
LRU_BLOCKS = 8
LRU_BLOCK_DIM = LRU_WIDTH // LRU_BLOCKS
CONV_WIDTH = 4
LRU_C = 8.0
EPS = 1e-6
LOG2_E = 1.4426950408889634
SQRT_2_OVER_PI = 0.7978845608028654

V7X_VMEM_BYTES = 64 * 1024 * 1024
SUBLANES = 8
LANES = 128

FFN_TM = 1024
FFN_TF = 256
PROJ_TM = 2048
PROJ_TN = 512
ROW_TM = 512
HGRN_CHUNK = 64
HGRN_SUB = 16
MIX_TS = 2048


def _vmem_limit(nbytes):
    return int(min(V7X_VMEM_BYTES - (2 << 20), nbytes + nbytes // 4 + (2 << 20)))


def _sigmoid(x):
    return 1.0 / (1.0 + jnp.exp2(x * (-LOG2_E)))


def _gelu_tanh(x):
    k0 = -2.0 * SQRT_2_OVER_PI * LOG2_E
    return x / (1.0 + jnp.exp2(x * (k0 + (k0 * 0.044715) * (x * x))))


def _first_step_rows(ni, nj):
    def index(i, j):
        return (jnp.minimum(i + (j >= nj // 2).astype(jnp.int32), ni - 1), 0)
    return index


def _rmsnorm(xf, g):
    ms = jnp.mean(xf * xf, axis=-1, keepdims=True)
    return xf * lax.rsqrt(ms + EPS) * g


def _ffn_kernel(x_ref, g_ref, wg_ref, wu_ref, wd_ref, o_ref, h_ref):
    j = pl.program_id(1)

    def step(first):
        if first:
            h_ref[...] = _rmsnorm(x_ref[...], g_ref[...]).astype(BF16)
        h = h_ref[...]
        gate = jnp.dot(h, wg_ref[...].astype(BF16), preferred_element_type=F32)
        up = jnp.dot(h, wu_ref[...].astype(BF16), preferred_element_type=F32)
        act = (0.5 * (gate * jax.nn.sigmoid(gate) * up)).astype(BF16)
        part = jnp.dot(act, wd_ref[...].astype(BF16), preferred_element_type=F32)
        if first:
            o_ref[...] = x_ref[...] + part
        else:
            o_ref[...] += part

    pl.when(j == 0)(functools.partial(step, True))
    pl.when(j > 0)(functools.partial(step, False))


def _ffn(x, g, wg, wu, wd):
    t, d = x.shape
    dff = wg.shape[1]
    tm, tf = FFN_TM, FFN_TF
    wbytes = wg.dtype.itemsize
    est = (2 * tm * d * 4) * 2 + tm * d * 2 + 3 * d * tf * (2 * wbytes + 2) + 3 * tm * tf * 4
    ni, nj = t // tm, dff // tf
    return pl.pallas_call(
        _ffn_kernel,
        grid=(ni, nj),
        in_specs=[
            pl.BlockSpec((tm, d), _first_step_rows(ni, nj)),
            pl.BlockSpec((1, d), lambda i, j: (0, 0)),
            pl.BlockSpec((d, tf), lambda i, j: (0, j)),
            pl.BlockSpec((d, tf), lambda i, j: (0, j)),
            pl.BlockSpec((tf, d), lambda i, j: (j, 0)),
        ],
        out_specs=pl.BlockSpec((tm, d), lambda i, j: (i, 0)),
        out_shape=jax.ShapeDtypeStruct((t, d), F32),
        scratch_shapes=[pltpu.VMEM((tm, d), BF16)],
        compiler_params=pltpu.CompilerParams(
            dimension_semantics=("arbitrary", "arbitrary"),
            vmem_limit_bytes=_vmem_limit(est)),
    )(x, g, wg, wu, wd)


def _proj_kernel(x_ref, g_ref, w_ref, o_ref, h_ref):
    @pl.when(pl.program_id(1) == 0)
    def _():
        h_ref[...] = _rmsnorm(x_ref[...], g_ref[...]).astype(BF16)

    o_ref[...] = jnp.dot(h_ref[...], w_ref[...].astype(BF16), preferred_element_type=F32)


def _in_proj(x, g, w):
    t, d = x.shape
    n = w.shape[1]
    tm, tn = PROJ_TM, PROJ_TN
    est = 2 * tm * d * 4 + tm * d * 2 + d * tn * (2 * w.dtype.itemsize + 2) + 3 * tm * tn * 4
    ni, nj = t // tm, n // tn
    return pl.pallas_call(
        _proj_kernel,
        grid=(ni, nj),
        in_specs=[
            pl.BlockSpec((tm, d), _first_step_rows(ni, nj)),
            pl.BlockSpec((1, d), lambda i, j: (0, 0)),
            pl.BlockSpec((d, tn), lambda i, j: (0, j)),
        ],
        out_specs=pl.BlockSpec((tm, tn), lambda i, j: (i, j)),
        out_shape=jax.ShapeDtypeStruct((t, n), F32),
        scratch_shapes=[pltpu.VMEM((tm, d), BF16)],
        compiler_params=pltpu.CompilerParams(
            dimension_semantics=("arbitrary", "arbitrary"),
            vmem_limit_bytes=_vmem_limit(est)),
    )(x, g, w)


def _hgrn_unit(zq, zf, zv, zg, lb, gn, st_ref, lf_scr, p_scr, t_scr, unit):
    c, sub = HGRN_CHUNK, HGRN_SUB
    rr = zq.shape[0]
    nc = rr // c
    nv = c // SUBLANES
    assert c == 4 * sub and sub == 2 * SUBLANES
    shp4 = (nc, nv, SUBLANES, LANES)
    to4 = lambda x: x.reshape(shp4)
    f = lb + (1.0 - lb) * _sigmoid(zf)
    lf = jnp.log2(f)
    kk = to4(1.0 - f)
    q = to4(zq)

    nvr = rr // SUBLANES
    vreg_row = lambda r: pl.ds(r, nvr, stride=SUBLANES)
    lf_scr[...] = lf
    acc = lf_scr[vreg_row(0), :]
    p_scr[vreg_row(0), :] = acc
    for r in range(1, SUBLANES):
        acc = acc + lf_scr[vreg_row(r), :]
        p_scr[vreg_row(r), :] = acc
    for r in range(SUBLANES):
        t_scr[vreg_row(r), :] = acc
    p8 = to4(p_scr[...])
    t8 = to4(t_scr[...])
    s8 = t8 - p8
    tv = [t8[:, v] for v in range(nv)]
    pv = [p8[:, v] for v in range(nv)]
    sv = [s8[:, v] for v in range(nv)]
    p16 = [pv[v] if v % 2 == 0 else pv[v] + tv[v - 1] for v in range(nv)]
    s16 = [sv[v] + tv[v + 1] if v % 2 == 0 else sv[v] for v in range(nv)]
    t16 = [tv[2 * i] + tv[2 * i + 1] for i in range(nv // 2)]
    p32 = [p16[v] if (v // 2) % 2 == 0 else p16[v] + t16[v // 2 - 1] for v in range(nv)]
    s32 = [s16[v] + t16[v // 2 + 1] if (v // 2) % 2 == 0 else s16[v] for v in range(nv)]
    t32 = [t16[0] + t16[1], t16[2] + t16[3]]
    p64 = [p32[v] if v < nv // 2 else p32[v] + t32[0] for v in range(nv)]
    s64 = [s32[v] + t32[1] if v < nv // 2 else s32[v] for v in range(nv)]
    t64 = t32[0] + t32[1]

    st4 = lambda parts: jnp.stack(parts, axis=1)
    x32 = st4([s32[v] if v < nv // 2 else p32[v] for v in range(nv)])
    x16 = st4([s16[v] if (v // 2) % 2 == 0 else p16[v] for v in range(nv)])
    xd = st4([-sv[v] if v % 2 == 0 else pv[v] for v in range(nv)])
    qk32 = st4([kk[:, v] if v < nv // 2 else q[:, v] for v in range(nv)])
    qk16 = st4([kk[:, v] if (v // 2) % 2 == 0 else q[:, v] for v in range(nv)])

    to3 = lambda x: x.reshape(nc, c, LANES)
    m32 = to3(qk32 * jnp.exp2(x32)).astype(BF16)
    m16 = to3(qk16 * jnp.exp2(x16)).astype(BF16)
    qed = to3(q * jnp.exp2(xd)).astype(BF16)
    ked = to3(kk * jnp.exp2(-xd)).astype(BF16)
    qhat = to3(q * jnp.exp2(st4(p64))).astype(BF16)
    khat = to3(kk * jnp.exp2(st4(s64))).astype(BF16)
    dec = jnp.exp2(t64)
    vb = zv.reshape(nc, c, LANES).astype(BF16)

    zb = jnp.zeros((sub, LANES), BF16)
    blk = lambda x, i: x[i * sub:(i + 1) * sub]
    col = lambda parts: jnp.concatenate(parts, axis=0)
    ti = lax.broadcasted_iota(jnp.int32, (c, c), 0)
    si = lax.broadcasted_iota(jnp.int32, (c, c), 1)
    diag = ((ti // sub) == (si // sub)) & (si <= ti)
    nt = (((1,), (1,)), ((), ()))
    tn = (((0,), (0,)), ((), ()))

    scores = []
    for ci in range(nc):
        a32, a16 = m32[ci], m16[ci]
        q_off = jnp.concatenate([
            col([zb, zb, blk(a32, 2), blk(a32, 3)]),
            col([zb, blk(a16, 1), zb, zb]),
            col([zb, zb, zb, blk(a16, 3)])], axis=1)
        k_off = jnp.concatenate([
            col([blk(a32, 0), blk(a32, 1), zb, zb]),
            col([blk(a16, 0), zb, zb, zb]),
            col([zb, zb, blk(a16, 2), zb])], axis=1)
        s_off = lax.dot_general(q_off, k_off, nt, preferred_element_type=F32)
        s_dia = lax.dot_general(qed[ci], ked[ci], nt, preferred_element_type=F32)
        scores.append((s_off + jnp.where(diag, s_dia, 0.0)).astype(BF16))

    upd = [lax.dot_general(vb[ci], khat[ci], tn, preferred_element_type=F32)
           for ci in range(nc)]
    st = st_ref[unit]
    states = []
    for ci in range(nc):
        states.append(st.astype(BF16))
        st = dec[ci, 0:1, :] * st + upd[ci]
    st_ref[unit] = st

    outs = []
    for ci in range(nc):
        outs.append(lax.dot_general(qhat[ci], states[ci], nt, preferred_element_type=F32)
                    + jnp.dot(scores[ci], vb[ci], preferred_element_type=F32))
    o = jnp.concatenate(outs, axis=0)
    ms = jnp.mean(o * o, axis=-1, keepdims=True)
    return o * lax.rsqrt(ms + EPS) * gn * (zg * _sigmoid(zg))


def _scan_rows(a, b, carry):
    n = a.shape[0] // SUBLANES
    a3 = a.reshape(n, SUBLANES, LANES)
    b3 = b.reshape(n, SUBLANES, LANES)
    row = lax.broadcasted_iota(jnp.int32, a3.shape, 1)
    d = 1
    while d < SUBLANES:
        m = row >= d
        b3 = jnp.where(m, b3 + a3 * pltpu.roll(b3, d, axis=1), b3)
        a3 = jnp.where(m, a3 * pltpu.roll(a3, d, axis=1), a3)
        d *= 2
    es = []
    for s in range(n):
        e = b3[s] + a3[s] * carry
        es.append(e)
        carry = jnp.broadcast_to(e[SUBLANES - 1:SUBLANES, :], e.shape)
    return jnp.concatenate(es, axis=0), carry


def _lru_unit(zx, zgate, cw, cb, wax, ba, bx, lam, tail_ref, hc_ref, a_scr, u_scr, h_scr,
              x_scr, unit, slot):
    rr = zx.shape[0]
    nd = LRU_BLOCK_DIM
    x_scr[0:SUBLANES, :] = tail_ref[1 - slot, unit]
    x_scr[SUBLANES:, :] = zx
    xc = cb + cw[CONV_WIDTH - 1:CONV_WIDTH, :] * zx
    for j in range(CONV_WIDTH - 1):
        off = SUBLANES - (CONV_WIDTH - 1) + j
        xc = xc + cw[j:j + 1, :] * x_scr[pl.ds(off, rr), :]
    tail_ref[slot, unit] = zx[rr - SUBLANES:rr, :]

    nl = -lam
    sp = jnp.maximum(nl, 0.0) + jnp.log1p(jnp.exp(-jnp.abs(nl)))
    rx = jnp.dot(xc.astype(BF16), wax, preferred_element_type=F32)
    r = _sigmoid(rx[:, :nd] + ba)
    ig = _sigmoid(rx[:, nd:] + bx)
    log_a = r * ((-LRU_C) * sp)
    a = jnp.exp(log_a)
    u = jnp.sqrt(-jnp.tanh(log_a) * (a * a + 1.0)) * (ig * xc)

    ns = rr // SUBLANES
    a_scr[...] = a
    u_scr[...] = u
    slab_row = lambda ref, r: ref[pl.ds(r, ns, stride=SUBLANES), :]
    hz, az = [slab_row(u_scr, 0)], [slab_row(a_scr, 0)]
    for r in range(1, SUBLANES):
        ar = slab_row(a_scr, r)
        hz.append(ar * hz[-1] + slab_row(u_scr, r))
        az.append(ar * az[-1])
    carry0 = hc_ref[unit]
    ends, carry = _scan_rows(az[-1], hz[-1], carry0)
    hc_ref[unit] = carry
    first = lax.broadcasted_iota(jnp.int32, ends.shape, 0) == 0
    cin = jnp.where(first, carry0[0:1, :], pltpu.roll(ends, 1, axis=0))
    for r in range(SUBLANES):
        h_scr[pl.ds(r, ns, stride=SUBLANES), :] = hz[r] + az[r] * cin
    h = h_scr[...]
    return h * _gelu_tanh(zgate)


def _mixer_kernel(zq_ref, zf_ref, zv_ref, zg_ref, zx_ref, zgate_ref, lbr_ref, gn_ref,
                  cw_ref, cb_ref, wax_ref, ba_ref, bx_ref, lam_ref,
                  o_ref, st_ref, tail_ref, hc_ref, lf_scr, p_scr, t_scr, a_scr, u_scr, h_scr, x_scr,
                  *, tiles_per_seq):
    unit = pl.program_id(1)
    slot = pl.program_id(0) % 2

    @pl.when(pl.program_id(0) % tiles_per_seq == 0)
    def _():
        st_ref[unit] = jnp.zeros(st_ref.shape[1:], F32)
        tail_ref[1 - slot, unit] = jnp.zeros(tail_ref.shape[2:], F32)
        hc_ref[unit] = jnp.zeros(hc_ref.shape[1:], F32)

    lbr = lbr_ref[...]
    ex = jnp.exp(lbr - jnp.max(lbr, axis=0, keepdims=True))
    lb = ex[0:1, :] / jnp.sum(ex, axis=0, keepdims=True)
    oh = _hgrn_unit(zq_ref[...], zf_ref[...], zv_ref[...], zg_ref[...], lb, gn_ref[...],
                    st_ref, lf_scr, p_scr, t_scr, unit)
    o_ref[:, :LANES] = oh.astype(o_ref.dtype)
    ol = _lru_unit(zx_ref[...], zgate_ref[...], cw_ref[...], cb_ref[...], wax_ref[0],
                   ba_ref[...], bx_ref[...], lam_ref[...], tail_ref, hc_ref,
                   a_scr, u_scr, h_scr, x_scr, unit, slot)
    o_ref[:, LANES:] = ol.astype(o_ref.dtype)


def _mixer(z, lower_bounds, g_norm, conv_w, conv_b, wax, b_a, b_x, lam, seq):
    t = z.shape[0]
    ts = MIX_TS
    nu = HGRN_HEADS
    assert nu == LRU_BLOCKS and HEAD_DIM == LRU_BLOCK_DIM == LANES and seq % ts == 0
    est = 6 * 2 * ts * LANES * 4 + 2 * ts * 2 * LANES * 2 + 7 * ts * LANES * 4 + 4 * (1 << 20)
    est += t * (HGRN_WIDTH + LRU_WIDTH) * 2

    def zspec(part):
        return pl.BlockSpec((ts, LANES), lambda i, j, part=part: (i, part * nu + j))

    def vec(rows):
        return pl.BlockSpec((rows, LANES), lambda i, j: (0, j))

    return pl.pallas_call(
        functools.partial(_mixer_kernel, tiles_per_seq=seq // ts),
        grid=(t // ts, nu),
        in_specs=[zspec(0), zspec(1), zspec(2), zspec(3), zspec(4), zspec(5),
                  vec(lower_bounds.shape[0]),
                  pl.BlockSpec((1, HEAD_DIM), lambda i, j: (0, 0)),
                  vec(CONV_WIDTH), vec(1),
                  pl.BlockSpec((1,) + wax.shape[1:], lambda i, j: (j, 0, 0)),
                  vec(1), vec(1), vec(1)],
        out_specs=pl.BlockSpec((ts, 2 * LANES), lambda i, j: (i, j)),
        out_shape=jax.ShapeDtypeStruct((t, HGRN_WIDTH + LRU_WIDTH), BF16),
        scratch_shapes=[pltpu.VMEM((nu, HEAD_DIM, HEAD_DIM), F32),
                        pltpu.VMEM((2, nu, SUBLANES, LANES), F32),
                        pltpu.VMEM((nu, SUBLANES, LANES), F32)]
        + [pltpu.VMEM((ts, LANES), F32)] * 6 + [pltpu.VMEM((ts + SUBLANES, LANES), F32)],
        compiler_params=pltpu.CompilerParams(
            dimension_semantics=("arbitrary", "arbitrary"),
            vmem_limit_bytes=_vmem_limit(est)),
    )(z, z, z, z, z, z, lower_bounds, g_norm, conv_w, conv_b, wax, b_a, b_x, lam)


def _resident(shape):
    return pl.BlockSpec(shape, lambda i: (0,) * len(shape), pipeline_mode=pl.Buffered(1))


def _outproj_kernel(x_ref, o_ref_in, w_ref, out_ref, wb_ref):
    @pl.when(pl.program_id(0) == 0)
    def _():
        for u in range(HGRN_HEADS):
            wb_ref[pl.ds(2 * u * LANES, LANES), :] = (
                w_ref[pl.ds(u * LANES, LANES), :].astype(BF16))
            wb_ref[pl.ds((2 * u + 1) * LANES, LANES), :] = (
                w_ref[pl.ds(HGRN_WIDTH + u * LANES, LANES), :].astype(BF16))

    out_ref[...] = x_ref[...] + jnp.dot(o_ref_in[...], wb_ref[...],
                                        preferred_element_type=F32)


def _out_proj(x, o, w_out):
    t, d = x.shape
    tm = ROW_TM
    k = o.shape[1]
    est = 2 * 2 * tm * d * 4 + 2 * tm * k * 2 + k * d * (4 + 2)
    return pl.pallas_call(
        _outproj_kernel,
        grid=(t // tm,),
        in_specs=[
            pl.BlockSpec((tm, d), lambda i: (i, 0)),
            pl.BlockSpec((tm, k), lambda i: (i, 0)),
            _resident(w_out.shape),
        ],
        out_specs=pl.BlockSpec((tm, d), lambda i: (i, 0)),
        out_shape=jax.ShapeDtypeStruct((t, d), F32),
        scratch_shapes=[pltpu.VMEM(w_out.shape, BF16)],
        compiler_params=pltpu.CompilerParams(
            dimension_semantics=("arbitrary",),
            vmem_limit_bytes=_vmem_limit(est)),
    )(x, o, w_out)


def _ple_kernel(x_ref, p_ref, gp_ref, wg_ref, bg_ref, wp_ref, gf_ref, o_ref,
                wgb_ref, wpb_ref):
    @pl.when(pl.program_id(0) == 0)
    def _():
        wgb_ref[...] = wg_ref[...].astype(BF16)
        wpb_ref[...] = wp_ref[...].astype(BF16)

    x = x_ref[...]
    h = _rmsnorm(x, gp_ref[...]).astype(BF16)
    gate = jax.nn.sigmoid(
        jnp.dot(h, wgb_ref[...], preferred_element_type=F32) + bg_ref[...])
    emb = jnp.dot(p_ref[...].astype(BF16), wpb_ref[...], preferred_element_type=F32)
    o_ref[...] = _rmsnorm(x + gate * emb, gf_ref[...])


def _ple(x, p, g_ple, w_gate, b_gate, w_proj, g_final):
    t, d = x.shape
    pd = p.shape[1]
    tm = ROW_TM
    est = 2 * 2 * tm * d * 4 + 2 * tm * pd * 4 + (d + pd) * d * (4 + 2) + 2 * tm * d * 4
    row = pl.BlockSpec((1, d), lambda i: (0, 0))
    return pl.pallas_call(
        _ple_kernel,
        grid=(t // tm,),
        in_specs=[
            pl.BlockSpec((tm, d), lambda i: (i, 0)),
            pl.BlockSpec((tm, pd), lambda i: (i, 0)),
            row,
            _resident(w_gate.shape),
            row,
            _resident(w_proj.shape),
            row,
        ],
        out_specs=pl.BlockSpec((tm, d), lambda i: (i, 0)),
        out_shape=jax.ShapeDtypeStruct((t, d), F32),
        scratch_shapes=[pltpu.VMEM(w_gate.shape, BF16), pltpu.VMEM(w_proj.shape, BF16)],
        compiler_params=pltpu.CompilerParams(
            dimension_semantics=("arbitrary",),
            vmem_limit_bytes=_vmem_limit(est)),
    )(x, p, g_ple, w_gate, b_gate, w_proj, g_final)


def kernel(x, p, ffn1_norm, ffn1_w_gate, ffn1_w_up, ffn1_w_down, mix_norm, w_in,
           hgrn_lower_bounds, hgrn_g_norm, conv_w, conv_b, lru_w_a, lru_b_a, lru_w_x,
           lru_b_x, lru_lambda, w_out, ffn2_norm, ffn2_w_gate, ffn2_w_up, ffn2_w_down,
           ple_norm, ple_w_gate, ple_b_gate, ple_w_proj, final_norm):
    batch, seq, d = x.shape
    t = batch * seq
    depth = ffn1_norm.shape[0]
    assert depth == 1, "the shared lower-bound cumsum is specialised to one layer"
    l = 0
    bf = lambda a: a.astype(BF16)
    xt = x.reshape(t, d)

    xt = _ffn(xt, ffn1_norm[l][None], ffn1_w_gate[l], ffn1_w_up[l], ffn1_w_down[l])

    z = _in_proj(xt, mix_norm[l][None], w_in[l])
    wax = bf(jnp.concatenate([lru_w_a[l], lru_w_x[l]], axis=-1))
    mixed = _mixer(z, hgrn_lower_bounds, hgrn_g_norm[l][None], conv_w[l], conv_b[l][None], wax,
                   lru_b_a[l][None], lru_b_x[l][None], lru_lambda[l][None], seq)
    xt = _out_proj(xt, mixed, w_out[l])

    xt = _ffn(xt, ffn2_norm[l][None], ffn2_w_gate[l], ffn2_w_up[l], ffn2_w_down[l])

    out = _ple(xt, p[l].reshape(t, -1), ple_norm[l][None], ple_w_gate[l],
               ple_b_gate[l][None], ple_w_proj[l], final_norm[None])
    return out.reshape(batch, seq, d)
```

```python
import functools

import jax
import jax.numpy as jnp
from jax import lax
from jax.experimental import pallas as pl
from jax.experimental.pallas import tpu as pltpu

F32 = jnp.float32
BF16 = jnp.bfloat16

D_MODEL = 2048
D_FF = 5632
PLE_DIM = 256
HGRN_WIDTH = D_MODEL // 2
HEAD_DIM = 128
HGRN_HEADS = HGRN_WIDTH // HEAD_DIM
LRU_WIDTH = D_MODEL - HGRN_WIDTH
LRU_BLOCKS = 8
LRU_BLOCK_DIM = LRU_WIDTH // LRU_BLOCKS
CONV_WIDTH = 4
LRU_C = 8.0
EPS = 1e-6
LOG2_E = 1.4426950408889634
SQRT_2_OVER_PI = 0.7978845608028654

V7X_VMEM_BYTES = 64 * 1024 * 1024
SUBLANES = 8
LANES = 128

FFN_TM = 1024
FFN_TF = 512
FFN_SUB = 256
PROJ_TM = 2048
PROJ_TN = 512
ROW_TM = 512
HGRN_CHUNK = 64
HGRN_SUB = 16
MIX_TS = 2048


def _vmem_limit(nbytes):
    return int(min(V7X_VMEM_BYTES - (2 << 20), nbytes + nbytes // 4 + (2 << 20)))


def _sigmoid(x):
    return 1.0 / (1.0 + jnp.exp2(x * (-LOG2_E)))


def _gelu_tanh(x):
    k0 = -2.0 * SQRT_2_OVER_PI * LOG2_E
    return x / (1.0 + jnp.exp2(x * (k0 + (k0 * 0.044715) * (x * x))))


def _rmsnorm(xf, g):
    ms = jnp.mean(xf * xf, axis=-1, keepdims=True)
    return xf * lax.rsqrt(ms + EPS) * g


def _ffn_kernel(x_hbm, g_ref, wg_ref, wu_ref, wd_ref, o_ref, h_ref, x_ref, x_sem):
    i, j = pl.program_id(0), pl.program_id(1)
    tm = x_ref.shape[0]

    def x_copy(tile):
        return pltpu.make_async_copy(x_hbm.at[pl.ds(tile * tm, tm), :], x_ref, x_sem)

    @pl.when((i == 0) & (j == 0))
    def _():
        x_copy(0).start()

    @pl.when(j == 0)
    def _():
        x_copy(i).wait()

    @pl.when((j == 1) & (i + 1 < pl.num_programs(0)))
    def _():
        x_copy(i + 1).start()

    def step(first):
        if first:
            h_ref[...] = _rmsnorm(x_ref[...], g_ref[...]).astype(BF16)
        h = h_ref[...]
        for k in range(wg_ref.shape[1] // FFN_SUB):
            cols = pl.ds(k * FFN_SUB, FFN_SUB)
            gate = jnp.dot(h, wg_ref[:, cols].astype(BF16), preferred_element_type=F32)
            up = jnp.dot(h, wu_ref[:, cols].astype(BF16), preferred_element_type=F32)
            act = (0.5 * (gate * jax.nn.sigmoid(gate) * up)).astype(BF16)
            part = jnp.dot(act, wd_ref[cols, :].astype(BF16), preferred_element_type=F32)
            if first and k == 0:
                o_ref[...] = x_ref[...] + part
            else:
                o_ref[...] += part

    pl.when(j == 0)(functools.partial(step, True))
    pl.when(j > 0)(functools.partial(step, False))


def _ffn(x, g, wg, wu, wd):
    t, d = x.shape
    dff = wg.shape[1]
    tm, tf = FFN_TM, FFN_TF
    assert dff // tf >= 2, "the next x tile is fetched from the second column step on"
    wbytes = wg.dtype.itemsize
    est = (2 * tm * d * 4) + tm * d * 4 + tm * d * 2 + 3 * d * tf * (2 * wbytes) + 3 * d * FFN_SUB * 2 \
        + 3 * tm * FFN_SUB * 4
    return pl.pallas_call(
        _ffn_kernel,
        grid=(t // tm, dff // tf),
        in_specs=[
            pl.BlockSpec(memory_space=pl.ANY),
            pl.BlockSpec((1, d), lambda i, j: (0, 0)),
            pl.BlockSpec((d, tf), lambda i, j: (0, j)),
            pl.BlockSpec((d, tf), lambda i, j: (0, j)),
            pl.BlockSpec((tf, d), lambda i, j: (j, 0)),
        ],
        out_specs=pl.BlockSpec((tm, d), lambda i, j: (i, 0)),
        out_shape=jax.ShapeDtypeStruct((t, d), F32),
        scratch_shapes=[pltpu.VMEM((tm, d), BF16), pltpu.VMEM((tm, d), F32),
                        pltpu.SemaphoreType.DMA(())],
        compiler_params=pltpu.CompilerParams(
            dimension_semantics=("arbitrary", "arbitrary"),
            vmem_limit_bytes=_vmem_limit(est)),
    )(x, g, wg, wu, wd)


def _proj_kernel(x_ref, g_ref, w_ref, o_ref, h_ref):
    @pl.when(pl.program_id(1) == 0)
    def _():
        h_ref[...] = _rmsnorm(x_ref[...], g_ref[...]).astype(BF16)

    o_ref[...] = jnp.dot(h_ref[...], w_ref[...].astype(BF16), preferred_element_type=F32)


def _in_proj(x, g, w):
    t, d = x.shape
    n = w.shape[1]
    tm, tn = PROJ_TM, PROJ_TN
    est = 2 * tm * d * 4 + tm * d * 2 + d * tn * (2 * w.dtype.itemsize + 2) + 3 * tm * tn * 4
    return pl.pallas_call(
        _proj_kernel,
        grid=(t // tm, n // tn),
        in_specs=[
            pl.BlockSpec((tm, d), lambda i, j: (i, 0)),
            pl.BlockSpec((1, d), lambda i, j: (0, 0)),
            pl.BlockSpec((d, tn), lambda i, j: (0, j)),
        ],
        out_specs=pl.BlockSpec((tm, tn), lambda i, j: (i, j)),
        out_shape=jax.ShapeDtypeStruct((t, n), F32),
        scratch_shapes=[pltpu.VMEM((tm, d), BF16)],
        compiler_params=pltpu.CompilerParams(
            dimension_semantics=("arbitrary", "arbitrary"),
            vmem_limit_bytes=_vmem_limit(est)),
    )(x, g, w)


def _hgrn_unit(zq, zf, zv, zg, lb, gn, st_ref, lf_scr, p_scr, t_scr, unit):
    c, sub = HGRN_CHUNK, HGRN_SUB
    rr = zq.shape[0]
    nc = rr // c
    nv = c // SUBLANES
    assert c == 4 * sub and sub == 2 * SUBLANES
    shp4 = (nc, nv, SUBLANES, LANES)
    to4 = lambda x: x.reshape(shp4)
    f = lb + (1.0 - lb) * _sigmoid(zf)
    lf = jnp.log2(f)
    kk = to4(1.0 - f)
    q = to4(zq)

    nvr = rr // SUBLANES
    vreg_row = lambda r: pl.ds(r, nvr, stride=SUBLANES)
    lf_scr[...] = lf
    acc = lf_scr[vreg_row(0), :]
    p_scr[vreg_row(0), :] = acc
    for r in range(1, SUBLANES):
        acc = acc + lf_scr[vreg_row(r), :]
        p_scr[vreg_row(r), :] = acc
    for r in range(SUBLANES):
        t_scr[vreg_row(r), :] = acc
    p8 = to4(p_scr[...])
    t8 = to4(t_scr[...])
    s8 = t8 - p8
    tv = [t8[:, v] for v in range(nv)]
    pv = [p8[:, v] for v in range(nv)]
    sv = [s8[:, v] for v in range(nv)]
    p16 = [pv[v] if v % 2 == 0 else pv[v] + tv[v - 1] for v in range(nv)]
    s16 = [sv[v] + tv[v + 1] if v % 2 == 0 else sv[v] for v in range(nv)]
    t16 = [tv[2 * i] + tv[2 * i + 1] for i in range(nv // 2)]
    p32 = [p16[v] if (v // 2) % 2 == 0 else p16[v] + t16[v // 2 - 1] for v in range(nv)]
    s32 = [s16[v] + t16[v // 2 + 1] if (v // 2) % 2 == 0 else s16[v] for v in range(nv)]
    t32 = [t16[0] + t16[1], t16[2] + t16[3]]
    p64 = [p32[v] if v < nv // 2 else p32[v] + t32[0] for v in range(nv)]
    s64 = [s32[v] + t32[1] if v < nv // 2 else s32[v] for v in range(nv)]
    t64 = t32[0] + t32[1]

    st4 = lambda parts: jnp.stack(parts, axis=1)
    x32 = st4([s32[v] if v < nv // 2 else p32[v] for v in range(nv)])
    x16 = st4([s16[v] if (v // 2) % 2 == 0 else p16[v] for v in range(nv)])
    xd = st4([-sv[v] if v % 2 == 0 else pv[v] for v in range(nv)])
    qk32 = st4([kk[:, v] if v < nv // 2 else q[:, v] for v in range(nv)])
    qk16 = st4([kk[:, v] if (v // 2) % 2 == 0 else q[:, v] for v in range(nv)])

    to3 = lambda x: x.reshape(nc, c, LANES)
    m32 = to3(qk32 * jnp.exp2(x32)).astype(BF16)
    m16 = to3(qk16 * jnp.exp2(x16)).astype(BF16)
    qed = to3(q * jnp.exp2(xd)).astype(BF16)
    ked = to3(kk * jnp.exp2(-xd)).astype(BF16)
    qhat = to3(q * jnp.exp2(st4(p64))).astype(BF16)
    khat = to3(kk * jnp.exp2(st4(s64))).astype(BF16)
    dec = jnp.exp2(t64)
    vb = zv.reshape(nc, c, LANES).astype(BF16)

    zb = jnp.zeros((sub, LANES), BF16)
    blk = lambda x, i: x[i * sub:(i + 1) * sub]
    col = lambda parts: jnp.concatenate(parts, axis=0)
    ti = lax.broadcasted_iota(jnp.int32, (c, c), 0)
    si = lax.broadcasted_iota(jnp.int32, (c, c), 1)
    diag = ((ti // sub) == (si // sub)) & (si <= ti)
    nt = (((1,), (1,)), ((), ()))
    tn = (((0,), (0,)), ((), ()))

    scores = []
    for ci in range(nc):
        a32, a16 = m32[ci], m16[ci]
        q_off = jnp.concatenate([
            col([zb, zb, blk(a32, 2), blk(a32, 3)]),
            col([zb, blk(a16, 1), zb, zb]),
            col([zb, zb, zb, blk(a16, 3)])], axis=1)
        k_off = jnp.concatenate([
            col([blk(a32, 0), blk(a32, 1), zb, zb]),
            col([blk(a16, 0), zb, zb, zb]),
            col([zb, zb, blk(a16, 2), zb])], axis=1)
        s_off = lax.dot_general(q_off, k_off, nt, preferred_element_type=F32)
        s_dia = lax.dot_general(qed[ci], ked[ci], nt, preferred_element_type=F32)
        scores.append((s_off + jnp.where(diag, s_dia, 0.0)).astype(BF16))

    upd = [lax.dot_general(vb[ci], khat[ci], tn, preferred_element_type=F32)
           for ci in range(nc)]
    st = st_ref[unit]
    states = []
    for ci in range(nc):
        states.append(st.astype(BF16))
        st = dec[ci, 0:1, :] * st + upd[ci]
    st_ref[unit] = st

    outs = []
    for ci in range(nc):
        outs.append(lax.dot_general(qhat[ci], states[ci], nt, preferred_element_type=F32)
                    + jnp.dot(scores[ci], vb[ci], preferred_element_type=F32))
    o = jnp.concatenate(outs, axis=0)
    ms = jnp.mean(o * o, axis=-1, keepdims=True)
    return o * lax.rsqrt(ms + EPS) * gn * (zg * _sigmoid(zg))


def _scan_rows(a, b, carry):
    n = a.shape[0] // SUBLANES
    a3 = a.reshape(n, SUBLANES, LANES)
    b3 = b.reshape(n, SUBLANES, LANES)
    row = lax.broadcasted_iota(jnp.int32, a3.shape, 1)
    d = 1
    while d < SUBLANES:
        m = row >= d
        b3 = jnp.where(m, b3 + a3 * pltpu.roll(b3, d, axis=1), b3)
        a3 = jnp.where(m, a3 * pltpu.roll(a3, d, axis=1), a3)
        d *= 2
    es = []
    for s in range(n):
        e = b3[s] + a3[s] * carry
        es.append(e)
        carry = jnp.broadcast_to(e[SUBLANES - 1:SUBLANES, :], e.shape)
    return jnp.concatenate(es, axis=0), carry


def _lru_unit(zx, zgate, cw, cb, wax, ba, bx, lam, tail_ref, hc_ref, a_scr, u_scr, h_scr,
              x_scr, unit, slot):
    rr = zx.shape[0]
    nd = LRU_BLOCK_DIM
    x_scr[0:SUBLANES, :] = tail_ref[1 - slot, unit]
    x_scr[SUBLANES:, :] = zx
    xc = cb + cw[CONV_WIDTH - 1:CONV_WIDTH, :] * zx
    for j in range(CONV_WIDTH - 1):
        off = SUBLANES - (CONV_WIDTH - 1) + j
        xc = xc + cw[j:j + 1, :] * x_scr[pl.ds(off, rr), :]
    tail_ref[slot, unit] = zx[rr - SUBLANES:rr, :]

    nl = -lam
    sp = jnp.maximum(nl, 0.0) + jnp.log1p(jnp.exp(-jnp.abs(nl)))
    rx = jnp.dot(xc.astype(BF16), wax, preferred_element_type=F32)
    r = _sigmoid(rx[:, :nd] + ba)
    ig = _sigmoid(rx[:, nd:] + bx)
    log_a = r * ((-LRU_C) * sp)
    a = jnp.exp(log_a)
    u = jnp.sqrt(-jnp.tanh(log_a) * (a * a + 1.0)) * (ig * xc)

    ns = rr // SUBLANES
    a_scr[...] = a
    u_scr[...] = u
    slab_row = lambda ref, r: ref[pl.ds(r, ns, stride=SUBLANES), :]
    hz, az = [slab_row(u_scr, 0)], [slab_row(a_scr, 0)]
    for r in range(1, SUBLANES):
        ar = slab_row(a_scr, r)
        hz.append(ar * hz[-1] + slab_row(u_scr, r))
        az.append(ar * az[-1])
    carry0 = hc_ref[unit]
    ends, carry = _scan_rows(az[-1], hz[-1], carry0)
    hc_ref[unit] = carry
    first = lax.broadcasted_iota(jnp.int32, ends.shape, 0) == 0
    cin = jnp.where(first, carry0[0:1, :], pltpu.roll(ends, 1, axis=0))
    for r in range(SUBLANES):
        h_scr[pl.ds(r, ns, stride=SUBLANES), :] = hz[r] + az[r] * cin
    h = h_scr[...]
    return h * _gelu_tanh(zgate)


def _mixer_kernel(zq_ref, zf_ref, zv_ref, zg_ref, zx_ref, zgate_ref, lbr_ref, gn_ref,
                  cw_ref, cb_ref, wax_ref, ba_ref, bx_ref, lam_ref,
                  o_ref, st_ref, tail_ref, hc_ref, lf_scr, p_scr, t_scr, a_scr, u_scr, h_scr, x_scr,
                  *, tiles_per_seq):
    unit = pl.program_id(1)
    slot = pl.program_id(0) % 2

    @pl.when(pl.program_id(0) % tiles_per_seq == 0)
    def _():
        st_ref[unit] = jnp.zeros(st_ref.shape[1:], F32)
        tail_ref[1 - slot, unit] = jnp.zeros(tail_ref.shape[2:], F32)
        hc_ref[unit] = jnp.zeros(hc_ref.shape[1:], F32)

    lbr = lbr_ref[...]
    ex = jnp.exp(lbr - jnp.max(lbr, axis=0, keepdims=True))
    lb = ex[0:1, :] / jnp.sum(ex, axis=0, keepdims=True)
    oh = _hgrn_unit(zq_ref[...], zf_ref[...], zv_ref[...], zg_ref[...], lb, gn_ref[...],
                    st_ref, lf_scr, p_scr, t_scr, unit)
    o_ref[:, :LANES] = oh.astype(o_ref.dtype)
    ol = _lru_unit(zx_ref[...], zgate_ref[...], cw_ref[...], cb_ref[...], wax_ref[0],
                   ba_ref[...], bx_ref[...], lam_ref[...], tail_ref, hc_ref,
                   a_scr, u_scr, h_scr, x_scr, unit, slot)
    o_ref[:, LANES:] = ol.astype(o_ref.dtype)


def _mixer(z, lower_bounds, g_norm, conv_w, conv_b, wax, b_a, b_x, lam, seq):
    t = z.shape[0]
    ts = MIX_TS
    nu = HGRN_HEADS
    assert nu == LRU_BLOCKS and HEAD_DIM == LRU_BLOCK_DIM == LANES and seq % ts == 0
    est = 6 * 2 * ts * LANES * 4 + 2 * ts * 2 * LANES * 2 + 7 * ts * LANES * 4 + 4 * (1 << 20)
    est += t * (HGRN_WIDTH + LRU_WIDTH) * 2

    def zspec(part):
        return pl.BlockSpec((ts, LANES), lambda i, j, part=part: (i, part * nu + j))

    def vec(rows):
        return pl.BlockSpec((rows, LANES), lambda i, j: (0, j))

    return pl.pallas_call(
        functools.partial(_mixer_kernel, tiles_per_seq=seq // ts),
        grid=(t // ts, nu),
        in_specs=[zspec(0), zspec(1), zspec(2), zspec(3), zspec(4), zspec(5),
                  vec(lower_bounds.shape[0]),
                  pl.BlockSpec((1, HEAD_DIM), lambda i, j: (0, 0)),
                  vec(CONV_WIDTH), vec(1),
                  pl.BlockSpec((1,) + wax.shape[1:], lambda i, j: (j, 0, 0)),
                  vec(1), vec(1), vec(1)],
        out_specs=pl.BlockSpec((ts, 2 * LANES), lambda i, j: (i, j)),
        out_shape=jax.ShapeDtypeStruct((t, HGRN_WIDTH + LRU_WIDTH), BF16),
        scratch_shapes=[pltpu.VMEM((nu, HEAD_DIM, HEAD_DIM), F32),
                        pltpu.VMEM((2, nu, SUBLANES, LANES), F32),
                        pltpu.VMEM((nu, SUBLANES, LANES), F32)]
        + [pltpu.VMEM((ts, LANES), F32)] * 6 + [pltpu.VMEM((ts + SUBLANES, LANES), F32)],
        compiler_params=pltpu.CompilerParams(
            dimension_semantics=("arbitrary", "arbitrary"),
            vmem_limit_bytes=_vmem_limit(est)),
    )(z, z, z, z, z, z, lower_bounds, g_norm, conv_w, conv_b, wax, b_a, b_x, lam)


def _resident(shape):
    return pl.BlockSpec(shape, lambda i: (0,) * len(shape), pipeline_mode=pl.Buffered(1))


def _outproj_kernel(x_ref, o_ref_in, w_ref, out_ref, wb_ref):
    @pl.when(pl.program_id(0) == 0)
    def _():
        for u in range(HGRN_HEADS):
            wb_ref[pl.ds(2 * u * LANES, LANES), :] = (
                w_ref[pl.ds(u * LANES, LANES), :].astype(BF16))
            wb_ref[pl.ds((2 * u + 1) * LANES, LANES), :] = (
                w_ref[pl.ds(HGRN_WIDTH + u * LANES, LANES), :].astype(BF16))

    out_ref[...] = x_ref[...] + jnp.dot(o_ref_in[...], wb_ref[...],
                                        preferred_element_type=F32)


def _out_proj(x, o, w_out):
    t, d = x.shape
    tm = ROW_TM
    k = o.shape[1]
    est = 2 * 2 * tm * d * 4 + 2 * tm * k * 2 + k * d * (4 + 2)
    return pl.pallas_call(
        _outproj_kernel,
        grid=(t // tm,),
        in_specs=[
            pl.BlockSpec((tm, d), lambda i: (i, 0)),
            pl.BlockSpec((tm, k), lambda i: (i, 0)),
            _resident(w_out.shape),
        ],
        out_specs=pl.BlockSpec((tm, d), lambda i: (i, 0)),
        out_shape=jax.ShapeDtypeStruct((t, d), F32),
        scratch_shapes=[pltpu.VMEM(w_out.shape, BF16)],
        compiler_params=pltpu.CompilerParams(
            dimension_semantics=("arbitrary",),
            vmem_limit_bytes=_vmem_limit(est)),
    )(x, o, w_out)


def _ple_kernel(x_ref, p_ref, gp_ref, wg_ref, bg_ref, wp_ref, gf_ref, o_ref,
                wgb_ref, wpb_ref):
    @pl.when(pl.program_id(0) == 0)
    def _():
        wgb_ref[...] = wg_ref[...].astype(BF16)
        wpb_ref[...] = wp_ref[...].astype(BF16)

    x = x_ref[...]
    h = _rmsnorm(x, gp_ref[...]).astype(BF16)
    gate = jax.nn.sigmoid(
        jnp.dot(h, wgb_ref[...], preferred_element_type=F32) + bg_ref[...])
    emb = jnp.dot(p_ref[...].astype(BF16), wpb_ref[...], preferred_element_type=F32)
    o_ref[...] = _rmsnorm(x + gate * emb, gf_ref[...])


def _ple(x, p, g_ple, w_gate, b_gate, w_proj, g_final):
    t, d = x.shape
    pd = p.shape[1]
    tm = ROW_TM
    est = 2 * 2 * tm * d * 4 + 2 * tm * pd * 4 + (d + pd) * d * (4 + 2) + 2 * tm * d * 4
    row = pl.BlockSpec((1, d), lambda i: (0, 0))
    return pl.pallas_call(
        _ple_kernel,
        grid=(t // tm,),
        in_specs=[
            pl.BlockSpec((tm, d), lambda i: (i, 0)),
            pl.BlockSpec((tm, pd), lambda i: (i, 0)),
            row,
            _resident(w_gate.shape),
            row,
            _resident(w_proj.shape),
            row,
        ],
        out_specs=pl.BlockSpec((tm, d), lambda i: (i, 0)),
        out_shape=jax.ShapeDtypeStruct((t, d), F32),
        scratch_shapes=[pltpu.VMEM(w_gate.shape, BF16), pltpu.VMEM(w_proj.shape, BF16)],
        compiler_params=pltpu.CompilerParams(
            dimension_semantics=("arbitrary",),
            vmem_limit_bytes=_vmem_limit(est)),
    )(x, p, g_ple, w_gate, b_gate, w_proj, g_final)


def kernel(x, p, ffn1_norm, ffn1_w_gate, ffn1_w_up, ffn1_w_down, mix_norm, w_in,
           hgrn_lower_bounds, hgrn_g_norm, conv_w, conv_b, lru_w_a, lru_b_a, lru_w_x,
           lru_b_x, lru_lambda, w_out, ffn2_norm, ffn2_w_gate, ffn2_w_up, ffn2_w_down,
           ple_norm, ple_w_gate, ple_b_gate, ple_w_proj, final_norm):
    batch, seq, d = x.shape
    t = batch * seq
    depth = ffn1_norm.shape[0]
    assert depth == 1, "the shared lower-bound cumsum is specialised to one layer"
    l = 0
    bf = lambda a: a.astype(BF16)
    xt = x.reshape(t, d)

    xt = _ffn(xt, ffn1_norm[l][None], ffn1_w_gate[l], ffn1_w_up[l], ffn1_w_down[l])

    z = _in_proj(xt, mix_norm[l][None], w_in[l])
    wax = bf(jnp.concatenate([lru_w_a[l], lru_w_x[l]], axis=-1))
    mixed = _mixer(z, hgrn_lower_bounds, hgrn_g_norm[l][None], conv_w[l], conv_b[l][None], wax,
                   lru_b_a[l][None], lru_b_x[l][None], lru_lambda[l][None], seq)
    xt = _out_proj(xt, mixed, w_out[l])

    xt = _ffn(xt, ffn2_norm[l][None], ffn2_w_gate[l], ffn2_w_up[l], ffn2_w_down[l])

    out = _ple(xt, p[l].reshape(t, -1), ple_norm[l][None], ple_w_gate[l],
               ple_b_gate[l][None], ple_w_proj[l], final_norm[None])
    return out.reshape(batch, seq, d)
```

```python
import functools

import jax
import jax.numpy as jnp
from jax import lax
from jax.experimental import pallas as pl
from jax.experimental.pallas import tpu as pltpu

F32 = jnp.float32
BF16 = jnp.bfloat16

D_MODEL = 2048
D_FF = 5632
PLE_DIM = 256
HGRN_WIDTH = D_MODEL // 2
HEAD_DIM = 128
HGRN_HEADS = HGRN_WIDTH // HEAD_DIM
LRU_WIDTH = D_MODEL - HGRN_WIDTH
LRU_BLOCKS = 8
LRU_BLOCK_DIM = LRU_WIDTH // LRU_BLOCKS
CONV_WIDTH = 4
LRU_C = 8.0
EPS = 1e-6
LOG2_E = 1.4426950408889634
SQRT_2_OVER_PI = 0.7978845608028654

V7X_VMEM_BYTES = 64 * 1024 * 1024
SUBLANES = 8
LANES = 128

FFN_TM = 1024
FFN_TF = 512
FFN_SUB = 256
PROJ_TM = 2048
PROJ_TN = 768
PROJ_SUB = 256
ROW_TM = 512
HGRN_CHUNK = 64
HGRN_SUB = 16
MIX_TS = 2048


def _vmem_limit(nbytes):
    return int(min(V7X_VMEM_BYTES - (2 << 20), nbytes + nbytes // 4 + (2 << 20)))


def _sigmoid(x):
    return 1.0 / (1.0 + jnp.exp2(x * (-LOG2_E)))


def _gelu_tanh(x):
    k0 = -2.0 * SQRT_2_OVER_PI * LOG2_E
    return x / (1.0 + jnp.exp2(x * (k0 + (k0 * 0.044715) * (x * x))))


def _rmsnorm(xf, g):
    ms = jnp.mean(xf * xf, axis=-1, keepdims=True)
    return xf * lax.rsqrt(ms + EPS) * g


def _stream_row_tile(x_hbm, x_ref, x_sem):
    i, j = pl.program_id(0), pl.program_id(1)
    tm = x_ref.shape[0]

    def x_copy(tile):
        return pltpu.make_async_copy(x_hbm.at[pl.ds(tile * tm, tm), :], x_ref, x_sem)

    @pl.when((i == 0) & (j == 0))
    def _():
        x_copy(0).start()

    @pl.when(j == 0)
    def _():
        x_copy(i).wait()

    @pl.when((j == 1) & (i + 1 < pl.num_programs(0)))
    def _():
        x_copy(i + 1).start()


def _ffn_kernel(x_hbm, g_ref, wg_ref, wu_ref, wd_ref, o_ref, h_ref, x_ref, x_sem):
    j = pl.program_id(1)
    _stream_row_tile(x_hbm, x_ref, x_sem)

    def step(first):
        if first:
            h_ref[...] = _rmsnorm(x_ref[...], g_ref[...]).astype(BF16)
        h = h_ref[...]
        for k in range(wg_ref.shape[1] // FFN_SUB):
            cols = pl.ds(k * FFN_SUB, FFN_SUB)
            gate = jnp.dot(h, wg_ref[:, cols].astype(BF16), preferred_element_type=F32)
            up = jnp.dot(h, wu_ref[:, cols].astype(BF16), preferred_element_type=F32)
            act = (0.5 * (gate * jax.nn.sigmoid(gate) * up)).astype(BF16)
            part = jnp.dot(act, wd_ref[cols, :].astype(BF16), preferred_element_type=F32)
            if first and k == 0:
                o_ref[...] = x_ref[...] + part
            else:
                o_ref[...] += part

    pl.when(j == 0)(functools.partial(step, True))
    pl.when(j > 0)(functools.partial(step, False))


def _ffn(x, g, wg, wu, wd):
    t, d = x.shape
    dff = wg.shape[1]
    tm, tf = FFN_TM, FFN_TF
    assert dff // tf >= 2, "the next x tile is fetched from the second column step on"
    wbytes = wg.dtype.itemsize
    est = (2 * tm * d * 4) + tm * d * 4 + tm * d * 2 + 3 * d * tf * (2 * wbytes) + 3 * d * FFN_SUB * 2 \
        + 3 * tm * FFN_SUB * 4
    return pl.pallas_call(
        _ffn_kernel,
        grid=(t // tm, dff // tf),
        in_specs=[
            pl.BlockSpec(memory_space=pl.ANY),
            pl.BlockSpec((1, d), lambda i, j: (0, 0)),
            pl.BlockSpec((d, tf), lambda i, j: (0, j)),
            pl.BlockSpec((d, tf), lambda i, j: (0, j)),
            pl.BlockSpec((tf, d), lambda i, j: (j, 0)),
        ],
        out_specs=pl.BlockSpec((tm, d), lambda i, j: (i, 0)),
        out_shape=jax.ShapeDtypeStruct((t, d), F32),
        scratch_shapes=[pltpu.VMEM((tm, d), BF16), pltpu.VMEM((tm, d), F32),
                        pltpu.SemaphoreType.DMA(())],
        compiler_params=pltpu.CompilerParams(
            dimension_semantics=("arbitrary", "arbitrary"),
            vmem_limit_bytes=_vmem_limit(est)),
    )(x, g, wg, wu, wd)


def _proj_kernel(x_hbm, g_ref, w_ref, o_ref, h_ref, x_ref, x_sem):
    _stream_row_tile(x_hbm, x_ref, x_sem)

    @pl.when(pl.program_id(1) == 0)
    def _():
        h_ref[...] = _rmsnorm(x_ref[...], g_ref[...]).astype(BF16)

    h = h_ref[...]
    for k in range(w_ref.shape[1] // PROJ_SUB):
        cols = pl.ds(k * PROJ_SUB, PROJ_SUB)
        o_ref[:, cols] = jnp.dot(h, w_ref[:, cols].astype(BF16), preferred_element_type=F32)


def _in_proj(x, g, w):
    t, d = x.shape
    n = w.shape[1]
    tm, tn = PROJ_TM, PROJ_TN
    assert n // tn >= 2, "the next x tile is fetched from the second column step on"
    est = tm * d * 4 + tm * d * 2 + d * tn * 2 * w.dtype.itemsize + d * PROJ_SUB * 2 + 2 * tm * tn * 4
    return pl.pallas_call(
        _proj_kernel,
        grid=(t // tm, n // tn),
        in_specs=[
            pl.BlockSpec(memory_space=pl.ANY),
            pl.BlockSpec((1, d), lambda i, j: (0, 0)),
            pl.BlockSpec((d, tn), lambda i, j: (0, j)),
        ],
        out_specs=pl.BlockSpec((tm, tn), lambda i, j: (i, j)),
        out_shape=jax.ShapeDtypeStruct((t, n), F32),
        scratch_shapes=[pltpu.VMEM((tm, d), BF16), pltpu.VMEM((tm, d), F32),
                        pltpu.SemaphoreType.DMA(())],
        compiler_params=pltpu.CompilerParams(
            dimension_semantics=("arbitrary", "arbitrary"),
            vmem_limit_bytes=_vmem_limit(est)),
    )(x, g, w)


def _hgrn_unit(zq, zf, zv, zg, lb, gn, st_ref, lf_scr, p_scr, t_scr, unit):
    c, sub = HGRN_CHUNK, HGRN_SUB
    rr = zq.shape[0]
    nc = rr // c
    nv = c // SUBLANES
    assert c == 4 * sub and sub == 2 * SUBLANES
    shp4 = (nc, nv, SUBLANES, LANES)
    to4 = lambda x: x.reshape(shp4)
    f = lb + (1.0 - lb) * _sigmoid(zf)
    lf = jnp.log2(f)
    kk = to4(1.0 - f)
    q = to4(zq)

    nvr = rr // SUBLANES
    vreg_row = lambda r: pl.ds(r, nvr, stride=SUBLANES)
    lf_scr[...] = lf
    acc = lf_scr[vreg_row(0), :]
    p_scr[vreg_row(0), :] = acc
    for r in range(1, SUBLANES):
        acc = acc + lf_scr[vreg_row(r), :]
        p_scr[vreg_row(r), :] = acc
    for r in range(SUBLANES):
        t_scr[vreg_row(r), :] = acc
    p8 = to4(p_scr[...])
    t8 = to4(t_scr[...])
    s8 = t8 - p8
    tv = [t8[:, v] for v in range(nv)]
    pv = [p8[:, v] for v in range(nv)]
    sv = [s8[:, v] for v in range(nv)]
    p16 = [pv[v] if v % 2 == 0 else pv[v] + tv[v - 1] for v in range(nv)]
    s16 = [sv[v] + tv[v + 1] if v % 2 == 0 else sv[v] for v in range(nv)]
    t16 = [tv[2 * i] + tv[2 * i + 1] for i in range(nv // 2)]
    p32 = [p16[v] if (v // 2) % 2 == 0 else p16[v] + t16[v // 2 - 1] for v in range(nv)]
    s32 = [s16[v] + t16[v // 2 + 1] if (v // 2) % 2 == 0 else s16[v] for v in range(nv)]
    t32 = [t16[0] + t16[1], t16[2] + t16[3]]
    p64 = [p32[v] if v < nv // 2 else p32[v] + t32[0] for v in range(nv)]
    s64 = [s32[v] + t32[1] if v < nv // 2 else s32[v] for v in range(nv)]
    t64 = t32[0] + t32[1]

    st4 = lambda parts: jnp.stack(parts, axis=1)
    x32 = st4([s32[v] if v < nv // 2 else p32[v] for v in range(nv)])
    x16 = st4([s16[v] if (v // 2) % 2 == 0 else p16[v] for v in range(nv)])
    xd = st4([-sv[v] if v % 2 == 0 else pv[v] for v in range(nv)])
    qk32 = st4([kk[:, v] if v < nv // 2 else q[:, v] for v in range(nv)])
    qk16 = st4([kk[:, v] if (v // 2) % 2 == 0 else q[:, v] for v in range(nv)])

    to3 = lambda x: x.reshape(nc, c, LANES)
    m32 = to3(qk32 * jnp.exp2(x32)).astype(BF16)
    m16 = to3(qk16 * jnp.exp2(x16)).astype(BF16)
    qed = to3(q * jnp.exp2(xd)).astype(BF16)
    ked = to3(kk * jnp.exp2(-xd)).astype(BF16)
    qhat = to3(q * jnp.exp2(st4(p64))).astype(BF16)
    khat = to3(kk * jnp.exp2(st4(s64))).astype(BF16)
    dec = jnp.exp2(t64)
    vb = zv.reshape(nc, c, LANES).astype(BF16)

    zb = jnp.zeros((sub, LANES), BF16)
    blk = lambda x, i: x[i * sub:(i + 1) * sub]
    col = lambda parts: jnp.concatenate(parts, axis=0)
    ti = lax.broadcasted_iota(jnp.int32, (c, c), 0)
    si = lax.broadcasted_iota(jnp.int32, (c, c), 1)
    diag = ((ti // sub) == (si // sub)) & (si <= ti)
    nt = (((1,), (1,)), ((), ()))
    tn = (((0,), (0,)), ((), ()))

    scores = []
    for ci in range(nc):
        a32, a16 = m32[ci], m16[ci]
        q_off = jnp.concatenate([
            col([zb, zb, blk(a32, 2), blk(a32, 3)]),
            col([zb, blk(a16, 1), zb, zb]),
            col([zb, zb, zb, blk(a16, 3)])], axis=1)
        k_off = jnp.concatenate([
            col([blk(a32, 0), blk(a32, 1), zb, zb]),
            col([blk(a16, 0), zb, zb, zb]),
            col([zb, zb, blk(a16, 2), zb])], axis=1)
        s_off = lax.dot_general(q_off, k_off, nt, preferred_element_type=F32)
        s_dia = lax.dot_general(qed[ci], ked[ci], nt, preferred_element_type=F32)
        scores.append((s_off + jnp.where(diag, s_dia, 0.0)).astype(BF16))

    upd = [lax.dot_general(vb[ci], khat[ci], tn, preferred_element_type=F32)
           for ci in range(nc)]
    st = st_ref[unit]
    states = []
    for ci in range(nc):
        states.append(st.astype(BF16))
        st = dec[ci, 0:1, :] * st + upd[ci]
    st_ref[unit] = st

    outs = []
    for ci in range(nc):
        outs.append(lax.dot_general(qhat[ci], states[ci], nt, preferred_element_type=F32)
                    + jnp.dot(scores[ci], vb[ci], preferred_element_type=F32))
    o = jnp.concatenate(outs, axis=0)
    ms = jnp.mean(o * o, axis=-1, keepdims=True)
    return o * lax.rsqrt(ms + EPS) * gn * (zg * _sigmoid(zg))


def _scan_rows(a, b, carry):
    n = a.shape[0] // SUBLANES
    a3 = a.reshape(n, SUBLANES, LANES)
    b3 = b.reshape(n, SUBLANES, LANES)
    row = lax.broadcasted_iota(jnp.int32, a3.shape, 1)
    d = 1
    while d < SUBLANES:
        m = row >= d
        b3 = jnp.where(m, b3 + a3 * pltpu.roll(b3, d, axis=1), b3)
        a3 = jnp.where(m, a3 * pltpu.roll(a3, d, axis=1), a3)
        d *= 2
    es = []
    for s in range(n):
        e = b3[s] + a3[s] * carry
        es.append(e)
        carry = jnp.broadcast_to(e[SUBLANES - 1:SUBLANES, :], e.shape)
    return jnp.concatenate(es, axis=0), carry


def _lru_unit(zx, zgate, cw, cb, wax, ba, bx, lam, tail_ref, hc_ref, a_scr, u_scr, h_scr,
              x_scr, unit, slot):
    rr = zx.shape[0]
    nd = LRU_BLOCK_DIM
    x_scr[0:SUBLANES, :] = tail_ref[1 - slot, unit]
    x_scr[SUBLANES:, :] = zx
    xc = cb + cw[CONV_WIDTH - 1:CONV_WIDTH, :] * zx
    for j in range(CONV_WIDTH - 1):
        off = SUBLANES - (CONV_WIDTH - 1) + j
        xc = xc + cw[j:j + 1, :] * x_scr[pl.ds(off, rr), :]
    tail_ref[slot, unit] = zx[rr - SUBLANES:rr, :]

    nl = -lam
    sp = jnp.maximum(nl, 0.0) + jnp.log1p(jnp.exp(-jnp.abs(nl)))
    rx = jnp.dot(xc.astype(BF16), wax, preferred_element_type=F32)
    r = _sigmoid(rx[:, :nd] + ba)
    ig = _sigmoid(rx[:, nd:] + bx)
    log_a = r * ((-LRU_C) * sp)
    a = jnp.exp(log_a)
    u = jnp.sqrt(-jnp.tanh(log_a) * (a * a + 1.0)) * (ig * xc)

    ns = rr // SUBLANES
    a_scr[...] = a
    u_scr[...] = u
    slab_row = lambda ref, r: ref[pl.ds(r, ns, stride=SUBLANES), :]
    hz, az = [slab_row(u_scr, 0)], [slab_row(a_scr, 0)]
    for r in range(1, SUBLANES):
        ar = slab_row(a_scr, r)
        hz.append(ar * hz[-1] + slab_row(u_scr, r))
        az.append(ar * az[-1])
    carry0 = hc_ref[unit]
    ends, carry = _scan_rows(az[-1], hz[-1], carry0)
    hc_ref[unit] = carry
    first = lax.broadcasted_iota(jnp.int32, ends.shape, 0) == 0
    cin = jnp.where(first, carry0[0:1, :], pltpu.roll(ends, 1, axis=0))
    for r in range(SUBLANES):
        h_scr[pl.ds(r, ns, stride=SUBLANES), :] = hz[r] + az[r] * cin
    h = h_scr[...]
    return h * _gelu_tanh(zgate)


def _mixer_kernel(zq_ref, zf_ref, zv_ref, zg_ref, zx_ref, zgate_ref, lbr_ref, gn_ref,
                  cw_ref, cb_ref, wax_ref, ba_ref, bx_ref, lam_ref,
                  o_ref, st_ref, tail_ref, hc_ref, lf_scr, p_scr, t_scr, a_scr, u_scr, h_scr, x_scr,
                  *, tiles_per_seq):
    unit = pl.program_id(1)
    slot = pl.program_id(0) % 2

    @pl.when(pl.program_id(0) % tiles_per_seq == 0)
    def _():
        st_ref[unit] = jnp.zeros(st_ref.shape[1:], F32)
        tail_ref[1 - slot, unit] = jnp.zeros(tail_ref.shape[2:], F32)
        hc_ref[unit] = jnp.zeros(hc_ref.shape[1:], F32)

    lbr = lbr_ref[...]
    ex = jnp.exp(lbr - jnp.max(lbr, axis=0, keepdims=True))
    lb = ex[0:1, :] / jnp.sum(ex, axis=0, keepdims=True)
    oh = _hgrn_unit(zq_ref[...], zf_ref[...], zv_ref[...], zg_ref[...], lb, gn_ref[...],
                    st_ref, lf_scr, p_scr, t_scr, unit)
    o_ref[:, :LANES] = oh.astype(o_ref.dtype)
    ol = _lru_unit(zx_ref[...], zgate_ref[...], cw_ref[...], cb_ref[...], wax_ref[0],
                   ba_ref[...], bx_ref[...], lam_ref[...], tail_ref, hc_ref,
                   a_scr, u_scr, h_scr, x_scr, unit, slot)
    o_ref[:, LANES:] = ol.astype(o_ref.dtype)


def _mixer(z, lower_bounds, g_norm, conv_w, conv_b, wax, b_a, b_x, lam, seq):
    t = z.shape[0]
    ts = MIX_TS
    nu = HGRN_HEADS
    assert nu == LRU_BLOCKS and HEAD_DIM == LRU_BLOCK_DIM == LANES and seq % ts == 0
    est = 6 * 2 * ts * LANES * 4 + 2 * ts * 2 * LANES * 2 + 7 * ts * LANES * 4 + 4 * (1 << 20)
    est += t * (HGRN_WIDTH + LRU_WIDTH) * 2

    def zspec(part):
        return pl.BlockSpec((ts, LANES), lambda i, j, part=part: (i, part * nu + j))

    def vec(rows):
        return pl.BlockSpec((rows, LANES), lambda i, j: (0, j))

    return pl.pallas_call(
        functools.partial(_mixer_kernel, tiles_per_seq=seq // ts),
        grid=(t // ts, nu),
        in_specs=[zspec(0), zspec(1), zspec(2), zspec(3), zspec(4), zspec(5),
                  vec(lower_bounds.shape[0]),
                  pl.BlockSpec((1, HEAD_DIM), lambda i, j: (0, 0)),
                  vec(CONV_WIDTH), vec(1),
                  pl.BlockSpec((1,) + wax.shape[1:], lambda i, j: (j, 0, 0)),
                  vec(1), vec(1), vec(1)],
        out_specs=pl.BlockSpec((ts, 2 * LANES), lambda i, j: (i, j)),
        out_shape=jax.ShapeDtypeStruct((t, HGRN_WIDTH + LRU_WIDTH), BF16),
        scratch_shapes=[pltpu.VMEM((nu, HEAD_DIM, HEAD_DIM), F32),
                        pltpu.VMEM((2, nu, SUBLANES, LANES), F32),
                        pltpu.VMEM((nu, SUBLANES, LANES), F32)]
        + [pltpu.VMEM((ts, LANES), F32)] * 6 + [pltpu.VMEM((ts + SUBLANES, LANES), F32)],
        compiler_params=pltpu.CompilerParams(
            dimension_semantics=("arbitrary", "arbitrary"),
            vmem_limit_bytes=_vmem_limit(est)),
    )(z, z, z, z, z, z, lower_bounds, g_norm, conv_w, conv_b, wax, b_a, b_x, lam)


def _resident(shape):
    return pl.BlockSpec(shape, lambda i: (0,) * len(shape), pipeline_mode=pl.Buffered(1))


def _outproj_kernel(x_ref, o_ref_in, w_ref, out_ref, wb_ref):
    @pl.when(pl.program_id(0) == 0)
    def _():
        for u in range(HGRN_HEADS):
            wb_ref[pl.ds(2 * u * LANES, LANES), :] = (
                w_ref[pl.ds(u * LANES, LANES), :].astype(BF16))
            wb_ref[pl.ds((2 * u + 1) * LANES, LANES), :] = (
                w_ref[pl.ds(HGRN_WIDTH + u * LANES, LANES), :].astype(BF16))

    out_ref[...] = x_ref[...] + jnp.dot(o_ref_in[...], wb_ref[...],
                                        preferred_element_type=F32)


def _out_proj(x, o, w_out):
    t, d = x.shape
    tm = ROW_TM
    k = o.shape[1]
    est = 2 * 2 * tm * d * 4 + 2 * tm * k * 2 + k * d * (4 + 2)
    return pl.pallas_call(
        _outproj_kernel,
        grid=(t // tm,),
        in_specs=[
            pl.BlockSpec((tm, d), lambda i: (i, 0)),
            pl.BlockSpec((tm, k), lambda i: (i, 0)),
            _resident(w_out.shape),
        ],
        out_specs=pl.BlockSpec((tm, d), lambda i: (i, 0)),
        out_shape=jax.ShapeDtypeStruct((t, d), F32),
        scratch_shapes=[pltpu.VMEM(w_out.shape, BF16)],
        compiler_params=pltpu.CompilerParams(
            dimension_semantics=("arbitrary",),
            vmem_limit_bytes=_vmem_limit(est)),
    )(x, o, w_out)


def _ple_kernel(x_ref, p_ref, gp_ref, wg_ref, bg_ref, wp_ref, gf_ref, o_ref,
                wgb_ref, wpb_ref):
    @pl.when(pl.program_id(0) == 0)
    def _():
        wgb_ref[...] = wg_ref[...].astype(BF16)
        wpb_ref[...] = wp_ref[...].astype(BF16)

    x = x_ref[...]
    h = _rmsnorm(x, gp_ref[...]).astype(BF16)
    gate = jax.nn.sigmoid(
        jnp.dot(h, wgb_ref[...], preferred_element_type=F32) + bg_ref[...])
    emb = jnp.dot(p_ref[...].astype(BF16), wpb_ref[...], preferred_element_type=F32)
    o_ref[...] = _rmsnorm(x + gate * emb, gf_ref[...])


def _ple(x, p, g_ple, w_gate, b_gate, w_proj, g_final):
    t, d = x.shape
    pd = p.shape[1]
    tm = ROW_TM
    est = 2 * 2 * tm * d * 4 + 2 * tm * pd * 4 + (d + pd) * d * (4 + 2) + 2 * tm * d * 4
    row = pl.BlockSpec((1, d), lambda i: (0, 0))
    return pl.pallas_call(
        _ple_kernel,
        grid=(t // tm,),
        in_specs=[
            pl.BlockSpec((tm, d), lambda i: (i, 0)),
            pl.BlockSpec((tm, pd), lambda i: (i, 0)),
            row,
            _resident(w_gate.shape),
            row,
            _resident(w_proj.shape),
            row,
        ],
        out_specs=pl.BlockSpec((tm, d), lambda i: (i, 0)),
        out_shape=jax.ShapeDtypeStruct((t, d), F32),
        scratch_shapes=[pltpu.VMEM(w_gate.shape, BF16), pltpu.VMEM(w_proj.shape, BF16)],
        compiler_params=pltpu.CompilerParams(
            dimension_semantics=("arbitrary",),
            vmem_limit_bytes=_vmem_limit(est)),
    )(x, p, g_ple, w_gate, b_gate, w_proj, g_final)


def kernel(x, p, ffn1_norm, ffn1_w_gate, ffn1_w_up, ffn1_w_down, mix_norm, w_in,
           hgrn_lower_bounds, hgrn_g_norm, conv_w, conv_b, lru_w_a, lru_b_a, lru_w_x,
           lru_b_x, lru_lambda, w_out, ffn2_norm, ffn2_w_gate, ffn2_w_up, ffn2_w_down,
           ple_norm, ple_w_gate, ple_b_gate, ple_w_proj, final_norm):
    batch, seq, d = x.shape
    t = batch * seq
    depth = ffn1_norm.shape[0]
    assert depth == 1, "the shared lower-bound cumsum is specialised to one layer"
    l = 0
    bf = lambda a: a.astype(BF16)
    xt = x.reshape(t, d)

    xt = _ffn(xt, ffn1_norm[l][None], ffn1_w_gate[l], ffn1_w_up[l], ffn1_w_down[l])

    z = _in_proj(xt, mix_norm[l][None], w_in[l])
    wax = bf(jnp.concatenate([lru_w_a[l], lru_w_x[l]], axis=-1))
    mixed = _mixer(z, hgrn_lower_bounds, hgrn_g_norm[l][None], conv_w[l], conv_b[l][None], wax,
                   lru_b_a[l][None], lru_b_x[l][None], lru_lambda[l][None], seq)
    xt = _out_proj(xt, mixed, w_out[l])

    xt = _ffn(xt, ffn2_norm[l][None], ffn2_w_gate[l], ffn2_w_up[l], ffn2_w_down[l])

    out = _ple(xt, p[l].reshape(t, -1), ple_norm[l][None], ple_w_gate[l],
               ple_b_gate[l][None], ple_w_proj[l], final_norm[None])
    return out.reshape(batch, seq, d)
```

```python
import functools

import jax
import jax.numpy as jnp
from jax import lax
from jax.experimental import pallas as pl
from jax.experimental.pallas import tpu as pltpu

F32 = jnp.float32
BF16 = jnp.bfloat16

D_MODEL = 2048
D_FF = 5632
PLE_DIM = 256
HGRN_WIDTH = D_MODEL // 2
HEAD_DIM = 128
HGRN_HEADS = HGRN_WIDTH // HEAD_DIM
LRU_WIDTH = D_MODEL - HGRN_WIDTH
LRU_BLOCKS = 8
LRU_BLOCK_DIM = LRU_WIDTH // LRU_BLOCKS
CONV_WIDTH = 4
LRU_C = 8.0
EPS = 1e-6
LOG2_E = 1.4426950408889634
SQRT_2_OVER_PI = 0.7978845608028654

V7X_VMEM_BYTES = 64 * 1024 * 1024
SUBLANES = 8
LANES = 128

FFN_TM = 1024
FFN_TF = 512
FFN_SUB = 256
PROJ_TM = 2048
PROJ_TN = 768
PROJ_SUB = 256
ROW_TM = 512
HGRN_CHUNK = 64
HGRN_SUB = 16
MIX_TS = 2048


def _vmem_limit(nbytes):
    return int(min(V7X_VMEM_BYTES - (2 << 20), nbytes + nbytes // 4 + (2 << 20)))


def _sigmoid(x):
    return 1.0 / (1.0 + jnp.exp2(x * (-LOG2_E)))


def _gelu_tanh(x):
    k0 = -2.0 * SQRT_2_OVER_PI * LOG2_E
    return x / (1.0 + jnp.exp2(x * (k0 + (k0 * 0.044715) * (x * x))))


def _rmsnorm(xf, g):
    ms = jnp.mean(xf * xf, axis=-1, keepdims=True)
    return xf * lax.rsqrt(ms + EPS) * g


def _stream_row_tile(x_hbm, x_ref, x_sem):
    i, j = pl.program_id(0), pl.program_id(1)
    tm = x_ref.shape[0]

    def x_copy(tile):
        return pltpu.make_async_copy(x_hbm.at[pl.ds(tile * tm, tm), :], x_ref, x_sem)

    @pl.when((i == 0) & (j == 0))
    def _():
        x_copy(0).start()

    @pl.when(j == 0)
    def _():
        x_copy(i).wait()

    @pl.when((j == 1) & (i + 1 < pl.num_programs(0)))
    def _():
        x_copy(i + 1).start()


def _ffn_kernel(x_hbm, g_ref, wg_ref, wu_ref, wd_ref, o_ref, h_ref, x_ref, x_sem):
    j = pl.program_id(1)
    _stream_row_tile(x_hbm, x_ref, x_sem)

    def step(first):
        if first:
            h_ref[...] = _rmsnorm(x_ref[...], g_ref[...]).astype(BF16)
        h = h_ref[...]
        for k in range(wg_ref.shape[1] // FFN_SUB):
            cols = pl.ds(k * FFN_SUB, FFN_SUB)
            gate = jnp.dot(h, wg_ref[:, cols].astype(BF16), preferred_element_type=F32)
            up = jnp.dot(h, wu_ref[:, cols].astype(BF16), preferred_element_type=F32)
            act = (0.5 * (gate * jax.nn.sigmoid(gate) * up)).astype(BF16)
            part = jnp.dot(act, wd_ref[cols, :].astype(BF16), preferred_element_type=F32)
            if first and k == 0:
                o_ref[...] = x_ref[...] + part
            else:
                o_ref[...] += part

    pl.when(j == 0)(functools.partial(step, True))
    pl.when(j > 0)(functools.partial(step, False))


def _ffn(x, g, wg, wu, wd):
    t, d = x.shape
    dff = wg.shape[1]
    tm, tf = FFN_TM, FFN_TF
    assert dff // tf >= 2, "the next x tile is fetched from the second column step on"
    wbytes = wg.dtype.itemsize
    est = (2 * tm * d * 4) + tm * d * 4 + tm * d * 2 + 3 * d * tf * (2 * wbytes) + 3 * d * FFN_SUB * 2 \
        + 3 * tm * FFN_SUB * 4
    return pl.pallas_call(
        _ffn_kernel,
        grid=(t // tm, dff // tf),
        in_specs=[
            pl.BlockSpec(memory_space=pl.ANY),
            pl.BlockSpec((1, d), lambda i, j: (0, 0)),
            pl.BlockSpec((d, tf), lambda i, j: (0, j)),
            pl.BlockSpec((d, tf), lambda i, j: (0, j)),
            pl.BlockSpec((tf, d), lambda i, j: (j, 0)),
        ],
        out_specs=pl.BlockSpec((tm, d), lambda i, j: (i, 0)),
        out_shape=jax.ShapeDtypeStruct((t, d), F32),
        scratch_shapes=[pltpu.VMEM((tm, d), BF16), pltpu.VMEM((tm, d), F32),
                        pltpu.SemaphoreType.DMA(())],
        compiler_params=pltpu.CompilerParams(
            dimension_semantics=("arbitrary", "arbitrary"),
            vmem_limit_bytes=_vmem_limit(est)),
    )(x, g, wg, wu, wd)


def _proj_kernel(x_hbm, g_ref, w_ref, o_ref, h_ref, x_ref, x_sem):
    _stream_row_tile(x_hbm, x_ref, x_sem)

    def step(first):
        if first:
            h_ref[...] = _rmsnorm(x_ref[...], g_ref[...]).astype(BF16)
        h = h_ref[...]
        for k in range(w_ref.shape[1] // PROJ_SUB):
            cols = pl.ds(k * PROJ_SUB, PROJ_SUB)
            o_ref[:, cols] = jnp.dot(h, w_ref[:, cols].astype(BF16),
                                     preferred_element_type=F32)

    j = pl.program_id(1)
    pl.when(j == 0)(functools.partial(step, True))
    pl.when(j > 0)(functools.partial(step, False))


def _in_proj(x, g, w):
    t, d = x.shape
    n = w.shape[1]
    tm, tn = PROJ_TM, PROJ_TN
    assert n // tn >= 2, "the next x tile is fetched from the second column step on"
    est = tm * d * 4 + tm * d * 2 + d * tn * 2 * w.dtype.itemsize + d * PROJ_SUB * 2 + 2 * tm * tn * 4
    return pl.pallas_call(
        _proj_kernel,
        grid=(t // tm, n // tn),
        in_specs=[
            pl.BlockSpec(memory_space=pl.ANY),
            pl.BlockSpec((1, d), lambda i, j: (0, 0)),
            pl.BlockSpec((d, tn), lambda i, j: (0, j)),
        ],
        out_specs=pl.BlockSpec((tm, tn), lambda i, j: (i, j)),
        out_shape=jax.ShapeDtypeStruct((t, n), F32),
        scratch_shapes=[pltpu.VMEM((tm, d), BF16), pltpu.VMEM((tm, d), F32),
                        pltpu.SemaphoreType.DMA(())],
        compiler_params=pltpu.CompilerParams(
            dimension_semantics=("arbitrary", "arbitrary"),
            vmem_limit_bytes=_vmem_limit(est)),
    )(x, g, w)


def _hgrn_unit(zq, zf, zv, zg, lb, gn, st_ref, lf_scr, p_scr, t_scr, unit):
    c, sub = HGRN_CHUNK, HGRN_SUB
    rr = zq.shape[0]
    nc = rr // c
    nv = c // SUBLANES
    assert c == 4 * sub and sub == 2 * SUBLANES
    shp4 = (nc, nv, SUBLANES, LANES)
    to4 = lambda x: x.reshape(shp4)
    f = lb + (1.0 - lb) * _sigmoid(zf)
    lf = jnp.log2(f)
    kk = to4(1.0 - f)
    q = to4(zq)

    nvr = rr // SUBLANES
    vreg_row = lambda r: pl.ds(r, nvr, stride=SUBLANES)
    lf_scr[...] = lf
    acc = lf_scr[vreg_row(0), :]
    p_scr[vreg_row(0), :] = acc
    for r in range(1, SUBLANES):
        acc = acc + lf_scr[vreg_row(r), :]
        p_scr[vreg_row(r), :] = acc
    for r in range(SUBLANES):
        t_scr[vreg_row(r), :] = acc
    p8 = to4(p_scr[...])
    t8 = to4(t_scr[...])
    s8 = t8 - p8
    tv = [t8[:, v] for v in range(nv)]
    pv = [p8[:, v] for v in range(nv)]
    sv = [s8[:, v] for v in range(nv)]
    p16 = [pv[v] if v % 2 == 0 else pv[v] + tv[v - 1] for v in range(nv)]
    s16 = [sv[v] + tv[v + 1] if v % 2 == 0 else sv[v] for v in range(nv)]
    t16 = [tv[2 * i] + tv[2 * i + 1] for i in range(nv // 2)]
    p32 = [p16[v] if (v // 2) % 2 == 0 else p16[v] + t16[v // 2 - 1] for v in range(nv)]
    s32 = [s16[v] + t16[v // 2 + 1] if (v // 2) % 2 == 0 else s16[v] for v in range(nv)]
    t32 = [t16[0] + t16[1], t16[2] + t16[3]]
    p64 = [p32[v] if v < nv // 2 else p32[v] + t32[0] for v in range(nv)]
    s64 = [s32[v] + t32[1] if v < nv // 2 else s32[v] for v in range(nv)]
    t64 = t32[0] + t32[1]

    st4 = lambda parts: jnp.stack(parts, axis=1)
    x32 = st4([s32[v] if v < nv // 2 else p32[v] for v in range(nv)])
    x16 = st4([s16[v] if (v // 2) % 2 == 0 else p16[v] for v in range(nv)])
    xd = st4([-sv[v] if v % 2 == 0 else pv[v] for v in range(nv)])
    qk32 = st4([kk[:, v] if v < nv // 2 else q[:, v] for v in range(nv)])
    qk16 = st4([kk[:, v] if (v // 2) % 2 == 0 else q[:, v] for v in range(nv)])

    to3 = lambda x: x.reshape(nc, c, LANES)
    m32 = to3(qk32 * jnp.exp2(x32)).astype(BF16)
    m16 = to3(qk16 * jnp.exp2(x16)).astype(BF16)
    qed = to3(q * jnp.exp2(xd)).astype(BF16)
    ked = to3(kk * jnp.exp2(-xd)).astype(BF16)
    qhat = to3(q * jnp.exp2(st4(p64))).astype(BF16)
    khat = to3(kk * jnp.exp2(st4(s64))).astype(BF16)
    dec = jnp.exp2(t64)
    vb = zv.reshape(nc, c, LANES).astype(BF16)

    zb = jnp.zeros((sub, LANES), BF16)
    blk = lambda x, i: x[i * sub:(i + 1) * sub]
    col = lambda parts: jnp.concatenate(parts, axis=0)
    ti = lax.broadcasted_iota(jnp.int32, (c, c), 0)
    si = lax.broadcasted_iota(jnp.int32, (c, c), 1)
    diag = ((ti // sub) == (si // sub)) & (si <= ti)
    nt = (((1,), (1,)), ((), ()))
    tn = (((0,), (0,)), ((), ()))

    scores = []
    for ci in range(nc):
        a32, a16 = m32[ci], m16[ci]
        q_off = jnp.concatenate([
            col([zb, zb, blk(a32, 2), blk(a32, 3)]),
            col([zb, blk(a16, 1), zb, zb]),
            col([zb, zb, zb, blk(a16, 3)])], axis=1)
        k_off = jnp.concatenate([
            col([blk(a32, 0), blk(a32, 1), zb, zb]),
            col([blk(a16, 0), zb, zb, zb]),
            col([zb, zb, blk(a16, 2), zb])], axis=1)
        s_off = lax.dot_general(q_off, k_off, nt, preferred_element_type=F32)
        s_dia = lax.dot_general(qed[ci], ked[ci], nt, preferred_element_type=F32)
        scores.append((s_off + jnp.where(diag, s_dia, 0.0)).astype(BF16))

    upd = [lax.dot_general(vb[ci], khat[ci], tn, preferred_element_type=F32)
           for ci in range(nc)]
    st = st_ref[unit]
    states = []
    for ci in range(nc):
        states.append(st.astype(BF16))
        st = dec[ci, 0:1, :] * st + upd[ci]
    st_ref[unit] = st

    outs = []
    for ci in range(nc):
        outs.append(lax.dot_general(qhat[ci], states[ci], nt, preferred_element_type=F32)
                    + jnp.dot(scores[ci], vb[ci], preferred_element_type=F32))
    o = jnp.concatenate(outs, axis=0)
    ms = jnp.mean(o * o, axis=-1, keepdims=True)
    return o * lax.rsqrt(ms + EPS) * gn * (zg * _sigmoid(zg))


def _scan_rows(a, b, carry):
    n = a.shape[0] // SUBLANES
    a3 = a.reshape(n, SUBLANES, LANES)
    b3 = b.reshape(n, SUBLANES, LANES)
    row = lax.broadcasted_iota(jnp.int32, a3.shape, 1)
    d = 1
    while d < SUBLANES:
        m = row >= d
        b3 = jnp.where(m, b3 + a3 * pltpu.roll(b3, d, axis=1), b3)
        a3 = jnp.where(m, a3 * pltpu.roll(a3, d, axis=1), a3)
        d *= 2
    es = []
    for s in range(n):
        e = b3[s] + a3[s] * carry
        es.append(e)
        carry = jnp.broadcast_to(e[SUBLANES - 1:SUBLANES, :], e.shape)
    return jnp.concatenate(es, axis=0), carry


def _lru_unit(zx, zgate, cw, cb, wax, ba, bx, lam, tail_ref, hc_ref, a_scr, u_scr, h_scr,
              x_scr, unit, slot):
    rr = zx.shape[0]
    nd = LRU_BLOCK_DIM
    x_scr[0:SUBLANES, :] = tail_ref[1 - slot, unit]
    x_scr[SUBLANES:, :] = zx
    xc = cb + cw[CONV_WIDTH - 1:CONV_WIDTH, :] * zx
    for j in range(CONV_WIDTH - 1):
        off = SUBLANES - (CONV_WIDTH - 1) + j
        xc = xc + cw[j:j + 1, :] * x_scr[pl.ds(off, rr), :]
    tail_ref[slot, unit] = zx[rr - SUBLANES:rr, :]

    nl = -lam
    sp = jnp.maximum(nl, 0.0) + jnp.log1p(jnp.exp(-jnp.abs(nl)))
    rx = jnp.dot(xc.astype(BF16), wax, preferred_element_type=F32)
    r = _sigmoid(rx[:, :nd] + ba)
    ig = _sigmoid(rx[:, nd:] + bx)
    log_a = r * ((-LRU_C) * sp)
    a = jnp.exp(log_a)
    u = jnp.sqrt(-jnp.tanh(log_a) * (a * a + 1.0)) * (ig * xc)

    ns = rr // SUBLANES
    a_scr[...] = a
    u_scr[...] = u
    slab_row = lambda ref, r: ref[pl.ds(r, ns, stride=SUBLANES), :]
    hz, az = [slab_row(u_scr, 0)], [slab_row(a_scr, 0)]
    for r in range(1, SUBLANES):
        ar = slab_row(a_scr, r)
        hz.append(ar * hz[-1] + slab_row(u_scr, r))
        az.append(ar * az[-1])
    carry0 = hc_ref[unit]
    ends, carry = _scan_rows(az[-1], hz[-1], carry0)
    hc_ref[unit] = carry
    first = lax.broadcasted_iota(jnp.int32, ends.shape, 0) == 0
    cin = jnp.where(first, carry0[0:1, :], pltpu.roll(ends, 1, axis=0))
    for r in range(SUBLANES):
        h_scr[pl.ds(r, ns, stride=SUBLANES), :] = hz[r] + az[r] * cin
    h = h_scr[...]
    return h * _gelu_tanh(zgate)


def _mixer_kernel(zq_ref, zf_ref, zv_ref, zg_ref, zx_ref, zgate_ref, lbr_ref, gn_ref,
                  cw_ref, cb_ref, wax_ref, ba_ref, bx_ref, lam_ref,
                  o_ref, st_ref, tail_ref, hc_ref, lf_scr, p_scr, t_scr, a_scr, u_scr, h_scr, x_scr,
                  *, tiles_per_seq):
    unit = pl.program_id(1)
    slot = pl.program_id(0) % 2

    @pl.when(pl.program_id(0) % tiles_per_seq == 0)
    def _():
        st_ref[unit] = jnp.zeros(st_ref.shape[1:], F32)
        tail_ref[1 - slot, unit] = jnp.zeros(tail_ref.shape[2:], F32)
        hc_ref[unit] = jnp.zeros(hc_ref.shape[1:], F32)

    lbr = lbr_ref[...]
    ex = jnp.exp(lbr - jnp.max(lbr, axis=0, keepdims=True))
    lb = ex[0:1, :] / jnp.sum(ex, axis=0, keepdims=True)
    oh = _hgrn_unit(zq_ref[...], zf_ref[...], zv_ref[...], zg_ref[...], lb, gn_ref[...],
                    st_ref, lf_scr, p_scr, t_scr, unit)
    o_ref[:, :LANES] = oh.astype(o_ref.dtype)
    ol = _lru_unit(zx_ref[...], zgate_ref[...], cw_ref[...], cb_ref[...], wax_ref[0],
                   ba_ref[...], bx_ref[...], lam_ref[...], tail_ref, hc_ref,
                   a_scr, u_scr, h_scr, x_scr, unit, slot)
    o_ref[:, LANES:] = ol.astype(o_ref.dtype)


def _mixer(z, lower_bounds, g_norm, conv_w, conv_b, wax, b_a, b_x, lam, seq):
    t = z.shape[0]
    ts = MIX_TS
    nu = HGRN_HEADS
    assert nu == LRU_BLOCKS and HEAD_DIM == LRU_BLOCK_DIM == LANES and seq % ts == 0
    est = 6 * 2 * ts * LANES * 4 + 2 * ts * 2 * LANES * 2 + 7 * ts * LANES * 4 + 4 * (1 << 20)
    est += t * (HGRN_WIDTH + LRU_WIDTH) * 2

    def zspec(part):
        return pl.BlockSpec((ts, LANES), lambda i, j, part=part: (i, part * nu + j))

    def vec(rows):
        return pl.BlockSpec((rows, LANES), lambda i, j: (0, j))

    return pl.pallas_call(
        functools.partial(_mixer_kernel, tiles_per_seq=seq // ts),
        grid=(t // ts, nu),
        in_specs=[zspec(0), zspec(1), zspec(2), zspec(3), zspec(4), zspec(5),
                  vec(lower_bounds.shape[0]),
                  pl.BlockSpec((1, HEAD_DIM), lambda i, j: (0, 0)),
                  vec(CONV_WIDTH), vec(1),
                  pl.BlockSpec((1,) + wax.shape[1:], lambda i, j: (j, 0, 0)),
                  vec(1), vec(1), vec(1)],
        out_specs=pl.BlockSpec((ts, 2 * LANES), lambda i, j: (i, j)),
        out_shape=jax.ShapeDtypeStruct((t, HGRN_WIDTH + LRU_WIDTH), BF16),
        scratch_shapes=[pltpu.VMEM((nu, HEAD_DIM, HEAD_DIM), F32),
                        pltpu.VMEM((2, nu, SUBLANES, LANES), F32),
                        pltpu.VMEM((nu, SUBLANES, LANES), F32)]
        + [pltpu.VMEM((ts, LANES), F32)] * 6 + [pltpu.VMEM((ts + SUBLANES, LANES), F32)],
        compiler_params=pltpu.CompilerParams(
            dimension_semantics=("arbitrary", "arbitrary"),
            vmem_limit_bytes=_vmem_limit(est)),
    )(z, z, z, z, z, z, lower_bounds, g_norm, conv_w, conv_b, wax, b_a, b_x, lam)


def _resident(shape):
    return pl.BlockSpec(shape, lambda i: (0,) * len(shape), pipeline_mode=pl.Buffered(1))


def _outproj_kernel(x_ref, o_ref_in, w_ref, out_ref, wb_ref):
    @pl.when(pl.program_id(0) == 0)
    def _():
        for u in range(HGRN_HEADS):
            wb_ref[pl.ds(2 * u * LANES, LANES), :] = (
                w_ref[pl.ds(u * LANES, LANES), :].astype(BF16))
            wb_ref[pl.ds((2 * u + 1) * LANES, LANES), :] = (
                w_ref[pl.ds(HGRN_WIDTH + u * LANES, LANES), :].astype(BF16))

    out_ref[...] = x_ref[...] + jnp.dot(o_ref_in[...], wb_ref[...],
                                        preferred_element_type=F32)


def _out_proj(x, o, w_out):
    t, d = x.shape
    tm = ROW_TM
    k = o.shape[1]
    est = 2 * 2 * tm * d * 4 + 2 * tm * k * 2 + k * d * (4 + 2)
    return pl.pallas_call(
        _outproj_kernel,
        grid=(t // tm,),
        in_specs=[
            pl.BlockSpec((tm, d), lambda i: (i, 0)),
            pl.BlockSpec((tm, k), lambda i: (i, 0)),
            _resident(w_out.shape),
        ],
        out_specs=pl.BlockSpec((tm, d), lambda i: (i, 0)),
        out_shape=jax.ShapeDtypeStruct((t, d), F32),
        scratch_shapes=[pltpu.VMEM(w_out.shape, BF16)],
        compiler_params=pltpu.CompilerParams(
            dimension_semantics=("arbitrary",),
            vmem_limit_bytes=_vmem_limit(est)),
    )(x, o, w_out)


def _ple_kernel(x_ref, p_ref, gp_ref, wg_ref, bg_ref, wp_ref, gf_ref, o_ref,
                wgb_ref, wpb_ref):
    @pl.when(pl.program_id(0) == 0)
    def _():
        wgb_ref[...] = wg_ref[...].astype(BF16)
        wpb_ref[...] = wp_ref[...].astype(BF16)

    x = x_ref[...]
    h = _rmsnorm(x, gp_ref[...]).astype(BF16)
    gate = jax.nn.sigmoid(
        jnp.dot(h, wgb_ref[...], preferred_element_type=F32) + bg_ref[...])
    emb = jnp.dot(p_ref[...].astype(BF16), wpb_ref[...], preferred_element_type=F32)
    o_ref[...] = _rmsnorm(x + gate * emb, gf_ref[...])


def _ple(x, p, g_ple, w_gate, b_gate, w_proj, g_final):
    t, d = x.shape
    pd = p.shape[1]
    tm = ROW_TM
    est = 2 * 2 * tm * d * 4 + 2 * tm * pd * 4 + (d + pd) * d * (4 + 2) + 2 * tm * d * 4
    row = pl.BlockSpec((1, d), lambda i: (0, 0))
    return pl.pallas_call(
        _ple_kernel,
        grid=(t // tm,),
        in_specs=[
            pl.BlockSpec((tm, d), lambda i: (i, 0)),
            pl.BlockSpec((tm, pd), lambda i: (i, 0)),
            row,
            _resident(w_gate.shape),
            row,
            _resident(w_proj.shape),
            row,
        ],
        out_specs=pl.BlockSpec((tm, d), lambda i: (i, 0)),
        out_shape=jax.ShapeDtypeStruct((t, d), F32),
        scratch_shapes=[pltpu.VMEM(w_gate.shape, BF16), pltpu.VMEM(w_proj.shape, BF16)],
        compiler_params=pltpu.CompilerParams(
            dimension_semantics=("arbitrary",),
            vmem_limit_bytes=_vmem_limit(est)),
    )(x, p, g_ple, w_gate, b_gate, w_proj, g_final)


def kernel(x, p, ffn1_norm, ffn1_w_gate, ffn1_w_up, ffn1_w_down, mix_norm, w_in,
           hgrn_lower_bounds, hgrn_g_norm, conv_w, conv_b, lru_w_a, lru_b_a, lru_w_x,
           lru_b_x, lru_lambda, w_out, ffn2_norm, ffn2_w_gate, ffn2_w_up, ffn2_w_down,
           ple_norm, ple_w_gate, ple_b_gate, ple_w_proj, final_norm):
    batch, seq, d = x.shape
    t = batch * seq
    depth = ffn1_norm.shape[0]
    assert depth == 1, "the shared lower-bound cumsum is specialised to one layer"
    l = 0
    bf = lambda a: a.astype(BF16)
    xt = x.reshape(t, d)

    xt = _ffn(xt, ffn1_norm[l][None], ffn1_w_gate[l], ffn1_w_up[l], ffn1_w_down[l])

    z = _in_proj(xt, mix_norm[l][None], w_in[l])
    wax = bf(jnp.concatenate([lru_w_a[l], lru_w_x[l]], axis=-1))
    mixed = _mixer(z, hgrn_lower_bounds, hgrn_g_norm[l][None], conv_w[l], conv_b[l][None], wax,
                   lru_b_a[l][None], lru_b_x[l][None], lru_lambda[l][None], seq)
    xt = _out_proj(xt, mixed, w_out[l])

    xt = _ffn(xt, ffn2_norm[l][None], ffn2_w_gate[l], ffn2_w_up[l], ffn2_w_down[l])

    out = _ple(xt, p[l].reshape(t, -1), ple_norm[l][None], ple_w_gate[l],
               ple_b_gate[l][None], ple_w_proj[l], final_norm[None])
    return out.reshape(batch, seq, d)
```

```python
import functools

import jax
import jax.numpy as jnp
from jax import lax
from jax.experimental import pallas as pl
from jax.experimental.pallas import tpu as pltpu

F32 = jnp.float32
BF16 = jnp.bfloat16

D_MODEL = 2048
D_FF = 5632
PLE_DIM = 256
HGRN_WIDTH = D_MODEL // 2
HEAD_DIM = 128
HGRN_HEADS = HGRN_WIDTH // HEAD_DIM
LRU_WIDTH = D_MODEL - HGRN_WIDTH
LRU_BLOCKS = 8
LRU_BLOCK_DIM = LRU_WIDTH // LRU_BLOCKS
CONV_WIDTH = 4
LRU_C = 8.0
EPS = 1e-6
LOG2_E = 1.4426950408889634
SQRT_2_OVER_PI = 0.7978845608028654

V7X_VMEM_BYTES = 64 * 1024 * 1024
SUBLANES = 8
LANES = 128

FFN_TM = 1024
FFN_TF = 512
FFN_SUB = 256
PROJ_TM = 2048
PROJ_TN = 768
PROJ_SUB = 256
ROW_TM = 512
HGRN_CHUNK = 64
HGRN_SUB = 16
MIX_TS = 2048


def _vmem_limit(nbytes):
    return int(min(V7X_VMEM_BYTES - (2 << 20), nbytes + nbytes // 4 + (2 << 20)))


def _sigmoid(x):
    return 1.0 / (1.0 + jnp.exp2(x * (-LOG2_E)))


def _gelu_tanh(x):
    k0 = -2.0 * SQRT_2_OVER_PI * LOG2_E
    return x / (1.0 + jnp.exp2(x * (k0 + (k0 * 0.044715) * (x * x))))


def _rmsnorm(xf, g):
    ms = jnp.mean(xf * xf, axis=-1, keepdims=True)
    return xf * lax.rsqrt(ms + EPS) * g


def _stream_row_tile(x_hbm, x_ref, x_sem):
    i, j = pl.program_id(0), pl.program_id(1)
    tm = x_ref.shape[0]

    def x_copy(tile):
        return pltpu.make_async_copy(x_hbm.at[pl.ds(tile * tm, tm), :], x_ref, x_sem)

    @pl.when((i == 0) & (j == 0))
    def _():
        x_copy(0).start()

    @pl.when(j == 0)
    def _():
        x_copy(i).wait()

    @pl.when((j == 1) & (i + 1 < pl.num_programs(0)))
    def _():
        x_copy(i + 1).start()


def _ffn_kernel(x_hbm, g_ref, wg_ref, wu_ref, wd_ref, o_ref, h_ref, x_ref, x_sem):
    j = pl.program_id(1)
    _stream_row_tile(x_hbm, x_ref, x_sem)

    def step(first):
        if first:
            h_ref[...] = _rmsnorm(x_ref[...], g_ref[...]).astype(BF16)
        h = h_ref[...]
        for k in range(wg_ref.shape[1] // FFN_SUB):
            cols = pl.ds(k * FFN_SUB, FFN_SUB)
            gate = jnp.dot(h, wg_ref[:, cols].astype(BF16), preferred_element_type=F32)
            up = jnp.dot(h, wu_ref[:, cols].astype(BF16), preferred_element_type=F32)
            act = (0.5 * (gate * jax.nn.sigmoid(gate) * up)).astype(BF16)
            part = jnp.dot(act, wd_ref[cols, :].astype(BF16), preferred_element_type=F32)
            if first and k == 0:
                o_ref[...] = x_ref[...] + part
            else:
                o_ref[...] += part

    pl.when(j == 0)(functools.partial(step, True))
    pl.when(j > 0)(functools.partial(step, False))


def _ffn(x, g, wg, wu, wd):
    t, d = x.shape
    dff = wg.shape[1]
    tm, tf = FFN_TM, FFN_TF
    assert dff // tf >= 2, "the next x tile is fetched from the second column step on"
    wbytes = wg.dtype.itemsize
    est = (2 * tm * d * 4) + tm * d * 4 + tm * d * 2 + 3 * d * tf * (2 * wbytes) + 3 * d * FFN_SUB * 2 \
        + 3 * tm * FFN_SUB * 4
    return pl.pallas_call(
        _ffn_kernel,
        grid=(t // tm, dff // tf),
        in_specs=[
            pl.BlockSpec(memory_space=pl.ANY),
            pl.BlockSpec((1, d), lambda i, j: (0, 0)),
            pl.BlockSpec((d, tf), lambda i, j: (0, j)),
            pl.BlockSpec((d, tf), lambda i, j: (0, j)),
            pl.BlockSpec((tf, d), lambda i, j: (j, 0)),
        ],
        out_specs=pl.BlockSpec((tm, d), lambda i, j: (i, 0)),
        out_shape=jax.ShapeDtypeStruct((t, d), F32),
        scratch_shapes=[pltpu.VMEM((tm, d), BF16), pltpu.VMEM((tm, d), F32),
                        pltpu.SemaphoreType.DMA(())],
        compiler_params=pltpu.CompilerParams(
            dimension_semantics=("arbitrary", "arbitrary"),
            vmem_limit_bytes=_vmem_limit(est)),
    )(x, g, wg, wu, wd)


def _proj_kernel(x_hbm, g_ref, w_ref, o_ref, h_ref, x_ref, x_sem):
    _stream_row_tile(x_hbm, x_ref, x_sem)

    def step(first):
        if first:
            h_ref[...] = _rmsnorm(x_ref[...], g_ref[...]).astype(BF16)
        h = h_ref[...]
        for k in range(w_ref.shape[1] // PROJ_SUB):
            cols = pl.ds(k * PROJ_SUB, PROJ_SUB)
            o_ref[:, cols] = jnp.dot(h, w_ref[:, cols].astype(BF16),
                                     preferred_element_type=F32)

    j = pl.program_id(1)
    pl.when(j == 0)(functools.partial(step, True))
    pl.when(j > 0)(functools.partial(step, False))


def _in_proj(x, g, w):
    t, d = x.shape
    n = w.shape[1]
    tm, tn = PROJ_TM, PROJ_TN
    assert n // tn >= 2, "the next x tile is fetched from the second column step on"
    est = tm * d * 4 + tm * d * 2 + d * tn * 2 * w.dtype.itemsize + d * PROJ_SUB * 2 + 2 * tm * tn * 4
    return pl.pallas_call(
        _proj_kernel,
        grid=(t // tm, n // tn),
        in_specs=[
            pl.BlockSpec(memory_space=pl.ANY),
            pl.BlockSpec((1, d), lambda i, j: (0, 0)),
            pl.BlockSpec((d, tn), lambda i, j: (0, j)),
        ],
        out_specs=pl.BlockSpec((tm, tn), lambda i, j: (i, j)),
        out_shape=jax.ShapeDtypeStruct((t, n), F32),
        scratch_shapes=[pltpu.VMEM((tm, d), BF16), pltpu.VMEM((tm, d), F32),
                        pltpu.SemaphoreType.DMA(())],
        compiler_params=pltpu.CompilerParams(
            dimension_semantics=("arbitrary", "arbitrary"),
            vmem_limit_bytes=_vmem_limit(est)),
    )(x, g, w)


def _hgrn_unit(zq, zf, zv, zg, lb, gn, st_ref, lf_scr, p_scr, t_scr, unit):
    c, sub = HGRN_CHUNK, HGRN_SUB
    rr = zq.shape[0]
    nc = rr // c
    nv = c // SUBLANES
    assert c == 4 * sub and sub == 2 * SUBLANES
    shp4 = (nc, nv, SUBLANES, LANES)
    to4 = lambda x: x.reshape(shp4)
    f = lb + (1.0 - lb) * _sigmoid(zf)
    lf = jnp.log2(f)
    kk = to4(1.0 - f)
    q = to4(zq)

    nvr = rr // SUBLANES
    vreg_row = lambda r: pl.ds(r, nvr, stride=SUBLANES)
    lf_scr[...] = lf
    acc = lf_scr[vreg_row(0), :]
    p_scr[vreg_row(0), :] = acc
    for r in range(1, SUBLANES):
        acc = acc + lf_scr[vreg_row(r), :]
        p_scr[vreg_row(r), :] = acc
    for r in range(SUBLANES):
        t_scr[vreg_row(r), :] = acc
    p8 = to4(p_scr[...])
    t8 = to4(t_scr[...])
    s8 = t8 - p8
    tv = [t8[:, v] for v in range(nv)]
    pv = [p8[:, v] for v in range(nv)]
    sv = [s8[:, v] for v in range(nv)]
    p16 = [pv[v] if v % 2 == 0 else pv[v] + tv[v - 1] for v in range(nv)]
    s16 = [sv[v] + tv[v + 1] if v % 2 == 0 else sv[v] for v in range(nv)]
    t16 = [tv[2 * i] + tv[2 * i + 1] for i in range(nv // 2)]
    p32 = [p16[v] if (v // 2) % 2 == 0 else p16[v] + t16[v // 2 - 1] for v in range(nv)]
    s32 = [s16[v] + t16[v // 2 + 1] if (v // 2) % 2 == 0 else s16[v] for v in range(nv)]
    t32 = [t16[0] + t16[1], t16[2] + t16[3]]
    p64 = [p32[v] if v < nv // 2 else p32[v] + t32[0] for v in range(nv)]
    s64 = [s32[v] + t32[1] if v < nv // 2 else s32[v] for v in range(nv)]
    t64 = t32[0] + t32[1]

    st4 = lambda parts: jnp.stack(parts, axis=1)
    x32 = st4([s32[v] if v < nv // 2 else p32[v] for v in range(nv)])
    x16 = st4([s16[v] if (v // 2) % 2 == 0 else p16[v] for v in range(nv)])
    xd = st4([-sv[v] if v % 2 == 0 else pv[v] for v in range(nv)])
    qk32 = st4([kk[:, v] if v < nv // 2 else q[:, v] for v in range(nv)])
    qk16 = st4([kk[:, v] if (v // 2) % 2 == 0 else q[:, v] for v in range(nv)])

    to3 = lambda x: x.reshape(nc, c, LANES)
    m32 = to3(qk32 * jnp.exp2(x32)).astype(BF16)
    m16 = to3(qk16 * jnp.exp2(x16)).astype(BF16)
    qed = to3(q * jnp.exp2(xd)).astype(BF16)
    ked = to3(kk * jnp.exp2(-xd)).astype(BF16)
    qhat = to3(q * jnp.exp2(st4(p64))).astype(BF16)
    khat = to3(kk * jnp.exp2(st4(s64))).astype(BF16)
    dec = jnp.exp2(t64)
    vb = zv.reshape(nc, c, LANES).astype(BF16)

    zb = jnp.zeros((sub, LANES), BF16)
    blk = lambda x, i: x[i * sub:(i + 1) * sub]
    col = lambda parts: jnp.concatenate(parts, axis=0)
    ti = lax.broadcasted_iota(jnp.int32, (c, c), 0)
    si = lax.broadcasted_iota(jnp.int32, (c, c), 1)
    diag = ((ti // sub) == (si // sub)) & (si <= ti)
    nt = (((1,), (1,)), ((), ()))
    tn = (((0,), (0,)), ((), ()))

    scores = []
    for ci in range(nc):
        a32, a16 = m32[ci], m16[ci]
        q_off = jnp.concatenate([
            col([zb, zb, blk(a32, 2), blk(a32, 3)]),
            col([zb, blk(a16, 1), zb, zb]),
            col([zb, zb, zb, blk(a16, 3)])], axis=1)
        k_off = jnp.concatenate([
            col([blk(a32, 0), blk(a32, 1), zb, zb]),
            col([blk(a16, 0), zb, zb, zb]),
            col([zb, zb, blk(a16, 2), zb])], axis=1)
        s_off = lax.dot_general(q_off, k_off, nt, preferred_element_type=F32)
        s_dia = lax.dot_general(qed[ci], ked[ci], nt, preferred_element_type=F32)
        scores.append((s_off + jnp.where(diag, s_dia, 0.0)).astype(BF16))

    upd = [lax.dot_general(vb[ci], khat[ci], tn, preferred_element_type=F32)
           for ci in range(nc)]
    st = st_ref[unit]
    states = []
    for ci in range(nc):
        states.append(st.astype(BF16))
        st = dec[ci, 0:1, :] * st + upd[ci]
    st_ref[unit] = st

    outs = []
    for ci in range(nc):
        outs.append(lax.dot_general(qhat[ci], states[ci], nt, preferred_element_type=F32)
                    + jnp.dot(scores[ci], vb[ci], preferred_element_type=F32))
    o = jnp.concatenate(outs, axis=0)
    ms = jnp.mean(o * o, axis=-1, keepdims=True)
    return o * lax.rsqrt(ms + EPS) * gn * (zg * _sigmoid(zg))


def _scan_rows(a, b, carry):
    n = a.shape[0] // SUBLANES
    a3 = a.reshape(n, SUBLANES, LANES)
    b3 = b.reshape(n, SUBLANES, LANES)
    row = lax.broadcasted_iota(jnp.int32, a3.shape, 1)
    d = 1
    while d < SUBLANES:
        m = row >= d
        b3 = jnp.where(m, b3 + a3 * pltpu.roll(b3, d, axis=1), b3)
        a3 = jnp.where(m, a3 * pltpu.roll(a3, d, axis=1), a3)
        d *= 2
    es = []
    for s in range(n):
        e = b3[s] + a3[s] * carry
        es.append(e)
        carry = jnp.broadcast_to(e[SUBLANES - 1:SUBLANES, :], e.shape)
    return jnp.concatenate(es, axis=0), carry


def _lru_unit(zx, zgate, cw, cb, wax, ba, bx, lam, tail_ref, hc_ref, a_scr, u_scr, h_scr,
              x_scr, unit, slot):
    rr = zx.shape[0]
    nd = LRU_BLOCK_DIM
    x_scr[0:SUBLANES, :] = tail_ref[1 - slot, unit]
    x_scr[SUBLANES:, :] = zx
    xc = cb + cw[CONV_WIDTH - 1:CONV_WIDTH, :] * zx
    for j in range(CONV_WIDTH - 1):
        off = SUBLANES - (CONV_WIDTH - 1) + j
        xc = xc + cw[j:j + 1, :] * x_scr[pl.ds(off, rr), :]
    tail_ref[slot, unit] = zx[rr - SUBLANES:rr, :]

    nl = -lam
    sp = jnp.maximum(nl, 0.0) + jnp.log1p(jnp.exp(-jnp.abs(nl)))
    rx = jnp.dot(xc.astype(BF16), wax, preferred_element_type=F32)
    r = _sigmoid(rx[:, :nd] + ba)
    ig = _sigmoid(rx[:, nd:] + bx)
    log_a = r * ((-LRU_C) * sp)
    a = jnp.exp(log_a)
    u = jnp.sqrt(-jnp.tanh(log_a) * (a * a + 1.0)) * (ig * xc)

    ns = rr // SUBLANES
    a_scr[...] = a
    u_scr[...] = u
    slab_row = lambda ref, r: ref[pl.ds(r, ns, stride=SUBLANES), :]
    hz, az = [slab_row(u_scr, 0)], [slab_row(a_scr, 0)]
    for r in range(1, SUBLANES):
        ar = slab_row(a_scr, r)
        hz.append(ar * hz[-1] + slab_row(u_scr, r))
        az.append(ar * az[-1])
    carry0 = hc_ref[unit]
    ends, carry = _scan_rows(az[-1], hz[-1], carry0)
    hc_ref[unit] = carry
    first = lax.broadcasted_iota(jnp.int32, ends.shape, 0) == 0
    cin = jnp.where(first, carry0[0:1, :], pltpu.roll(ends, 1, axis=0))
    for r in range(SUBLANES):
        h_scr[pl.ds(r, ns, stride=SUBLANES), :] = hz[r] + az[r] * cin
    h = h_scr[...]
    return h * _gelu_tanh(zgate)


def _mixer_kernel(zq_ref, zf_ref, zv_ref, zg_ref, zx_ref, zgate_ref, lbr_ref, gn_ref,
                  cw_ref, cb_ref, wax_ref, ba_ref, bx_ref, lam_ref, s0_ref, s1_ref,
                  o_ref, c0_ref, c1_ref,
                  st_ref, tail_ref, hc_ref, lf_scr, p_scr, t_scr, a_scr, u_scr, h_scr, x_scr,
                  *, tiles_per_seq):
    unit = pl.program_id(1)
    slot = pl.program_id(0) % 2

    c0_ref[...] = s0_ref[...].astype(BF16)
    c1_ref[...] = s1_ref[...].astype(BF16)

    @pl.when(pl.program_id(0) % tiles_per_seq == 0)
    def _():
        st_ref[unit] = jnp.zeros(st_ref.shape[1:], F32)
        tail_ref[1 - slot, unit] = jnp.zeros(tail_ref.shape[2:], F32)
        hc_ref[unit] = jnp.zeros(hc_ref.shape[1:], F32)

    lbr = lbr_ref[...]
    ex = jnp.exp(lbr - jnp.max(lbr, axis=0, keepdims=True))
    lb = ex[0:1, :] / jnp.sum(ex, axis=0, keepdims=True)
    oh = _hgrn_unit(zq_ref[...], zf_ref[...], zv_ref[...], zg_ref[...], lb, gn_ref[...],
                    st_ref, lf_scr, p_scr, t_scr, unit)
    o_ref[:, :LANES] = oh.astype(o_ref.dtype)
    ol = _lru_unit(zx_ref[...], zgate_ref[...], cw_ref[...], cb_ref[...], wax_ref[0],
                   ba_ref[...], bx_ref[...], lam_ref[...], tail_ref, hc_ref,
                   a_scr, u_scr, h_scr, x_scr, unit, slot)
    o_ref[:, LANES:] = ol.astype(o_ref.dtype)


def _mixer(z, lower_bounds, g_norm, conv_w, conv_b, wax, b_a, b_x, lam, w_out, w_side, seq):
    t = z.shape[0]
    ts = MIX_TS
    nu = HGRN_HEADS
    assert nu == LRU_BLOCKS and HEAD_DIM == LRU_BLOCK_DIM == LANES and seq % ts == 0
    est = 6 * 2 * ts * LANES * 4 + 2 * ts * 2 * LANES * 2 + 7 * ts * LANES * 4 + 4 * (1 << 20)
    est += t * (HGRN_WIDTH + LRU_WIDTH) * 2

    def zspec(part):
        return pl.BlockSpec((ts, LANES), lambda i, j, part=part: (i, part * nu + j))

    def vec(rows):
        return pl.BlockSpec((rows, LANES), lambda i, j: (0, j))

    nsteps = (t // ts) * nu
    rb = w_out.shape[0] // nsteps
    per_group = LANES // rb
    assert rb * nsteps == w_out.shape[0] == w_side.shape[0] and per_group * rb == LANES

    def regrouped(i, j):
        dst = i * nu + j
        group, q = dst // per_group, dst % per_group
        src_group = jnp.where(group % 2 == 0, group // 2, HGRN_HEADS + group // 2)
        return (src_group * per_group + q, 0)

    straight = lambda i, j: (i * nu + j, 0)
    side_in = [pl.BlockSpec((rb, w_out.shape[1]), regrouped),
               pl.BlockSpec((rb, w_side.shape[1]), straight)]
    side_out = [pl.BlockSpec((rb, w_out.shape[1]), straight),
                pl.BlockSpec((rb, w_side.shape[1]), straight)]
    est += 2 * rb * (w_out.shape[1] + w_side.shape[1]) * (4 + 2)

    return pl.pallas_call(
        functools.partial(_mixer_kernel, tiles_per_seq=seq // ts),
        grid=(t // ts, nu),
        in_specs=[zspec(0), zspec(1), zspec(2), zspec(3), zspec(4), zspec(5),
                  vec(lower_bounds.shape[0]),
                  pl.BlockSpec((1, HEAD_DIM), lambda i, j: (0, 0)),
                  vec(CONV_WIDTH), vec(1),
                  pl.BlockSpec((1,) + wax.shape[1:], lambda i, j: (j, 0, 0)),
                  vec(1), vec(1), vec(1)] + side_in,
        out_specs=[pl.BlockSpec((ts, 2 * LANES), lambda i, j: (i, j))] + side_out,
        out_shape=[jax.ShapeDtypeStruct((t, HGRN_WIDTH + LRU_WIDTH), BF16),
                   jax.ShapeDtypeStruct(w_out.shape, BF16),
                   jax.ShapeDtypeStruct(w_side.shape, BF16)],
        scratch_shapes=[pltpu.VMEM((nu, HEAD_DIM, HEAD_DIM), F32),
                        pltpu.VMEM((2, nu, SUBLANES, LANES), F32),
                        pltpu.VMEM((nu, SUBLANES, LANES), F32)]
        + [pltpu.VMEM((ts, LANES), F32)] * 6 + [pltpu.VMEM((ts + SUBLANES, LANES), F32)],
        compiler_params=pltpu.CompilerParams(
            dimension_semantics=("arbitrary", "arbitrary"),
            vmem_limit_bytes=_vmem_limit(est)),
    )(z, z, z, z, z, z, lower_bounds, g_norm, conv_w, conv_b, wax, b_a, b_x, lam, w_out, w_side)


def _resident(shape):
    return pl.BlockSpec(shape, lambda i: (0,) * len(shape), pipeline_mode=pl.Buffered(1))


def _outproj_kernel(x_ref, o_ref_in, wb_ref, out_ref):
    out_ref[...] = x_ref[...] + jnp.dot(o_ref_in[...], wb_ref[...],
                                        preferred_element_type=F32)


def _out_proj(x, o, w_out):
    t, d = x.shape
    tm = ROW_TM
    k = o.shape[1]
    est = 2 * 2 * tm * d * 4 + 2 * tm * k * 2 + k * d * 2
    return pl.pallas_call(
        _outproj_kernel,
        grid=(t // tm,),
        in_specs=[
            pl.BlockSpec((tm, d), lambda i: (i, 0)),
            pl.BlockSpec((tm, k), lambda i: (i, 0)),
            _resident(w_out.shape),
        ],
        out_specs=pl.BlockSpec((tm, d), lambda i: (i, 0)),
        out_shape=jax.ShapeDtypeStruct((t, d), F32),
        compiler_params=pltpu.CompilerParams(
            dimension_semantics=("arbitrary",),
            vmem_limit_bytes=_vmem_limit(est)),
    )(x, o, w_out)


def _ple_kernel(x_ref, p_ref, gp_ref, wgb_ref, bg_ref, wp_ref, gf_ref, o_ref, wpb_ref):
    @pl.when(pl.program_id(0) == 0)
    def _():
        wpb_ref[...] = wp_ref[...].astype(BF16)

    x = x_ref[...]
    h = _rmsnorm(x, gp_ref[...]).astype(BF16)
    gate = jax.nn.sigmoid(
        jnp.dot(h, wgb_ref[...], preferred_element_type=F32) + bg_ref[...])
    emb = jnp.dot(p_ref[...].astype(BF16), wpb_ref[...], preferred_element_type=F32)
    o_ref[...] = _rmsnorm(x + gate * emb, gf_ref[...])


def _ple(x, p, g_ple, w_gate, b_gate, w_proj, g_final):
    t, d = x.shape
    pd = p.shape[1]
    tm = ROW_TM
    est = 2 * 2 * tm * d * 4 + 2 * tm * pd * 4 + d * d * 2 + pd * d * (4 + 2) + 2 * tm * d * 4
    row = pl.BlockSpec((1, d), lambda i: (0, 0))
    return pl.pallas_call(
        _ple_kernel,
        grid=(t // tm,),
        in_specs=[
            pl.BlockSpec((tm, d), lambda i: (i, 0)),
            pl.BlockSpec((tm, pd), lambda i: (i, 0)),
            row,
            _resident(w_gate.shape),
            row,
            _resident(w_proj.shape),
            row,
        ],
        out_specs=pl.BlockSpec((tm, d), lambda i: (i, 0)),
        out_shape=jax.ShapeDtypeStruct((t, d), F32),
        scratch_shapes=[pltpu.VMEM(w_proj.shape, BF16)],
        compiler_params=pltpu.CompilerParams(
            dimension_semantics=("arbitrary",),
            vmem_limit_bytes=_vmem_limit(est)),
    )(x, p, g_ple, w_gate, b_gate, w_proj, g_final)


def kernel(x, p, ffn1_norm, ffn1_w_gate, ffn1_w_up, ffn1_w_down, mix_norm, w_in,
           hgrn_lower_bounds, hgrn_g_norm, conv_w, conv_b, lru_w_a, lru_b_a, lru_w_x,
           lru_b_x, lru_lambda, w_out, ffn2_norm, ffn2_w_gate, ffn2_w_up, ffn2_w_down,
           ple_norm, ple_w_gate, ple_b_gate, ple_w_proj, final_norm):
    batch, seq, d = x.shape
    t = batch * seq
    depth = ffn1_norm.shape[0]
    assert depth == 1, "the shared lower-bound cumsum is specialised to one layer"
    l = 0
    bf = lambda a: a.astype(BF16)
    xt = x.reshape(t, d)

    xt = _ffn(xt, ffn1_norm[l][None], ffn1_w_gate[l], ffn1_w_up[l], ffn1_w_down[l])

    z = _in_proj(xt, mix_norm[l][None], w_in[l])
    wax = bf(jnp.concatenate([lru_w_a[l], lru_w_x[l]], axis=-1))
    mixed, w_out_b, w_gate_b = _mixer(
        z, hgrn_lower_bounds, hgrn_g_norm[l][None], conv_w[l], conv_b[l][None], wax,
        lru_b_a[l][None], lru_b_x[l][None], lru_lambda[l][None], w_out[l], ple_w_gate[l], seq)
    xt = _out_proj(xt, mixed, w_out_b)

    xt = _ffn(xt, ffn2_norm[l][None], ffn2_w_gate[l], ffn2_w_up[l], ffn2_w_down[l])

    out = _ple(xt, p[l].reshape(t, -1), ple_norm[l][None], w_gate_b,
               ple_b_gate[l][None], ple_w_proj[l], final_norm[None])
    return out.reshape(batch, seq, d)
```

```python
import functools

import jax
import jax.numpy as jnp
from jax import lax
from jax.experimental import pallas as pl
from jax.experimental.pallas import tpu as pltpu

F32 = jnp.float32
BF16 = jnp.bfloat16

D_MODEL = 2048
D_FF = 5632
PLE_DIM = 256
HGRN_WIDTH = D_MODEL // 2
HEAD_DIM = 128
HGRN_HEADS = HGRN_WIDTH // HEAD_DIM
LRU_WIDTH = D_MODEL - HGRN_WIDTH
LRU_BLOCKS = 8
LRU_BLOCK_DIM = LRU_WIDTH // LRU_BLOCKS
CONV_WIDTH = 4
LRU_C = 8.0
EPS = 1e-6
LOG2_E = 1.4426950408889634
SQRT_2_OVER_PI = 0.7978845608028654

V7X_VMEM_BYTES = 64 * 1024 * 1024
SUBLANES = 8
LANES = 128

FFN_TM = 1024
FFN_TF = 512
FFN_SUB = 256
PROJ_TM = 2048
PROJ_TN = 768
PROJ_SUB = 256
ROW_TM = 1024
HGRN_CHUNK = 64
HGRN_SUB = 16
MIX_TS = 2048


def _vmem_limit(nbytes):
    return int(min(V7X_VMEM_BYTES - (2 << 20), nbytes + nbytes // 4 + (2 << 20)))


def _sigmoid(x):
    return 1.0 / (1.0 + jnp.exp2(x * (-LOG2_E)))


def _gelu_tanh(x):
    k0 = -2.0 * SQRT_2_OVER_PI * LOG2_E
    return x / (1.0 + jnp.exp2(x * (k0 + (k0 * 0.044715) * (x * x))))


def _rmsnorm(xf, g):
    ms = jnp.mean(xf * xf, axis=-1, keepdims=True)
    return xf * lax.rsqrt(ms + EPS) * g


def _stream_row_tile(x_hbm, x_ref, x_sem):
    i, j = pl.program_id(0), pl.program_id(1)
    tm = x_ref.shape[0]

    def x_copy(tile):
        return pltpu.make_async_copy(x_hbm.at[pl.ds(tile * tm, tm), :], x_ref, x_sem)

    @pl.when((i == 0) & (j == 0))
    def _():
        x_copy(0).start()

    @pl.when(j == 0)
    def _():
        x_copy(i).wait()

    @pl.when((j == 1) & (i + 1 < pl.num_programs(0)))
    def _():
        x_copy(i + 1).start()


def _ffn_kernel(x_hbm, g_ref, wg_ref, wu_ref, wd_ref, o_ref, h_ref, x_ref, x_sem):
    j = pl.program_id(1)
    _stream_row_tile(x_hbm, x_ref, x_sem)

    def step(first):
        if first:
            h_ref[...] = _rmsnorm(x_ref[...], g_ref[...]).astype(BF16)
        h = h_ref[...]
        for k in range(wg_ref.shape[1] // FFN_SUB):
            cols = pl.ds(k * FFN_SUB, FFN_SUB)
            gate = jnp.dot(h, wg_ref[:, cols].astype(BF16), preferred_element_type=F32)
            up = jnp.dot(h, wu_ref[:, cols].astype(BF16), preferred_element_type=F32)
            act = (0.5 * (gate * jax.nn.sigmoid(gate) * up)).astype(BF16)
            part = jnp.dot(act, wd_ref[cols, :].astype(BF16), preferred_element_type=F32)
            if first and k == 0:
                o_ref[...] = x_ref[...] + part
            else:
                o_ref[...] += part

    pl.when(j == 0)(functools.partial(step, True))
    pl.when(j > 0)(functools.partial(step, False))


def _ffn(x, g, wg, wu, wd):
    t, d = x.shape
    dff = wg.shape[1]
    tm, tf = FFN_TM, FFN_TF
    assert dff // tf >= 2, "the next x tile is fetched from the second column step on"
    wbytes = wg.dtype.itemsize
    est = (2 * tm * d * 4) + tm * d * 4 + tm * d * 2 + 3 * d * tf * (2 * wbytes) + 3 * d * FFN_SUB * 2 \
        + 3 * tm * FFN_SUB * 4
    return pl.pallas_call(
        _ffn_kernel,
        grid=(t // tm, dff // tf),
        in_specs=[
            pl.BlockSpec(memory_space=pl.ANY),
            pl.BlockSpec((1, d), lambda i, j: (0, 0)),
            pl.BlockSpec((d, tf), lambda i, j: (0, j)),
            pl.BlockSpec((d, tf), lambda i, j: (0, j)),
            pl.BlockSpec((tf, d), lambda i, j: (j, 0)),
        ],
        out_specs=pl.BlockSpec((tm, d), lambda i, j: (i, 0)),
        out_shape=jax.ShapeDtypeStruct((t, d), F32),
        scratch_shapes=[pltpu.VMEM((tm, d), BF16), pltpu.VMEM((tm, d), F32),
                        pltpu.SemaphoreType.DMA(())],
        compiler_params=pltpu.CompilerParams(
            dimension_semantics=("arbitrary", "arbitrary"),
            vmem_limit_bytes=_vmem_limit(est)),
    )(x, g, wg, wu, wd)


def _proj_kernel(x_hbm, g_ref, w_ref, o_ref, h_ref, x_ref, x_sem):
    _stream_row_tile(x_hbm, x_ref, x_sem)

    def step(first):
        if first:
            h_ref[...] = _rmsnorm(x_ref[...], g_ref[...]).astype(BF16)
        h = h_ref[...]
        for k in range(w_ref.shape[1] // PROJ_SUB):
            cols = pl.ds(k * PROJ_SUB, PROJ_SUB)
            o_ref[:, cols] = jnp.dot(h, w_ref[:, cols].astype(BF16),
                                     preferred_element_type=F32)

    j = pl.program_id(1)
    pl.when(j == 0)(functools.partial(step, True))
    pl.when(j > 0)(functools.partial(step, False))


def _in_proj(x, g, w):
    t, d = x.shape
    n = w.shape[1]
    tm, tn = PROJ_TM, PROJ_TN
    assert n // tn >= 2, "the next x tile is fetched from the second column step on"
    est = tm * d * 4 + tm * d * 2 + d * tn * 2 * w.dtype.itemsize + d * PROJ_SUB * 2 + 2 * tm * tn * 4
    return pl.pallas_call(
        _proj_kernel,
        grid=(t // tm, n // tn),
        in_specs=[
            pl.BlockSpec(memory_space=pl.ANY),
            pl.BlockSpec((1, d), lambda i, j: (0, 0)),
            pl.BlockSpec((d, tn), lambda i, j: (0, j)),
        ],
        out_specs=pl.BlockSpec((tm, tn), lambda i, j: (i, j)),
        out_shape=jax.ShapeDtypeStruct((t, n), F32),
        scratch_shapes=[pltpu.VMEM((tm, d), BF16), pltpu.VMEM((tm, d), F32),
                        pltpu.SemaphoreType.DMA(())],
        compiler_params=pltpu.CompilerParams(
            dimension_semantics=("arbitrary", "arbitrary"),
            vmem_limit_bytes=_vmem_limit(est)),
    )(x, g, w)


def _hgrn_unit(zq, zf, zv, zg, lb, gn, st_ref, lf_scr, p_scr, t_scr, unit):
    c, sub = HGRN_CHUNK, HGRN_SUB
    rr = zq.shape[0]
    nc = rr // c
    nv = c // SUBLANES
    assert c == 4 * sub and sub == 2 * SUBLANES
    shp4 = (nc, nv, SUBLANES, LANES)
    to4 = lambda x: x.reshape(shp4)
    f = lb + (1.0 - lb) * _sigmoid(zf)
    lf = jnp.log2(f)
    kk = to4(1.0 - f)
    q = to4(zq)

    nvr = rr // SUBLANES
    vreg_row = lambda r: pl.ds(r, nvr, stride=SUBLANES)
    lf_scr[...] = lf
    acc = lf_scr[vreg_row(0), :]
    p_scr[vreg_row(0), :] = acc
    for r in range(1, SUBLANES):
        acc = acc + lf_scr[vreg_row(r), :]
        p_scr[vreg_row(r), :] = acc
    for r in range(SUBLANES):
        t_scr[vreg_row(r), :] = acc
    p8 = to4(p_scr[...])
    t8 = to4(t_scr[...])
    s8 = t8 - p8
    tv = [t8[:, v] for v in range(nv)]
    pv = [p8[:, v] for v in range(nv)]
    sv = [s8[:, v] for v in range(nv)]
    p16 = [pv[v] if v % 2 == 0 else pv[v] + tv[v - 1] for v in range(nv)]
    s16 = [sv[v] + tv[v + 1] if v % 2 == 0 else sv[v] for v in range(nv)]
    t16 = [tv[2 * i] + tv[2 * i + 1] for i in range(nv // 2)]
    p32 = [p16[v] if (v // 2) % 2 == 0 else p16[v] + t16[v // 2 - 1] for v in range(nv)]
    s32 = [s16[v] + t16[v // 2 + 1] if (v // 2) % 2 == 0 else s16[v] for v in range(nv)]
    t32 = [t16[0] + t16[1], t16[2] + t16[3]]
    p64 = [p32[v] if v < nv // 2 else p32[v] + t32[0] for v in range(nv)]
    s64 = [s32[v] + t32[1] if v < nv // 2 else s32[v] for v in range(nv)]
    t64 = t32[0] + t32[1]

    st4 = lambda parts: jnp.stack(parts, axis=1)
    x32 = st4([s32[v] if v < nv // 2 else p32[v] for v in range(nv)])
    x16 = st4([s16[v] if (v // 2) % 2 == 0 else p16[v] for v in range(nv)])
    xd = st4([-sv[v] if v % 2 == 0 else pv[v] for v in range(nv)])
    qk32 = st4([kk[:, v] if v < nv // 2 else q[:, v] for v in range(nv)])
    qk16 = st4([kk[:, v] if (v // 2) % 2 == 0 else q[:, v] for v in range(nv)])

    to3 = lambda x: x.reshape(nc, c, LANES)
    m32 = to3(qk32 * jnp.exp2(x32)).astype(BF16)
    m16 = to3(qk16 * jnp.exp2(x16)).astype(BF16)
    qed = to3(q * jnp.exp2(xd)).astype(BF16)
    ked = to3(kk * jnp.exp2(-xd)).astype(BF16)
    qhat = to3(q * jnp.exp2(st4(p64))).astype(BF16)
    khat = to3(kk * jnp.exp2(st4(s64))).astype(BF16)
    dec = jnp.exp2(t64)
    vb = zv.reshape(nc, c, LANES).astype(BF16)

    zb = jnp.zeros((sub, LANES), BF16)
    blk = lambda x, i: x[i * sub:(i + 1) * sub]
    col = lambda parts: jnp.concatenate(parts, axis=0)
    ti = lax.broadcasted_iota(jnp.int32, (c, c), 0)
    si = lax.broadcasted_iota(jnp.int32, (c, c), 1)
    diag = ((ti // sub) == (si // sub)) & (si <= ti)
    nt = (((1,), (1,)), ((), ()))
    tn = (((0,), (0,)), ((), ()))

    scores = []
    for ci in range(nc):
        a32, a16 = m32[ci], m16[ci]
        q_off = jnp.concatenate([
            col([zb, zb, blk(a32, 2), blk(a32, 3)]),
            col([zb, blk(a16, 1), zb, zb]),
            col([zb, zb, zb, blk(a16, 3)])], axis=1)
        k_off = jnp.concatenate([
            col([blk(a32, 0), blk(a32, 1), zb, zb]),
            col([blk(a16, 0), zb, zb, zb]),
            col([zb, zb, blk(a16, 2), zb])], axis=1)
        s_off = lax.dot_general(q_off, k_off, nt, preferred_element_type=F32)
        s_dia = lax.dot_general(qed[ci], ked[ci], nt, preferred_element_type=F32)
        scores.append((s_off + jnp.where(diag, s_dia, 0.0)).astype(BF16))

    upd = [lax.dot_general(vb[ci], khat[ci], tn, preferred_element_type=F32)
           for ci in range(nc)]
    st = st_ref[unit]
    states = []
    for ci in range(nc):
        states.append(st.astype(BF16))
        st = dec[ci, 0:1, :] * st + upd[ci]
    st_ref[unit] = st

    outs = []
    for ci in range(nc):
        outs.append(lax.dot_general(qhat[ci], states[ci], nt, preferred_element_type=F32)
                    + jnp.dot(scores[ci], vb[ci], preferred_element_type=F32))
    o = jnp.concatenate(outs, axis=0)
    ms = jnp.mean(o * o, axis=-1, keepdims=True)
    return o * lax.rsqrt(ms + EPS) * gn * (zg * _sigmoid(zg))


def _scan_rows(a, b, carry):
    n = a.shape[0] // SUBLANES
    a3 = a.reshape(n, SUBLANES, LANES)
    b3 = b.reshape(n, SUBLANES, LANES)
    row = lax.broadcasted_iota(jnp.int32, a3.shape, 1)
    d = 1
    while d < SUBLANES:
        m = row >= d
        b3 = jnp.where(m, b3 + a3 * pltpu.roll(b3, d, axis=1), b3)
        a3 = jnp.where(m, a3 * pltpu.roll(a3, d, axis=1), a3)
        d *= 2
    es = []
    for s in range(n):
        e = b3[s] + a3[s] * carry
        es.append(e)
        carry = jnp.broadcast_to(e[SUBLANES - 1:SUBLANES, :], e.shape)
    return jnp.concatenate(es, axis=0), carry


def _lru_unit(zx, zgate, cw, cb, wax, ba, bx, lam, tail_ref, hc_ref, a_scr, u_scr, h_scr,
              x_scr, unit, slot):
    rr = zx.shape[0]
    nd = LRU_BLOCK_DIM
    x_scr[0:SUBLANES, :] = tail_ref[1 - slot, unit]
    x_scr[SUBLANES:, :] = zx
    xc = cb + cw[CONV_WIDTH - 1:CONV_WIDTH, :] * zx
    for j in range(CONV_WIDTH - 1):
        off = SUBLANES - (CONV_WIDTH - 1) + j
        xc = xc + cw[j:j + 1, :] * x_scr[pl.ds(off, rr), :]
    tail_ref[slot, unit] = zx[rr - SUBLANES:rr, :]

    nl = -lam
    sp = jnp.maximum(nl, 0.0) + jnp.log1p(jnp.exp(-jnp.abs(nl)))
    rx = jnp.dot(xc.astype(BF16), wax, preferred_element_type=F32)
    r = _sigmoid(rx[:, :nd] + ba)
    ig = _sigmoid(rx[:, nd:] + bx)
    log_a = r * ((-LRU_C) * sp)
    a = jnp.exp(log_a)
    u = jnp.sqrt(-jnp.tanh(log_a) * (a * a + 1.0)) * (ig * xc)

    ns = rr // SUBLANES
    a_scr[...] = a
    u_scr[...] = u
    slab_row = lambda ref, r: ref[pl.ds(r, ns, stride=SUBLANES), :]
    hz, az = [slab_row(u_scr, 0)], [slab_row(a_scr, 0)]
    for r in range(1, SUBLANES):
        ar = slab_row(a_scr, r)
        hz.append(ar * hz[-1] + slab_row(u_scr, r))
        az.append(ar * az[-1])
    carry0 = hc_ref[unit]
    ends, carry = _scan_rows(az[-1], hz[-1], carry0)
    hc_ref[unit] = carry
    first = lax.broadcasted_iota(jnp.int32, ends.shape, 0) == 0
    cin = jnp.where(first, carry0[0:1, :], pltpu.roll(ends, 1, axis=0))
    for r in range(SUBLANES):
        h_scr[pl.ds(r, ns, stride=SUBLANES), :] = hz[r] + az[r] * cin
    h = h_scr[...]
    return h * _gelu_tanh(zgate)


def _mixer_kernel(zq_ref, zf_ref, zv_ref, zg_ref, zx_ref, zgate_ref, lbr_ref, gn_ref,
                  cw_ref, cb_ref, wax_ref, ba_ref, bx_ref, lam_ref, s0_ref, s1_ref,
                  o_ref, c0_ref, c1_ref,
                  st_ref, tail_ref, hc_ref, lf_scr, p_scr, t_scr, a_scr, u_scr, h_scr, x_scr,
                  *, tiles_per_seq):
    unit = pl.program_id(1)
    slot = pl.program_id(0) % 2

    c0_ref[...] = s0_ref[...].astype(BF16)
    c1_ref[...] = s1_ref[...].astype(BF16)

    @pl.when(pl.program_id(0) % tiles_per_seq == 0)
    def _():
        st_ref[unit] = jnp.zeros(st_ref.shape[1:], F32)
        tail_ref[1 - slot, unit] = jnp.zeros(tail_ref.shape[2:], F32)
        hc_ref[unit] = jnp.zeros(hc_ref.shape[1:], F32)

    lbr = lbr_ref[...]
    ex = jnp.exp(lbr - jnp.max(lbr, axis=0, keepdims=True))
    lb = ex[0:1, :] / jnp.sum(ex, axis=0, keepdims=True)
    oh = _hgrn_unit(zq_ref[...], zf_ref[...], zv_ref[...], zg_ref[...], lb, gn_ref[...],
                    st_ref, lf_scr, p_scr, t_scr, unit)
    o_ref[:, :LANES] = oh.astype(o_ref.dtype)
    ol = _lru_unit(zx_ref[...], zgate_ref[...], cw_ref[...], cb_ref[...], wax_ref[0],
                   ba_ref[...], bx_ref[...], lam_ref[...], tail_ref, hc_ref,
                   a_scr, u_scr, h_scr, x_scr, unit, slot)
    o_ref[:, LANES:] = ol.astype(o_ref.dtype)


def _mixer(z, lower_bounds, g_norm, conv_w, conv_b, wax, b_a, b_x, lam, w_out, w_side, seq):
    t = z.shape[0]
    ts = MIX_TS
    nu = HGRN_HEADS
    assert nu == LRU_BLOCKS and HEAD_DIM == LRU_BLOCK_DIM == LANES and seq % ts == 0
    est = 6 * 2 * ts * LANES * 4 + 2 * ts * 2 * LANES * 2 + 7 * ts * LANES * 4 + 4 * (1 << 20)
    est += t * (HGRN_WIDTH + LRU_WIDTH) * 2

    def zspec(part):
        return pl.BlockSpec((ts, LANES), lambda i, j, part=part: (i, part * nu + j))

    def vec(rows):
        return pl.BlockSpec((rows, LANES), lambda i, j: (0, j))

    nsteps = (t // ts) * nu
    rb = w_out.shape[0] // nsteps
    per_group = LANES // rb
    assert rb * nsteps == w_out.shape[0] == w_side.shape[0] and per_group * rb == LANES

    def regrouped(i, j):
        dst = i * nu + j
        group, q = dst // per_group, dst % per_group
        src_group = jnp.where(group % 2 == 0, group // 2, HGRN_HEADS + group // 2)
        return (src_group * per_group + q, 0)

    straight = lambda i, j: (i * nu + j, 0)
    side_in = [pl.BlockSpec((rb, w_out.shape[1]), regrouped),
               pl.BlockSpec((rb, w_side.shape[1]), straight)]
    side_out = [pl.BlockSpec((rb, w_out.shape[1]), straight),
                pl.BlockSpec((rb, w_side.shape[1]), straight)]
    est += 2 * rb * (w_out.shape[1] + w_side.shape[1]) * (4 + 2)

    return pl.pallas_call(
        functools.partial(_mixer_kernel, tiles_per_seq=seq // ts),
        grid=(t // ts, nu),
        in_specs=[zspec(0), zspec(1), zspec(2), zspec(3), zspec(4), zspec(5),
                  vec(lower_bounds.shape[0]),
                  pl.BlockSpec((1, HEAD_DIM), lambda i, j: (0, 0)),
                  vec(CONV_WIDTH), vec(1),
                  pl.BlockSpec((1,) + wax.shape[1:], lambda i, j: (j, 0, 0)),
                  vec(1), vec(1), vec(1)] + side_in,
        out_specs=[pl.BlockSpec((ts, 2 * LANES), lambda i, j: (i, j))] + side_out,
        out_shape=[jax.ShapeDtypeStruct((t, HGRN_WIDTH + LRU_WIDTH), BF16),
                   jax.ShapeDtypeStruct(w_out.shape, BF16),
                   jax.ShapeDtypeStruct(w_side.shape, BF16)],
        scratch_shapes=[pltpu.VMEM((nu, HEAD_DIM, HEAD_DIM), F32),
                        pltpu.VMEM((2, nu, SUBLANES, LANES), F32),
                        pltpu.VMEM((nu, SUBLANES, LANES), F32)]
        + [pltpu.VMEM((ts, LANES), F32)] * 6 + [pltpu.VMEM((ts + SUBLANES, LANES), F32)],
        compiler_params=pltpu.CompilerParams(
            dimension_semantics=("arbitrary", "arbitrary"),
            vmem_limit_bytes=_vmem_limit(est)),
    )(z, z, z, z, z, z, lower_bounds, g_norm, conv_w, conv_b, wax, b_a, b_x, lam, w_out, w_side)


def _resident(shape):
    return pl.BlockSpec(shape, lambda i: (0,) * len(shape), pipeline_mode=pl.Buffered(1))


def _outproj_kernel(x_ref, o_ref_in, wb_ref, out_ref):
    out_ref[...] = x_ref[...] + jnp.dot(o_ref_in[...], wb_ref[...],
                                        preferred_element_type=F32)


def _out_proj(x, o, w_out):
    t, d = x.shape
    tm = ROW_TM
    k = o.shape[1]
    est = 2 * 2 * tm * d * 4 + 2 * tm * k * 2 + k * d * 2
    return pl.pallas_call(
        _outproj_kernel,
        grid=(t // tm,),
        in_specs=[
            pl.BlockSpec((tm, d), lambda i: (i, 0)),
            pl.BlockSpec((tm, k), lambda i: (i, 0)),
            _resident(w_out.shape),
        ],
        out_specs=pl.BlockSpec((tm, d), lambda i: (i, 0)),
        out_shape=jax.ShapeDtypeStruct((t, d), F32),
        compiler_params=pltpu.CompilerParams(
            dimension_semantics=("arbitrary",),
            vmem_limit_bytes=_vmem_limit(est)),
    )(x, o, w_out)


def _ple_kernel(x_ref, p_ref, gp_ref, wgb_ref, bg_ref, wp_ref, gf_ref, o_ref, wpb_ref):
    @pl.when(pl.program_id(0) == 0)
    def _():
        wpb_ref[...] = wp_ref[...].astype(BF16)

    x = x_ref[...]
    h = _rmsnorm(x, gp_ref[...]).astype(BF16)
    gate = jax.nn.sigmoid(
        jnp.dot(h, wgb_ref[...], preferred_element_type=F32) + bg_ref[...])
    emb = jnp.dot(p_ref[...].astype(BF16), wpb_ref[...], preferred_element_type=F32)
    o_ref[...] = _rmsnorm(x + gate * emb, gf_ref[...])


def _ple(x, p, g_ple, w_gate, b_gate, w_proj, g_final):
    t, d = x.shape
    pd = p.shape[1]
    tm = ROW_TM
    est = 2 * 2 * tm * d * 4 + 2 * tm * pd * 4 + d * d * 2 + pd * d * (4 + 2) + 2 * tm * d * 4
    row = pl.BlockSpec((1, d), lambda i: (0, 0))
    return pl.pallas_call(
        _ple_kernel,
        grid=(t // tm,),
        in_specs=[
            pl.BlockSpec((tm, d), lambda i: (i, 0)),
            pl.BlockSpec((tm, pd), lambda i: (i, 0)),
            row,
            _resident(w_gate.shape),
            row,
            _resident(w_proj.shape),
            row,
        ],
        out_specs=pl.BlockSpec((tm, d), lambda i: (i, 0)),
        out_shape=jax.ShapeDtypeStruct((t, d), F32),
        scratch_shapes=[pltpu.VMEM(w_proj.shape, BF16)],
        compiler_params=pltpu.CompilerParams(
            dimension_semantics=("arbitrary",),
            vmem_limit_bytes=_vmem_limit(est)),
    )(x, p, g_ple, w_gate, b_gate, w_proj, g_final)


def kernel(x, p, ffn1_norm, ffn1_w_gate, ffn1_w_up, ffn1_w_down, mix_norm, w_in,
           hgrn_lower_bounds, hgrn_g_norm, conv_w, conv_b, lru_w_a, lru_b_a, lru_w_x,
           lru_b_x, lru_lambda, w_out, ffn2_norm, ffn2_w_gate, ffn2_w_up, ffn2_w_down,
           ple_norm, ple_w_gate, ple_b_gate, ple_w_proj, final_norm):
    batch, seq, d = x.shape
    t = batch * seq
    depth = ffn1_norm.shape[0]
    assert depth == 1, "the shared lower-bound cumsum is specialised to one layer"
    l = 0
    bf = lambda a: a.astype(BF16)
    xt = x.reshape(t, d)

    xt = _ffn(xt, ffn1_norm[l][None], ffn1_w_gate[l], ffn1_w_up[l], ffn1_w_down[l])

    z = _in_proj(xt, mix_norm[l][None], w_in[l])
    wax = bf(jnp.concatenate([lru_w_a[l], lru_w_x[l]], axis=-1))
    mixed, w_out_b, w_gate_b = _mixer(
        z, hgrn_lower_bounds, hgrn_g_norm[l][None], conv_w[l], conv_b[l][None], wax,
        lru_b_a[l][None], lru_b_x[l][None], lru_lambda[l][None], w_out[l], ple_w_gate[l], seq)
    xt = _out_proj(xt, mixed, w_out_b)

    xt = _ffn(xt, ffn2_norm[l][None], ffn2_w_gate[l], ffn2_w_up[l], ffn2_w_down[l])

    out = _ple(xt, p[l].reshape(t, -1), ple_norm[l][None], w_gate_b,
               ple_b_gate[l][None], ple_w_proj[l], final_norm[None])
    return out.reshape(batch, seq, d)
```

```python
import functools

import jax
import jax.numpy as jnp
from jax import lax
from jax.experimental import pallas as pl
from jax.experimental.pallas import tpu as pltpu

F32 = jnp.float32
BF16 = jnp.bfloat16

D_MODEL = 2048
D_FF = 5632
PLE_DIM = 256
HGRN_WIDTH = D_MODEL // 2
HEAD_DIM = 128
HGRN_HEADS = HGRN_WIDTH // HEAD_DIM
LRU_WIDTH = D_MODEL - HGRN_WIDTH
LRU_BLOCKS = 8
LRU_BLOCK_DIM = LRU_WIDTH // LRU_BLOCKS
CONV_WIDTH = 4
LRU_C = 8.0
EPS = 1e-6
LOG2_E = 1.4426950408889634
SQRT_2_OVER_PI = 0.7978845608028654

V7X_VMEM_BYTES = 64 * 1024 * 1024
SUBLANES = 8
LANES = 128

FFN_TM = 1024
FFN_TF = 512
FFN_SUB = 256
PROJ_TM = 2048
PROJ_TN = 768
PROJ_SUB = 256
ROW_TM = 512
HGRN_CHUNK = 64
HGRN_SUB = 16
MIX_TS = 2048


def _vmem_limit(nbytes):
    return int(min(V7X_VMEM_BYTES - (2 << 20), nbytes + nbytes // 4 + (2 << 20)))


def _sigmoid(x):
    return 1.0 / (1.0 + jnp.exp2(x * (-LOG2_E)))


def _gelu_tanh(x):
    k0 = -2.0 * SQRT_2_OVER_PI * LOG2_E
    return x / (1.0 + jnp.exp2(x * (k0 + (k0 * 0.044715) * (x * x))))


def _rmsnorm(xf, g):
    ms = jnp.mean(xf * xf, axis=-1, keepdims=True)
    return xf * lax.rsqrt(ms + EPS) * g


def _stream_row_tile(x_hbm, x_ref, x_sem):
    i, j = pl.program_id(0), pl.program_id(1)
    tm = x_ref.shape[0]

    def x_copy(tile):
        return pltpu.make_async_copy(x_hbm.at[pl.ds(tile * tm, tm), :], x_ref, x_sem)

    @pl.when((i == 0) & (j == 0))
    def _():
        x_copy(0).start()

    @pl.when(j == 0)
    def _():
        x_copy(i).wait()

    @pl.when((j == pl.num_programs(1) - 2) & (i + 1 < pl.num_programs(0)))
    def _():
        x_copy(i + 1).start()


def _ffn_kernel(x_hbm, g_ref, wg_ref, wu_ref, wd_ref, o_ref, h_ref, x_ref, x_sem):
    j = pl.program_id(1)
    _stream_row_tile(x_hbm, x_ref, x_sem)

    def step(first):
        if first:
            h_ref[...] = _rmsnorm(x_ref[...], g_ref[...]).astype(BF16)
        h = h_ref[...]
        for k in range(wg_ref.shape[1] // FFN_SUB):
            cols = pl.ds(k * FFN_SUB, FFN_SUB)
            gate = jnp.dot(h, wg_ref[:, cols].astype(BF16), preferred_element_type=F32)
            up = jnp.dot(h, wu_ref[:, cols].astype(BF16), preferred_element_type=F32)
            act = (0.5 * (gate * jax.nn.sigmoid(gate) * up)).astype(BF16)
            part = jnp.dot(act, wd_ref[cols, :].astype(BF16), preferred_element_type=F32)
            if first and k == 0:
                o_ref[...] = x_ref[...] + part
            else:
                o_ref[...] += part

    pl.when(j == 0)(functools.partial(step, True))
    pl.when(j > 0)(functools.partial(step, False))


def _ffn(x, g, wg, wu, wd):
    t, d = x.shape
    dff = wg.shape[1]
    tm, tf = FFN_TM, FFN_TF
    assert dff // tf >= 3, "the next x tile is fetched in the second-to-last column step"
    wbytes = wg.dtype.itemsize
    est = (2 * tm * d * 4) + tm * d * 4 + tm * d * 2 + 3 * d * tf * (2 * wbytes) + 3 * d * FFN_SUB * 2 \
        + 3 * tm * FFN_SUB * 4
    return pl.pallas_call(
        _ffn_kernel,
        grid=(t // tm, dff // tf),
        in_specs=[
            pl.BlockSpec(memory_space=pl.ANY),
            pl.BlockSpec((1, d), lambda i, j: (0, 0)),
            pl.BlockSpec((d, tf), lambda i, j: (0, j)),
            pl.BlockSpec((d, tf), lambda i, j: (0, j)),
            pl.BlockSpec((tf, d), lambda i, j: (j, 0)),
        ],
        out_specs=pl.BlockSpec((tm, d), lambda i, j: (i, 0)),
        out_shape=jax.ShapeDtypeStruct((t, d), F32),
        scratch_shapes=[pltpu.VMEM((tm, d), BF16), pltpu.VMEM((tm, d), F32),
                        pltpu.SemaphoreType.DMA(())],
        compiler_params=pltpu.CompilerParams(
            dimension_semantics=("arbitrary", "arbitrary"),
            vmem_limit_bytes=_vmem_limit(est)),
    )(x, g, wg, wu, wd)


def _proj_kernel(x_hbm, g_ref, w_ref, o_ref, h_ref, x_ref, x_sem):
    _stream_row_tile(x_hbm, x_ref, x_sem)

    def step(first):
        if first:
            h_ref[...] = _rmsnorm(x_ref[...], g_ref[...]).astype(BF16)
        h = h_ref[...]
        for k in range(w_ref.shape[1] // PROJ_SUB):
            cols = pl.ds(k * PROJ_SUB, PROJ_SUB)
            o_ref[:, cols] = jnp.dot(h, w_ref[:, cols].astype(BF16),
                                     preferred_element_type=F32)

    j = pl.program_id(1)
    pl.when(j == 0)(functools.partial(step, True))
    pl.when(j > 0)(functools.partial(step, False))


def _in_proj(x, g, w):
    t, d = x.shape
    n = w.shape[1]
    tm, tn = PROJ_TM, PROJ_TN
    assert n // tn >= 3, "the next x tile is fetched in the second-to-last column step"
    est = tm * d * 4 + tm * d * 2 + d * tn * 2 * w.dtype.itemsize + d * PROJ_SUB * 2 + 2 * tm * tn * 4
    return pl.pallas_call(
        _proj_kernel,
        grid=(t // tm, n // tn),
        in_specs=[
            pl.BlockSpec(memory_space=pl.ANY),
            pl.BlockSpec((1, d), lambda i, j: (0, 0)),
            pl.BlockSpec((d, tn), lambda i, j: (0, j)),
        ],
        out_specs=pl.BlockSpec((tm, tn), lambda i, j: (i, j)),
        out_shape=jax.ShapeDtypeStruct((t, n), F32),
        scratch_shapes=[pltpu.VMEM((tm, d), BF16), pltpu.VMEM((tm, d), F32),
                        pltpu.SemaphoreType.DMA(())],
        compiler_params=pltpu.CompilerParams(
            dimension_semantics=("arbitrary", "arbitrary"),
            vmem_limit_bytes=_vmem_limit(est)),
    )(x, g, w)


def _hgrn_unit(zq, zf, zv, zg, lb, gn, st_ref, lf_scr, p_scr, t_scr, unit):
    c, sub = HGRN_CHUNK, HGRN_SUB
    rr = zq.shape[0]
    nc = rr // c
    nv = c // SUBLANES
    assert c == 4 * sub and sub == 2 * SUBLANES
    shp4 = (nc, nv, SUBLANES, LANES)
    to4 = lambda x: x.reshape(shp4)
    f = lb + (1.0 - lb) * _sigmoid(zf)
    lf = jnp.log2(f)
    kk = to4(1.0 - f)
    q = to4(zq)

    nvr = rr // SUBLANES
    vreg_row = lambda r: pl.ds(r, nvr, stride=SUBLANES)
    lf_scr[...] = lf
    acc = lf_scr[vreg_row(0), :]
    p_scr[vreg_row(0), :] = acc
    for r in range(1, SUBLANES):
        acc = acc + lf_scr[vreg_row(r), :]
        p_scr[vreg_row(r), :] = acc
    for r in range(SUBLANES):
        t_scr[vreg_row(r), :] = acc
    p8 = to4(p_scr[...])
    t8 = to4(t_scr[...])
    s8 = t8 - p8
    tv = [t8[:, v] for v in range(nv)]
    pv = [p8[:, v] for v in range(nv)]
    sv = [s8[:, v] for v in range(nv)]
    p16 = [pv[v] if v % 2 == 0 else pv[v] + tv[v - 1] for v in range(nv)]
    s16 = [sv[v] + tv[v + 1] if v % 2 == 0 else sv[v] for v in range(nv)]
    t16 = [tv[2 * i] + tv[2 * i + 1] for i in range(nv // 2)]
    p32 = [p16[v] if (v // 2) % 2 == 0 else p16[v] + t16[v // 2 - 1] for v in range(nv)]
    s32 = [s16[v] + t16[v // 2 + 1] if (v // 2) % 2 == 0 else s16[v] for v in range(nv)]
    t32 = [t16[0] + t16[1], t16[2] + t16[3]]
    p64 = [p32[v] if v < nv // 2 else p32[v] + t32[0] for v in range(nv)]
    s64 = [s32[v] + t32[1] if v < nv // 2 else s32[v] for v in range(nv)]
    t64 = t32[0] + t32[1]

    st4 = lambda parts: jnp.stack(parts, axis=1)
    x32 = st4([s32[v] if v < nv // 2 else p32[v] for v in range(nv)])
    x16 = st4([s16[v] if (v // 2) % 2 == 0 else p16[v] for v in range(nv)])
    xd = st4([-sv[v] if v % 2 == 0 else pv[v] for v in range(nv)])
    qk32 = st4([kk[:, v] if v < nv // 2 else q[:, v] for v in range(nv)])
    qk16 = st4([kk[:, v] if (v // 2) % 2 == 0 else q[:, v] for v in range(nv)])

    to3 = lambda x: x.reshape(nc, c, LANES)
    m32 = to3(qk32 * jnp.exp2(x32)).astype(BF16)
    m16 = to3(qk16 * jnp.exp2(x16)).astype(BF16)
    qed = to3(q * jnp.exp2(xd)).astype(BF16)
    ked = to3(kk * jnp.exp2(-xd)).astype(BF16)
    qhat = to3(q * jnp.exp2(st4(p64))).astype(BF16)
    khat = to3(kk * jnp.exp2(st4(s64))).astype(BF16)
    dec = jnp.exp2(t64)
    vb = zv.reshape(nc, c, LANES).astype(BF16)

    zb = jnp.zeros((sub, LANES), BF16)
    blk = lambda x, i: x[i * sub:(i + 1) * sub]
    col = lambda parts: jnp.concatenate(parts, axis=0)
    ti = lax.broadcasted_iota(jnp.int32, (c, c), 0)
    si = lax.broadcasted_iota(jnp.int32, (c, c), 1)
    diag = ((ti // sub) == (si // sub)) & (si <= ti)
    nt = (((1,), (1,)), ((), ()))
    tn = (((0,), (0,)), ((), ()))

    scores = []
    for ci in range(nc):
        a32, a16 = m32[ci], m16[ci]
        q_off = jnp.concatenate([
            col([zb, zb, blk(a32, 2), blk(a32, 3)]),
            col([zb, blk(a16, 1), zb, zb]),
            col([zb, zb, zb, blk(a16, 3)])], axis=1)
        k_off = jnp.concatenate([
            col([blk(a32, 0), blk(a32, 1), zb, zb]),
            col([blk(a16, 0), zb, zb, zb]),
            col([zb, zb, blk(a16, 2), zb])], axis=1)
        s_off = lax.dot_general(q_off, k_off, nt, preferred_element_type=F32)
        s_dia = lax.dot_general(qed[ci], ked[ci], nt, preferred_element_type=F32)
        scores.append((s_off + jnp.where(diag, s_dia, 0.0)).astype(BF16))

    upd = [lax.dot_general(vb[ci], khat[ci], tn, preferred_element_type=F32)
           for ci in range(nc)]
    st = st_ref[unit]
    states = []
    for ci in range(nc):
        states.append(st.astype(BF16))
        st = dec[ci, 0:1, :] * st + upd[ci]
    st_ref[unit] = st

    outs = []
    for ci in range(nc):
        outs.append(lax.dot_general(qhat[ci], states[ci], nt, preferred_element_type=F32)
                    + jnp.dot(scores[ci], vb[ci], preferred_element_type=F32))
    o = jnp.concatenate(outs, axis=0)
    ms = jnp.mean(o * o, axis=-1, keepdims=True)
    return o * lax.rsqrt(ms + EPS) * gn * (zg * _sigmoid(zg))


def _scan_rows(a, b, carry):
    n = a.shape[0] // SUBLANES
    a3 = a.reshape(n, SUBLANES, LANES)
    b3 = b.reshape(n, SUBLANES, LANES)
    row = lax.broadcasted_iota(jnp.int32, a3.shape, 1)
    d = 1
    while d < SUBLANES:
        m = row >= d
        b3 = jnp.where(m, b3 + a3 * pltpu.roll(b3, d, axis=1), b3)
        a3 = jnp.where(m, a3 * pltpu.roll(a3, d, axis=1), a3)
        d *= 2
    es = []
    for s in range(n):
        e = b3[s] + a3[s] * carry
        es.append(e)
        carry = jnp.broadcast_to(e[SUBLANES - 1:SUBLANES, :], e.shape)
    return jnp.concatenate(es, axis=0), carry


def _lru_unit(zx, zgate, cw, cb, wax, ba, bx, lam, tail_ref, hc_ref, a_scr, u_scr, h_scr,
              x_scr, unit, slot):
    rr = zx.shape[0]
    nd = LRU_BLOCK_DIM
    x_scr[0:SUBLANES, :] = tail_ref[1 - slot, unit]
    x_scr[SUBLANES:, :] = zx
    xc = cb + cw[CONV_WIDTH - 1:CONV_WIDTH, :] * zx
    for j in range(CONV_WIDTH - 1):
        off = SUBLANES - (CONV_WIDTH - 1) + j
        xc = xc + cw[j:j + 1, :] * x_scr[pl.ds(off, rr), :]
    tail_ref[slot, unit] = zx[rr - SUBLANES:rr, :]

    nl = -lam
    sp = jnp.maximum(nl, 0.0) + jnp.log1p(jnp.exp(-jnp.abs(nl)))
    rx = jnp.dot(xc.astype(BF16), wax, preferred_element_type=F32)
    r = _sigmoid(rx[:, :nd] + ba)
    ig = _sigmoid(rx[:, nd:] + bx)
    log_a = r * ((-LRU_C) * sp)
    a = jnp.exp(log_a)
    u = jnp.sqrt(-jnp.tanh(log_a) * (a * a + 1.0)) * (ig * xc)

    ns = rr // SUBLANES
    a_scr[...] = a
    u_scr[...] = u
    slab_row = lambda ref, r: ref[pl.ds(r, ns, stride=SUBLANES), :]
    hz, az = [slab_row(u_scr, 0)], [slab_row(a_scr, 0)]
    for r in range(1, SUBLANES):
        ar = slab_row(a_scr, r)
        hz.append(ar * hz[-1] + slab_row(u_scr, r))
        az.append(ar * az[-1])
    carry0 = hc_ref[unit]
    ends, carry = _scan_rows(az[-1], hz[-1], carry0)
    hc_ref[unit] = carry
    first = lax.broadcasted_iota(jnp.int32, ends.shape, 0) == 0
    cin = jnp.where(first, carry0[0:1, :], pltpu.roll(ends, 1, axis=0))
    for r in range(SUBLANES):
        h_scr[pl.ds(r, ns, stride=SUBLANES), :] = hz[r] + az[r] * cin
    h = h_scr[...]
    return h * _gelu_tanh(zgate)


def _mixer_kernel(zq_ref, zf_ref, zv_ref, zg_ref, zx_ref, zgate_ref, lbr_ref, gn_ref,
                  cw_ref, cb_ref, wax_ref, ba_ref, bx_ref, lam_ref, s0_ref, s1_ref,
                  o_ref, c0_ref, c1_ref,
                  st_ref, tail_ref, hc_ref, lf_scr, p_scr, t_scr, a_scr, u_scr, h_scr, x_scr,
                  *, tiles_per_seq):
    unit = pl.program_id(1)
    slot = pl.program_id(0) % 2

    c0_ref[...] = s0_ref[...].astype(BF16)
    c1_ref[...] = s1_ref[...].astype(BF16)

    @pl.when(pl.program_id(0) % tiles_per_seq == 0)
    def _():
        st_ref[unit] = jnp.zeros(st_ref.shape[1:], F32)
        tail_ref[1 - slot, unit] = jnp.zeros(tail_ref.shape[2:], F32)
        hc_ref[unit] = jnp.zeros(hc_ref.shape[1:], F32)

    lbr = lbr_ref[...]
    ex = jnp.exp(lbr - jnp.max(lbr, axis=0, keepdims=True))
    lb = ex[0:1, :] / jnp.sum(ex, axis=0, keepdims=True)
    oh = _hgrn_unit(zq_ref[...], zf_ref[...], zv_ref[...], zg_ref[...], lb, gn_ref[...],
                    st_ref, lf_scr, p_scr, t_scr, unit)
    o_ref[:, :LANES] = oh.astype(o_ref.dtype)
    ol = _lru_unit(zx_ref[...], zgate_ref[...], cw_ref[...], cb_ref[...], wax_ref[0],
                   ba_ref[...], bx_ref[...], lam_ref[...], tail_ref, hc_ref,
                   a_scr, u_scr, h_scr, x_scr, unit, slot)
    o_ref[:, LANES:] = ol.astype(o_ref.dtype)


def _mixer(z, lower_bounds, g_norm, conv_w, conv_b, wax, b_a, b_x, lam, w_out, w_side, seq):
    t = z.shape[0]
    ts = MIX_TS
    nu = HGRN_HEADS
    assert nu == LRU_BLOCKS and HEAD_DIM == LRU_BLOCK_DIM == LANES and seq % ts == 0
    est = 6 * 2 * ts * LANES * 4 + 2 * ts * 2 * LANES * 2 + 7 * ts * LANES * 4 + 4 * (1 << 20)
    est += t * (HGRN_WIDTH + LRU_WIDTH) * 2

    def zspec(part):
        return pl.BlockSpec((ts, LANES), lambda i, j, part=part: (i, part * nu + j))

    def vec(rows):
        return pl.BlockSpec((rows, LANES), lambda i, j: (0, j))

    nsteps = (t // ts) * nu
    rb = w_out.shape[0] // nsteps
    per_group = LANES // rb
    assert rb * nsteps == w_out.shape[0] == w_side.shape[0] and per_group * rb == LANES

    def regrouped(i, j):
        dst = i * nu + j
        group, q = dst // per_group, dst % per_group
        src_group = jnp.where(group % 2 == 0, group // 2, HGRN_HEADS + group // 2)
        return (src_group * per_group + q, 0)

    straight = lambda i, j: (i * nu + j, 0)
    side_in = [pl.BlockSpec((rb, w_out.shape[1]), regrouped),
               pl.BlockSpec((rb, w_side.shape[1]), straight)]
    side_out = [pl.BlockSpec((rb, w_out.shape[1]), straight),
                pl.BlockSpec((rb, w_side.shape[1]), straight)]
    est += 2 * rb * (w_out.shape[1] + w_side.shape[1]) * (4 + 2)

    return pl.pallas_call(
        functools.partial(_mixer_kernel, tiles_per_seq=seq // ts),
        grid=(t // ts, nu),
        in_specs=[zspec(0), zspec(1), zspec(2), zspec(3), zspec(4), zspec(5),
                  vec(lower_bounds.shape[0]),
                  pl.BlockSpec((1, HEAD_DIM), lambda i, j: (0, 0)),
                  vec(CONV_WIDTH), vec(1),
                  pl.BlockSpec((1,) + wax.shape[1:], lambda i, j: (j, 0, 0)),
                  vec(1), vec(1), vec(1)] + side_in,
        out_specs=[pl.BlockSpec((ts, 2 * LANES), lambda i, j: (i, j))] + side_out,
        out_shape=[jax.ShapeDtypeStruct((t, HGRN_WIDTH + LRU_WIDTH), BF16),
                   jax.ShapeDtypeStruct(w_out.shape, BF16),
                   jax.ShapeDtypeStruct(w_side.shape, BF16)],
        scratch_shapes=[pltpu.VMEM((nu, HEAD_DIM, HEAD_DIM), F32),
                        pltpu.VMEM((2, nu, SUBLANES, LANES), F32),
                        pltpu.VMEM((nu, SUBLANES, LANES), F32)]
        + [pltpu.VMEM((ts, LANES), F32)] * 6 + [pltpu.VMEM((ts + SUBLANES, LANES), F32)],
        compiler_params=pltpu.CompilerParams(
            dimension_semantics=("arbitrary", "arbitrary"),
            vmem_limit_bytes=_vmem_limit(est)),
    )(z, z, z, z, z, z, lower_bounds, g_norm, conv_w, conv_b, wax, b_a, b_x, lam, w_out, w_side)


def _resident(shape):
    return pl.BlockSpec(shape, lambda i: (0,) * len(shape), pipeline_mode=pl.Buffered(1))


def _outproj_kernel(x_ref, o_ref_in, wb_ref, out_ref):
    out_ref[...] = x_ref[...] + jnp.dot(o_ref_in[...], wb_ref[...],
                                        preferred_element_type=F32)


def _out_proj(x, o, w_out):
    t, d = x.shape
    tm = ROW_TM
    k = o.shape[1]
    est = 2 * 2 * tm * d * 4 + 2 * tm * k * 2 + k * d * 2
    return pl.pallas_call(
        _outproj_kernel,
        grid=(t // tm,),
        in_specs=[
            pl.BlockSpec((tm, d), lambda i: (i, 0)),
            pl.BlockSpec((tm, k), lambda i: (i, 0)),
            _resident(w_out.shape),
        ],
        out_specs=pl.BlockSpec((tm, d), lambda i: (i, 0)),
        out_shape=jax.ShapeDtypeStruct((t, d), F32),
        compiler_params=pltpu.CompilerParams(
            dimension_semantics=("arbitrary",),
            vmem_limit_bytes=_vmem_limit(est)),
    )(x, o, w_out)


def _ple_kernel(x_ref, p_ref, gp_ref, wgb_ref, bg_ref, wp_ref, gf_ref, o_ref, wpb_ref):
    @pl.when(pl.program_id(0) == 0)
    def _():
        wpb_ref[...] = wp_ref[...].astype(BF16)

    x = x_ref[...]
    h = _rmsnorm(x, gp_ref[...]).astype(BF16)
    gate = jax.nn.sigmoid(
        jnp.dot(h, wgb_ref[...], preferred_element_type=F32) + bg_ref[...])
    emb = jnp.dot(p_ref[...].astype(BF16), wpb_ref[...], preferred_element_type=F32)
    o_ref[...] = _rmsnorm(x + gate * emb, gf_ref[...])


def _ple(x, p, g_ple, w_gate, b_gate, w_proj, g_final):
    t, d = x.shape
    pd = p.shape[1]
    tm = ROW_TM
    est = 2 * 2 * tm * d * 4 + 2 * tm * pd * 4 + d * d * 2 + pd * d * (4 + 2) + 2 * tm * d * 4
    row = pl.BlockSpec((1, d), lambda i: (0, 0))
    return pl.pallas_call(
        _ple_kernel,
        grid=(t // tm,),
        in_specs=[
            pl.BlockSpec((tm, d), lambda i: (i, 0)),
            pl.BlockSpec((tm, pd), lambda i: (i, 0)),
            row,
            _resident(w_gate.shape),
            row,
            _resident(w_proj.shape),
            row,
        ],
        out_specs=pl.BlockSpec((tm, d), lambda i: (i, 0)),
        out_shape=jax.ShapeDtypeStruct((t, d), F32),
        scratch_shapes=[pltpu.VMEM(w_proj.shape, BF16)],
        compiler_params=pltpu.CompilerParams(
            dimension_semantics=("arbitrary",),
            vmem_limit_bytes=_vmem_limit(est)),
    )(x, p, g_ple, w_gate, b_gate, w_proj, g_final)


def kernel(x, p, ffn1_norm, ffn1_w_gate, ffn1_w_up, ffn1_w_down, mix_norm, w_in,
           hgrn_lower_bounds, hgrn_g_norm, conv_w, conv_b, lru_w_a, lru_b_a, lru_w_x,
           lru_b_x, lru_lambda, w_out, ffn2_norm, ffn2_w_gate, ffn2_w_up, ffn2_w_down,
           ple_norm, ple_w_gate, ple_b_gate, ple_w_proj, final_norm):
    batch, seq, d = x.shape
    t = batch * seq
    depth = ffn1_norm.shape[0]
    assert depth == 1, "the shared lower-bound cumsum is specialised to one layer"
    l = 0
    bf = lambda a: a.astype(BF16)
    xt = x.reshape(t, d)

    xt = _ffn(xt, ffn1_norm[l][None], ffn1_w_gate[l], ffn1_w_up[l], ffn1_w_down[l])

    z = _in_proj(xt, mix_norm[l][None], w_in[l])
    wax = bf(jnp.concatenate([lru_w_a[l], lru_w_x[l]], axis=-1))
    mixed, w_out_b, w_gate_b = _mixer(
        z, hgrn_lower_bounds, hgrn_g_norm[l][None], conv_w[l], conv_b[l][None], wax,
        lru_b_a[l][None], lru_b_x[l][None], lru_lambda[l][None], w_out[l], ple_w_gate[l], seq)
    xt = _out_proj(xt, mixed, w_out_b)

    xt = _ffn(xt, ffn2_norm[l][None], ffn2_w_gate[l], ffn2_w_up[l], ffn2_w_down[l])

    out = _ple(xt, p[l].reshape(t, -1), ple_norm[l][None], w_gate_b,
               ple_b_gate[l][None], ple_w_proj[l], final_norm[None])
    return out.reshape(batch, seq, d)
```

```python
import functools

import jax
import jax.numpy as jnp
from jax import lax
from jax.experimental import pallas as pl
from jax.experimental.pallas import tpu as pltpu

F32 = jnp.float32
BF16 = jnp.bfloat16

D_MODEL = 2048
D_FF = 5632
PLE_DIM = 256
HGRN_WIDTH = D_MODEL // 2
HEAD_DIM = 128
HGRN_HEADS = HGRN_WIDTH // HEAD_DIM
LRU_WIDTH = D_MODEL - HGRN_WIDTH
LRU_BLOCKS = 8
LRU_BLOCK_DIM = LRU_WIDTH // LRU_BLOCKS
CONV_WIDTH = 4
LRU_C = 8.0
EPS = 1e-6
LOG2_E = 1.4426950408889634
SQRT_2_OVER_PI = 0.7978845608028654

V7X_VMEM_BYTES = 64 * 1024 * 1024
SUBLANES = 8
LANES = 128

FFN_TM = 1024
FFN_TF = 512
FFN_SUB = 256
PROJ_TM = 2048
PROJ_TN = 768
PROJ_SUB = 256
ROW_TM = 512
HGRN_CHUNK = 64
HGRN_SUB = 16
MIX_TS = 2048


def _vmem_limit(nbytes):
    return int(min(V7X_VMEM_BYTES - (2 << 20), nbytes + nbytes // 4 + (2 << 20)))


def _sigmoid(x):
    return 1.0 / (1.0 + jnp.exp2(x * (-LOG2_E)))


def _gelu_tanh(x):
    k0 = -2.0 * SQRT_2_OVER_PI * LOG2_E
    return x / (1.0 + jnp.exp2(x * (k0 + (k0 * 0.044715) * (x * x))))


def _rmsnorm(xf, g):
    ms = jnp.mean(xf * xf, axis=-1, keepdims=True)
    return xf * lax.rsqrt(ms + EPS) * g


def _stream_row_tile(x_hbm, x_ref, x_sem):
    i, j = pl.program_id(0), pl.program_id(1)
    tm = x_ref.shape[0]

    def x_copy(tile):
        return pltpu.make_async_copy(x_hbm.at[pl.ds(tile * tm, tm), :], x_ref, x_sem)

    @pl.when((i == 0) & (j == 0))
    def _():
        x_copy(0).start()

    @pl.when(j == 0)
    def _():
        x_copy(i).wait()

    @pl.when((j == pl.num_programs(1) - 2) & (i + 1 < pl.num_programs(0)))
    def _():
        x_copy(i + 1).start()


def _ffn_kernel(x_hbm, g_ref, wg_ref, wu_ref, wd_ref, o_ref, h_ref, x_ref, x_sem):
    j = pl.program_id(1)
    _stream_row_tile(x_hbm, x_ref, x_sem)

    def step(first):
        if first:
            h_ref[...] = _rmsnorm(x_ref[...], g_ref[...]).astype(BF16)
        h = h_ref[...]
        acts = []
        for k in range(wg_ref.shape[1] // FFN_SUB):
            cols = pl.ds(k * FFN_SUB, FFN_SUB)
            gate = jnp.dot(h, wg_ref[:, cols].astype(BF16), preferred_element_type=F32)
            up = jnp.dot(h, wu_ref[:, cols].astype(BF16), preferred_element_type=F32)
            acts.append((0.5 * (gate * jax.nn.sigmoid(gate) * up)).astype(BF16))
        act = jnp.concatenate(acts, axis=1)
        part = jnp.dot(act, wd_ref[...].astype(BF16), preferred_element_type=F32)
        if first:
            o_ref[...] = x_ref[...] + part
        else:
            o_ref[...] += part

    pl.when(j == 0)(functools.partial(step, True))
    pl.when(j > 0)(functools.partial(step, False))


def _ffn(x, g, wg, wu, wd):
    t, d = x.shape
    dff = wg.shape[1]
    tm, tf = FFN_TM, FFN_TF
    assert dff // tf >= 3, "the next x tile is fetched in the second-to-last column step"
    wbytes = wg.dtype.itemsize
    est = (2 * tm * d * 4) + tm * d * 4 + tm * d * 2 + 3 * d * tf * (2 * wbytes) + 3 * d * FFN_SUB * 2 \
        + 3 * tm * FFN_SUB * 4
    return pl.pallas_call(
        _ffn_kernel,
        grid=(t // tm, dff // tf),
        in_specs=[
            pl.BlockSpec(memory_space=pl.ANY),
            pl.BlockSpec((1, d), lambda i, j: (0, 0)),
            pl.BlockSpec((d, tf), lambda i, j: (0, j)),
            pl.BlockSpec((d, tf), lambda i, j: (0, j)),
            pl.BlockSpec((tf, d), lambda i, j: (j, 0)),
        ],
        out_specs=pl.BlockSpec((tm, d), lambda i, j: (i, 0)),
        out_shape=jax.ShapeDtypeStruct((t, d), F32),
        scratch_shapes=[pltpu.VMEM((tm, d), BF16), pltpu.VMEM((tm, d), F32),
                        pltpu.SemaphoreType.DMA(())],
        compiler_params=pltpu.CompilerParams(
            dimension_semantics=("arbitrary", "arbitrary"),
            vmem_limit_bytes=_vmem_limit(est)),
    )(x, g, wg, wu, wd)


def _proj_kernel(x_hbm, g_ref, w_ref, o_ref, h_ref, x_ref, x_sem):
    _stream_row_tile(x_hbm, x_ref, x_sem)

    def step(first):
        if first:
            h_ref[...] = _rmsnorm(x_ref[...], g_ref[...]).astype(BF16)
        h = h_ref[...]
        for k in range(w_ref.shape[1] // PROJ_SUB):
            cols = pl.ds(k * PROJ_SUB, PROJ_SUB)
            o_ref[:, cols] = jnp.dot(h, w_ref[:, cols].astype(BF16),
                                     preferred_element_type=F32)

    j = pl.program_id(1)
    pl.when(j == 0)(functools.partial(step, True))
    pl.when(j > 0)(functools.partial(step, False))


def _in_proj(x, g, w):
    t, d = x.shape
    n = w.shape[1]
    tm, tn = PROJ_TM, PROJ_TN
    assert n // tn >= 3, "the next x tile is fetched in the second-to-last column step"
    est = tm * d * 4 + tm * d * 2 + d * tn * 2 * w.dtype.itemsize + d * PROJ_SUB * 2 + 2 * tm * tn * 4
    return pl.pallas_call(
        _proj_kernel,
        grid=(t // tm, n // tn),
        in_specs=[
            pl.BlockSpec(memory_space=pl.ANY),
            pl.BlockSpec((1, d), lambda i, j: (0, 0)),
            pl.BlockSpec((d, tn), lambda i, j: (0, j)),
        ],
        out_specs=pl.BlockSpec((tm, tn), lambda i, j: (i, j)),
        out_shape=jax.ShapeDtypeStruct((t, n), F32),
        scratch_shapes=[pltpu.VMEM((tm, d), BF16), pltpu.VMEM((tm, d), F32),
                        pltpu.SemaphoreType.DMA(())],
        compiler_params=pltpu.CompilerParams(
            dimension_semantics=("arbitrary", "arbitrary"),
            vmem_limit_bytes=_vmem_limit(est)),
    )(x, g, w)


def _hgrn_unit(zq, zf, zv, zg, lb, gn, st_ref, lf_scr, p_scr, t_scr, unit):
    c, sub = HGRN_CHUNK, HGRN_SUB
    rr = zq.shape[0]
    nc = rr // c
    nv = c // SUBLANES
    assert c == 4 * sub and sub == 2 * SUBLANES
    shp4 = (nc, nv, SUBLANES, LANES)
    to4 = lambda x: x.reshape(shp4)
    f = lb + (1.0 - lb) * _sigmoid(zf)
    lf = jnp.log2(f)
    kk = to4(1.0 - f)
    q = to4(zq)

    nvr = rr // SUBLANES
    vreg_row = lambda r: pl.ds(r, nvr, stride=SUBLANES)
    lf_scr[...] = lf
    acc = lf_scr[vreg_row(0), :]
    p_scr[vreg_row(0), :] = acc
    for r in range(1, SUBLANES):
        acc = acc + lf_scr[vreg_row(r), :]
        p_scr[vreg_row(r), :] = acc
    for r in range(SUBLANES):
        t_scr[vreg_row(r), :] = acc
    p8 = to4(p_scr[...])
    t8 = to4(t_scr[...])
    s8 = t8 - p8
    tv = [t8[:, v] for v in range(nv)]
    pv = [p8[:, v] for v in range(nv)]
    sv = [s8[:, v] for v in range(nv)]
    p16 = [pv[v] if v % 2 == 0 else pv[v] + tv[v - 1] for v in range(nv)]
    s16 = [sv[v] + tv[v + 1] if v % 2 == 0 else sv[v] for v in range(nv)]
    t16 = [tv[2 * i] + tv[2 * i + 1] for i in range(nv // 2)]
    p32 = [p16[v] if (v // 2) % 2 == 0 else p16[v] + t16[v // 2 - 1] for v in range(nv)]
    s32 = [s16[v] + t16[v // 2 + 1] if (v // 2) % 2 == 0 else s16[v] for v in range(nv)]
    t32 = [t16[0] + t16[1], t16[2] + t16[3]]
    p64 = [p32[v] if v < nv // 2 else p32[v] + t32[0] for v in range(nv)]
    s64 = [s32[v] + t32[1] if v < nv // 2 else s32[v] for v in range(nv)]
    t64 = t32[0] + t32[1]

    st4 = lambda parts: jnp.stack(parts, axis=1)
    x32 = st4([s32[v] if v < nv // 2 else p32[v] for v in range(nv)])
    x16 = st4([s16[v] if (v // 2) % 2 == 0 else p16[v] for v in range(nv)])
    xd = st4([-sv[v] if v % 2 == 0 else pv[v] for v in range(nv)])
    qk32 = st4([kk[:, v] if v < nv // 2 else q[:, v] for v in range(nv)])
    qk16 = st4([kk[:, v] if (v // 2) % 2 == 0 else q[:, v] for v in range(nv)])

    to3 = lambda x: x.reshape(nc, c, LANES)
    m32 = to3(qk32 * jnp.exp2(x32)).astype(BF16)
    m16 = to3(qk16 * jnp.exp2(x16)).astype(BF16)
    qed = to3(q * jnp.exp2(xd)).astype(BF16)
    ked = to3(kk * jnp.exp2(-xd)).astype(BF16)
    qhat = to3(q * jnp.exp2(st4(p64))).astype(BF16)
    khat = to3(kk * jnp.exp2(st4(s64))).astype(BF16)
    dec = jnp.exp2(t64)
    vb = zv.reshape(nc, c, LANES).astype(BF16)

    zb = jnp.zeros((sub, LANES), BF16)
    blk = lambda x, i: x[i * sub:(i + 1) * sub]
    col = lambda parts: jnp.concatenate(parts, axis=0)
    ti = lax.broadcasted_iota(jnp.int32, (c, c), 0)
    si = lax.broadcasted_iota(jnp.int32, (c, c), 1)
    diag = ((ti // sub) == (si // sub)) & (si <= ti)
    nt = (((1,), (1,)), ((), ()))
    tn = (((0,), (0,)), ((), ()))

    scores = []
    for ci in range(nc):
        a32, a16 = m32[ci], m16[ci]
        q_off = jnp.concatenate([
            col([zb, zb, blk(a32, 2), blk(a32, 3)]),
            col([zb, blk(a16, 1), zb, zb]),
            col([zb, zb, zb, blk(a16, 3)])], axis=1)
        k_off = jnp.concatenate([
            col([blk(a32, 0), blk(a32, 1), zb, zb]),
            col([blk(a16, 0), zb, zb, zb]),
            col([zb, zb, blk(a16, 2), zb])], axis=1)
        s_off = lax.dot_general(q_off, k_off, nt, preferred_element_type=F32)
        s_dia = lax.dot_general(qed[ci], ked[ci], nt, preferred_element_type=F32)
        scores.append((s_off + jnp.where(diag, s_dia, 0.0)).astype(BF16))

    upd = [lax.dot_general(vb[ci], khat[ci], tn, preferred_element_type=F32)
           for ci in range(nc)]
    st = st_ref[unit]
    states = []
    for ci in range(nc):
        states.append(st.astype(BF16))
        st = dec[ci, 0:1, :] * st + upd[ci]
    st_ref[unit] = st

    outs = []
    for ci in range(nc):
        outs.append(lax.dot_general(qhat[ci], states[ci], nt, preferred_element_type=F32)
                    + jnp.dot(scores[ci], vb[ci], preferred_element_type=F32))
    o = jnp.concatenate(outs, axis=0)
    ms = jnp.mean(o * o, axis=-1, keepdims=True)
    return o * lax.rsqrt(ms + EPS) * gn * (zg * _sigmoid(zg))


def _scan_rows(a, b, carry):
    n = a.shape[0] // SUBLANES
    a3 = a.reshape(n, SUBLANES, LANES)
    b3 = b.reshape(n, SUBLANES, LANES)
    row = lax.broadcasted_iota(jnp.int32, a3.shape, 1)
    d = 1
    while d < SUBLANES:
        m = row >= d
        b3 = jnp.where(m, b3 + a3 * pltpu.roll(b3, d, axis=1), b3)
        a3 = jnp.where(m, a3 * pltpu.roll(a3, d, axis=1), a3)
        d *= 2
    es = []
    for s in range(n):
        e = b3[s] + a3[s] * carry
        es.append(e)
        carry = jnp.broadcast_to(e[SUBLANES - 1:SUBLANES, :], e.shape)
    return jnp.concatenate(es, axis=0), carry


def _lru_unit(zx, zgate, cw, cb, wax, ba, bx, lam, tail_ref, hc_ref, a_scr, u_scr, h_scr,
              x_scr, unit, slot):
    rr = zx.shape[0]
    nd = LRU_BLOCK_DIM
    x_scr[0:SUBLANES, :] = tail_ref[1 - slot, unit]
    x_scr[SUBLANES:, :] = zx
    xc = cb + cw[CONV_WIDTH - 1:CONV_WIDTH, :] * zx
    for j in range(CONV_WIDTH - 1):
        off = SUBLANES - (CONV_WIDTH - 1) + j
        xc = xc + cw[j:j + 1, :] * x_scr[pl.ds(off, rr), :]
    tail_ref[slot, unit] = zx[rr - SUBLANES:rr, :]

    nl = -lam
    sp = jnp.maximum(nl, 0.0) + jnp.log1p(jnp.exp(-jnp.abs(nl)))
    rx = jnp.dot(xc.astype(BF16), wax, preferred_element_type=F32)
    r = _sigmoid(rx[:, :nd] + ba)
    ig = _sigmoid(rx[:, nd:] + bx)
    log_a = r * ((-LRU_C) * sp)
    a = jnp.exp(log_a)
    u = jnp.sqrt(-jnp.tanh(log_a) * (a * a + 1.0)) * (ig * xc)

    ns = rr // SUBLANES
    a_scr[...] = a
    u_scr[...] = u
    slab_row = lambda ref, r: ref[pl.ds(r, ns, stride=SUBLANES), :]
    hz, az = [slab_row(u_scr, 0)], [slab_row(a_scr, 0)]
    for r in range(1, SUBLANES):
        ar = slab_row(a_scr, r)
        hz.append(ar * hz[-1] + slab_row(u_scr, r))
        az.append(ar * az[-1])
    carry0 = hc_ref[unit]
    ends, carry = _scan_rows(az[-1], hz[-1], carry0)
    hc_ref[unit] = carry
    first = lax.broadcasted_iota(jnp.int32, ends.shape, 0) == 0
    cin = jnp.where(first, carry0[0:1, :], pltpu.roll(ends, 1, axis=0))
    for r in range(SUBLANES):
        h_scr[pl.ds(r, ns, stride=SUBLANES), :] = hz[r] + az[r] * cin
    h = h_scr[...]
    return h * _gelu_tanh(zgate)


def _mixer_kernel(zq_ref, zf_ref, zv_ref, zg_ref, zx_ref, zgate_ref, lbr_ref, gn_ref,
                  cw_ref, cb_ref, wax_ref, ba_ref, bx_ref, lam_ref, s0_ref, s1_ref,
                  o_ref, c0_ref, c1_ref,
                  st_ref, tail_ref, hc_ref, lf_scr, p_scr, t_scr, a_scr, u_scr, h_scr, x_scr,
                  *, tiles_per_seq):
    unit = pl.program_id(1)
    slot = pl.program_id(0) % 2

    c0_ref[...] = s0_ref[...].astype(BF16)
    c1_ref[...] = s1_ref[...].astype(BF16)

    @pl.when(pl.program_id(0) % tiles_per_seq == 0)
    def _():
        st_ref[unit] = jnp.zeros(st_ref.shape[1:], F32)
        tail_ref[1 - slot, unit] = jnp.zeros(tail_ref.shape[2:], F32)
        hc_ref[unit] = jnp.zeros(hc_ref.shape[1:], F32)

    lbr = lbr_ref[...]
    ex = jnp.exp(lbr - jnp.max(lbr, axis=0, keepdims=True))
    lb = ex[0:1, :] / jnp.sum(ex, axis=0, keepdims=True)
    oh = _hgrn_unit(zq_ref[...], zf_ref[...], zv_ref[...], zg_ref[...], lb, gn_ref[...],
                    st_ref, lf_scr, p_scr, t_scr, unit)
    o_ref[:, :LANES] = oh.astype(o_ref.dtype)
    ol = _lru_unit(zx_ref[...], zgate_ref[...], cw_ref[...], cb_ref[...], wax_ref[0],
                   ba_ref[...], bx_ref[...], lam_ref[...], tail_ref, hc_ref,
                   a_scr, u_scr, h_scr, x_scr, unit, slot)
    o_ref[:, LANES:] = ol.astype(o_ref.dtype)


def _mixer(z, lower_bounds, g_norm, conv_w, conv_b, wax, b_a, b_x, lam, w_out, w_side, seq):
    t = z.shape[0]
    ts = MIX_TS
    nu = HGRN_HEADS
    assert nu == LRU_BLOCKS and HEAD_DIM == LRU_BLOCK_DIM == LANES and seq % ts == 0
    est = 6 * 2 * ts * LANES * 4 + 2 * ts * 2 * LANES * 2 + 7 * ts * LANES * 4 + 4 * (1 << 20)
    est += t * (HGRN_WIDTH + LRU_WIDTH) * 2

    def zspec(part):
        return pl.BlockSpec((ts, LANES), lambda i, j, part=part: (i, part * nu + j))

    def vec(rows):
        return pl.BlockSpec((rows, LANES), lambda i, j: (0, j))

    nsteps = (t // ts) * nu
    rb = w_out.shape[0] // nsteps
    per_group = LANES // rb
    assert rb * nsteps == w_out.shape[0] == w_side.shape[0] and per_group * rb == LANES

    def regrouped(i, j):
        dst = i * nu + j
        group, q = dst // per_group, dst % per_group
        src_group = jnp.where(group % 2 == 0, group // 2, HGRN_HEADS + group // 2)
        return (src_group * per_group + q, 0)

    straight = lambda i, j: (i * nu + j, 0)
    side_in = [pl.BlockSpec((rb, w_out.shape[1]), regrouped),
               pl.BlockSpec((rb, w_side.shape[1]), straight)]
    side_out = [pl.BlockSpec((rb, w_out.shape[1]), straight),
                pl.BlockSpec((rb, w_side.shape[1]), straight)]
    est += 2 * rb * (w_out.shape[1] + w_side.shape[1]) * (4 + 2)

    return pl.pallas_call(
        functools.partial(_mixer_kernel, tiles_per_seq=seq // ts),
        grid=(t // ts, nu),
        in_specs=[zspec(0), zspec(1), zspec(2), zspec(3), zspec(4), zspec(5),
                  vec(lower_bounds.shape[0]),
                  pl.BlockSpec((1, HEAD_DIM), lambda i, j: (0, 0)),
                  vec(CONV_WIDTH), vec(1),
                  pl.BlockSpec((1,) + wax.shape[1:], lambda i, j: (j, 0, 0)),
                  vec(1), vec(1), vec(1)] + side_in,
        out_specs=[pl.BlockSpec((ts, 2 * LANES), lambda i, j: (i, j))] + side_out,
        out_shape=[jax.ShapeDtypeStruct((t, HGRN_WIDTH + LRU_WIDTH), BF16),
                   jax.ShapeDtypeStruct(w_out.shape, BF16),
                   jax.ShapeDtypeStruct(w_side.shape, BF16)],
        scratch_shapes=[pltpu.VMEM((nu, HEAD_DIM, HEAD_DIM), F32),
                        pltpu.VMEM((2, nu, SUBLANES, LANES), F32),
                        pltpu.VMEM((nu, SUBLANES, LANES), F32)]
        + [pltpu.VMEM((ts, LANES), F32)] * 6 + [pltpu.VMEM((ts + SUBLANES, LANES), F32)],
        compiler_params=pltpu.CompilerParams(
            dimension_semantics=("arbitrary", "arbitrary"),
            vmem_limit_bytes=_vmem_limit(est)),
    )(z, z, z, z, z, z, lower_bounds, g_norm, conv_w, conv_b, wax, b_a, b_x, lam, w_out, w_side)


def _resident(shape):
    return pl.BlockSpec(shape, lambda i: (0,) * len(shape), pipeline_mode=pl.Buffered(1))


def _outproj_kernel(x_ref, o_ref_in, wb_ref, out_ref):
    out_ref[...] = x_ref[...] + jnp.dot(o_ref_in[...], wb_ref[...],
                                        preferred_element_type=F32)


def _out_proj(x, o, w_out):
    t, d = x.shape
    tm = ROW_TM
    k = o.shape[1]
    est = 2 * 2 * tm * d * 4 + 2 * tm * k * 2 + k * d * 2
    return pl.pallas_call(
        _outproj_kernel,
        grid=(t // tm,),
        in_specs=[
            pl.BlockSpec((tm, d), lambda i: (i, 0)),
            pl.BlockSpec((tm, k), lambda i: (i, 0)),
            _resident(w_out.shape),
        ],
        out_specs=pl.BlockSpec((tm, d), lambda i: (i, 0)),
        out_shape=jax.ShapeDtypeStruct((t, d), F32),
        compiler_params=pltpu.CompilerParams(
            dimension_semantics=("arbitrary",),
            vmem_limit_bytes=_vmem_limit(est)),
    )(x, o, w_out)


def _ple_kernel(x_ref, p_ref, gp_ref, wgb_ref, bg_ref, wp_ref, gf_ref, o_ref, wpb_ref):
    @pl.when(pl.program_id(0) == 0)
    def _():
        wpb_ref[...] = wp_ref[...].astype(BF16)

    x = x_ref[...]
    h = _rmsnorm(x, gp_ref[...]).astype(BF16)
    gate = jax.nn.sigmoid(
        jnp.dot(h, wgb_ref[...], preferred_element_type=F32) + bg_ref[...])
    emb = jnp.dot(p_ref[...].astype(BF16), wpb_ref[...], preferred_element_type=F32)
    o_ref[...] = _rmsnorm(x + gate * emb, gf_ref[...])


def _ple(x, p, g_ple, w_gate, b_gate, w_proj, g_final):
    t, d = x.shape
    pd = p.shape[1]
    tm = ROW_TM
    est = 2 * 2 * tm * d * 4 + 2 * tm * pd * 4 + d * d * 2 + pd * d * (4 + 2) + 2 * tm * d * 4
    row = pl.BlockSpec((1, d), lambda i: (0, 0))
    return pl.pallas_call(
        _ple_kernel,
        grid=(t // tm,),
        in_specs=[
            pl.BlockSpec((tm, d), lambda i: (i, 0)),
            pl.BlockSpec((tm, pd), lambda i: (i, 0)),
            row,
            _resident(w_gate.shape),
            row,
            _resident(w_proj.shape),
            row,
        ],
        out_specs=pl.BlockSpec((tm, d), lambda i: (i, 0)),
        out_shape=jax.ShapeDtypeStruct((t, d), F32),
        scratch_shapes=[pltpu.VMEM(w_proj.shape, BF16)],
        compiler_params=pltpu.CompilerParams(
            dimension_semantics=("arbitrary",),
            vmem_limit_bytes=_vmem_limit(est)),
    )(x, p, g_ple, w_gate, b_gate, w_proj, g_final)


def kernel(x, p, ffn1_norm, ffn1_w_gate, ffn1_w_up, ffn1_w_down, mix_norm, w_in,
           hgrn_lower_bounds, hgrn_g_norm, conv_w, conv_b, lru_w_a, lru_b_a, lru_w_x,
           lru_b_x, lru_lambda, w_out, ffn2_norm, ffn2_w_gate, ffn2_w_up, ffn2_w_down,
           ple_norm, ple_w_gate, ple_b_gate, ple_w_proj, final_norm):
    batch, seq, d = x.shape
    t = batch * seq
    depth = ffn1_norm.shape[0]
    assert depth == 1, "the shared lower-bound cumsum is specialised to one layer"
    l = 0
    bf = lambda a: a.astype(BF16)
    xt = x.reshape(t, d)

    xt = _ffn(xt, ffn1_norm[l][None], ffn1_w_gate[l], ffn1_w_up[l], ffn1_w_down[l])

    z = _in_proj(xt, mix_norm[l][None], w_in[l])
    wax = bf(jnp.concatenate([lru_w_a[l], lru_w_x[l]], axis=-1))
    mixed, w_out_b, w_gate_b = _mixer(
        z, hgrn_lower_bounds, hgrn_g_norm[l][None], conv_w[l], conv_b[l][None], wax,
        lru_b_a[l][None], lru_b_x[l][None], lru_lambda[l][None], w_out[l], ple_w_gate[l], seq)
    xt = _out_proj(xt, mixed, w_out_b)

    xt = _ffn(xt, ffn2_norm[l][None], ffn2_w_gate[l], ffn2_w_up[l], ffn2_w_down[l])

    out = _ple(xt, p[l].reshape(t, -1), ple_norm[l][None], w_gate_b,
               ple_b_gate[l][None], ple_w_proj[l], final_norm[None])
    return out.reshape(batch, seq, d)
```
